```python
import jax, jax.numpy as jnp
from jax import lax
import numpy as np

D_MODEL = 2048
BATCH = 8
SEQ = 2048
DEPTH = 2
DEC_BATCH = 128
DEC_SEQ = 4
PAST_LEN = 8192
PAGE_SIZE = 128

N_A_LAYERS = DEPTH // 2
N_B_LAYERS = DEPTH - N_A_LAYERS
D_RNN = D_MODEL
N_LRU_BLOCKS = 16
LRU_BLOCK = D_RNN // N_LRU_BLOCKS
CONV_W = 4
LRU_C = 8.0
N_HEADS = 32
N_KV_HEADS = 4
HEAD_DIM = 64
GROUP = N_HEADS // N_KV_HEADS
WINDOW = 128
ATTN_BLOCK = WINDOW
D_FF = 4 * D_MODEL
EPS = 1e-6

kernel_name = 'yoco_hawk_swa_sink_decoder_step'

F32 = jnp.float32


def rmsnorm(x, g):
    xf = x.astype(F32)
    y = xf * lax.rsqrt(jnp.mean(xf * xf, axis=-1, keepdims=True) + EPS)
    return (y * g.astype(F32)).astype(x.dtype)


def causal_conv(x, buf, w, b):
    T = x.shape[1]
    xp = jnp.concatenate([buf.astype(x.dtype), x], axis=1)
    y = b + w[0] * xp[:, 0:T]
    for j in range(1, CONV_W):
        y = y + w[j] * xp[:, j:j + T]
    return y, xp[:, xp.shape[1] - (CONV_W - 1):]


def block_diag(x, w, b):
    xb = x.reshape(x.shape[:-1] + (N_LRU_BLOCKS, LRU_BLOCK))
    y = jnp.einsum('btnc,ncd->btnd', xb, w) + b
    return y.reshape(x.shape)


def rg_lru(x, h0, wa, ba, wx, bx, lam):
    r = jax.nn.sigmoid(block_diag(x, wa, ba).astype(F32))
    i = jax.nn.sigmoid(block_diag(x, wx, bx).astype(F32))
    log_a = -LRU_C * r * jax.nn.softplus(-lam.astype(F32))
    a = jnp.exp(log_a)
    u = jnp.sqrt(-jnp.expm1(2.0 * log_a)) * (i * x.astype(F32))

    def step(h, au):
        a_t, u_t = au
        h = a_t * h + u_t
        return h, h

    hT, hs = lax.scan(step, h0.astype(F32), (jnp.swapaxes(a, 0, 1), jnp.swapaxes(u, 0, 1)))
    return jnp.swapaxes(hs, 0, 1).astype(x.dtype), hT.astype(h0.dtype)


def recurrent_block(xn, conv_buf, h0, w_in, conv_w, conv_b, wa, ba, wx, bx, lam, w_out):
    gate, xr = jnp.split(xn @ w_in, 2, axis=-1)
    xr, new_buf = causal_conv(xr, conv_buf, conv_w, conv_b)
    hs, hT = rg_lru(xr, h0, wa, ba, wx, bx, lam)
    y = (jax.nn.gelu(gate) * hs) @ w_out
    return y, new_buf, hT


def sink_attention(q, k, v, q_pos, k_pos, sinks):
    s = jnp.einsum('bnqhgd,bnkhd->bnhgqk', q, k, preferred_element_type=F32) * (HEAD_DIM ** -0.5)
    dq = q_pos[:, :, None] - k_pos[:, None, :]
    allowed = (dq >= 0) & (dq < WINDOW) & (k_pos[:, None, :] >= 0)
    s = jnp.where(allowed[None, :, None, None], s, -jnp.inf)
    sink = sinks.astype(F32).reshape(1, 1, N_KV_HEADS, GROUP, 1, 1)
    m = jnp.maximum(jnp.max(s, axis=-1, keepdims=True), sink)
    p = jnp.exp(s - m)
    p = p / (jnp.sum(p, axis=-1, keepdims=True) + jnp.exp(sink - m))
    return jnp.einsum('bnhgqk,bnkhd->bnqhgd', p.astype(v.dtype), v)


def attention_prompt(xn, k, v, w_q, sinks, w_o):
    B, T, _ = xn.shape
    nb = T // ATTN_BLOCK
    q = (xn @ w_q).reshape(B, nb, ATTN_BLOCK, N_KV_HEADS, GROUP, HEAD_DIM)
    kb = k.reshape(B, nb, ATTN_BLOCK, N_KV_HEADS, HEAD_DIM)
    vb = v.reshape(B, nb, ATTN_BLOCK, N_KV_HEADS, HEAD_DIM)
    pad = ((0, 0), (1, 0), (0, 0), (0, 0), (0, 0))
    kk = jnp.concatenate([jnp.pad(kb[:, :-1], pad), kb], axis=2)
    vv = jnp.concatenate([jnp.pad(vb[:, :-1], pad), vb], axis=2)
    pos = jnp.arange(T, dtype=jnp.int32).reshape(nb, ATTN_BLOCK)
    k_pos = jnp.concatenate([pos - ATTN_BLOCK, pos], axis=1)
    o = sink_attention(q, kk, vv, pos, k_pos, sinks)
    return o.reshape(B, T, N_HEADS * HEAD_DIM) @ w_o


def attention_sample(xn, k_new, v_new, k_cache, v_cache, w_q, sinks, w_o):
    B, T, _ = xn.shape
    n_buf = k_cache.shape[1]
    q = (xn @ w_q).reshape(B, 1, T, N_KV_HEADS, GROUP, HEAD_DIM)
    kk = jnp.concatenate([k_cache.astype(k_new.dtype), k_new], axis=1)[:, None]
    vv = jnp.concatenate([v_cache.astype(v_new.dtype), v_new], axis=1)[:, None]
    q_pos = PAST_LEN + jnp.arange(T, dtype=jnp.int32)
    k_pos = jnp.concatenate([PAST_LEN - n_buf + jnp.arange(n_buf, dtype=jnp.int32), q_pos])
    o = sink_attention(q, kk, vv, q_pos[None], k_pos[None], sinks)
    return o.reshape(B, T, N_HEADS * HEAD_DIM) @ w_o


def sq_relu_mlp(xn, w_up, w_down):
    return jnp.square(jax.nn.relu(xn @ w_up)) @ w_down


def trunk(x, conv_state, h_state, k_cache, v_cache, norm_mix, norm_mlp, rec_w_in, rec_conv_w, rec_conv_b,
          rec_gate_a_w, rec_gate_a_b, rec_gate_x_w, rec_gate_x_b, rec_lambda, rec_w_out, kv_norm, w_kv,
          attn_w_q, attn_sinks, attn_w_o, mlp_w_up, mlp_w_down, final_norm):
    B, T, _ = x.shape
    h = x
    new_conv, new_h = [], []
    k = v = None
    for layer in range(DEPTH):
        xn = rmsnorm(h, norm_mix[layer])
        if layer < N_A_LAYERS:
            y, cb, hT = recurrent_block(xn, conv_state[:, layer], h_state[:, layer], rec_w_in[layer],
                                        rec_conv_w[layer], rec_conv_b[layer], rec_gate_a_w[layer],
                                        rec_gate_a_b[layer], rec_gate_x_w[layer], rec_gate_x_b[layer],
                                        rec_lambda[layer], rec_w_out[layer])
            new_conv.append(cb)
            new_h.append(hT)
        else:
            j = layer - N_A_LAYERS
            if k_cache is None:
                y = attention_prompt(xn, k, v, attn_w_q[j], attn_sinks[j], attn_w_o[j])
            else:
                y = attention_sample(xn, k, v, k_cache, v_cache, attn_w_q[j], attn_sinks[j], attn_w_o[j])
        h = h + y
        h = h + sq_relu_mlp(rmsnorm(h, norm_mlp[layer]), mlp_w_up[layer], mlp_w_down[layer])
        if layer == N_A_LAYERS - 1:
            k, v = jnp.split(rmsnorm(h, kv_norm) @ w_kv, 2, axis=-1)
            k = k.reshape(B, T, N_KV_HEADS, HEAD_DIM)
            v = v.reshape(B, T, N_KV_HEADS, HEAD_DIM)
    out = rmsnorm(h, final_norm)
    return out, jnp.stack(new_conv, axis=1), jnp.stack(new_h, axis=1), k, v


def setup_inputs(seed: int = 0) -> dict:
    key = jax.random.key(seed)
    ks = jax.random.split(key, 32)
    nrm = lambda k, shape, scale: jax.random.normal(k, shape, F32) * scale
    p_a = jax.random.uniform(ks[14], (N_A_LAYERS, D_RNN), F32, 0.9, 0.999)
    return {
        'x_prompt': nrm(ks[0], (BATCH, SEQ, D_MODEL), 1.0),
        'x_sample': nrm(ks[1], (DEC_BATCH, DEC_SEQ, D_MODEL), 1.0),
        'state_conv': nrm(ks[2], (DEC_BATCH, N_A_LAYERS, CONV_W - 1, D_RNN), 1.0),
        'state_h': nrm(ks[3], (DEC_BATCH, N_A_LAYERS, D_RNN), 0.5),
        'cache_k': nrm(ks[4], (DEC_BATCH, WINDOW, N_KV_HEADS, HEAD_DIM), 1.0),
        'cache_v': nrm(ks[5], (DEC_BATCH, WINDOW, N_KV_HEADS, HEAD_DIM), 1.0),
        'norm_mix': 1.0 + nrm(ks[6], (DEPTH, D_MODEL), 0.05),
        'norm_mlp': 1.0 + nrm(ks[7], (DEPTH, D_MODEL), 0.05),
        'rec_w_in': nrm(ks[8], (N_A_LAYERS, D_MODEL, 2 * D_RNN), D_MODEL ** -0.5),
        'rec_conv_w': nrm(ks[9], (N_A_LAYERS, CONV_W, D_RNN), CONV_W ** -0.5),
        'rec_conv_b': nrm(ks[10], (N_A_LAYERS, D_RNN), 0.01),
        'rec_gate_a_w': nrm(ks[11], (N_A_LAYERS, N_LRU_BLOCKS, LRU_BLOCK, LRU_BLOCK), LRU_BLOCK ** -0.5),
        'rec_gate_a_b': nrm(ks[12], (N_A_LAYERS, N_LRU_BLOCKS, LRU_BLOCK), 0.01),
        'rec_gate_x_w': nrm(ks[13], (N_A_LAYERS, N_LRU_BLOCKS, LRU_BLOCK, LRU_BLOCK), LRU_BLOCK ** -0.5),
        'rec_gate_x_b': nrm(ks[15], (N_A_LAYERS, N_LRU_BLOCKS, LRU_BLOCK), 0.01),
        'rec_lambda': jnp.log(p_a) - jnp.log1p(-p_a),
        'rec_w_out': nrm(ks[16], (N_A_LAYERS, D_RNN, D_MODEL), D_RNN ** -0.5),
        'kv_norm': 1.0 + nrm(ks[17], (D_MODEL,), 0.05),
        'w_kv': nrm(ks[18], (D_MODEL, 2 * N_KV_HEADS * HEAD_DIM), D_MODEL ** -0.5),
        'attn_w_q': nrm(ks[19], (N_B_LAYERS, D_MODEL, N_HEADS * HEAD_DIM), D_MODEL ** -0.5),
        'attn_sinks': nrm(ks[20], (N_B_LAYERS, N_HEADS), 0.5),
        'attn_w_o': nrm(ks[21], (N_B_LAYERS, N_HEADS * HEAD_DIM, D_MODEL), (N_HEADS * HEAD_DIM) ** -0.5),
        'mlp_w_up': nrm(ks[22], (DEPTH, D_MODEL, D_FF), D_MODEL ** -0.5),
        'mlp_w_down': nrm(ks[23], (DEPTH, D_FF, D_MODEL), D_FF ** -0.5),
        'final_norm': 1.0 + nrm(ks[24], (D_MODEL,), 0.05),
    }


def reference(x_prompt, x_sample, state_conv, state_h, cache_k, cache_v, norm_mix, norm_mlp, rec_w_in,
              rec_conv_w, rec_conv_b, rec_gate_a_w, rec_gate_a_b, rec_gate_x_w, rec_gate_x_b, rec_lambda,
              rec_w_out, kv_norm, w_kv, attn_w_q, attn_sinks, attn_w_o, mlp_w_up, mlp_w_down, final_norm):
    weights = (norm_mix, norm_mlp, rec_w_in, rec_conv_w, rec_conv_b, rec_gate_a_w, rec_gate_a_b,
               rec_gate_x_w, rec_gate_x_b, rec_lambda, rec_w_out, kv_norm, w_kv, attn_w_q, attn_sinks,
               attn_w_o, mlp_w_up, mlp_w_down, final_norm)
    B, T, _ = x_prompt.shape
    conv0 = jnp.zeros((B, N_A_LAYERS, CONV_W - 1, D_RNN), x_prompt.dtype)
    h0 = jnp.zeros((B, N_A_LAYERS, D_RNN), state_h.dtype)
    y_prompt, conv_p, h_p, k_p, v_p = trunk(x_prompt, conv0, h0, None, None, *weights)
    keep = min(WINDOW, T)
    new_k_prompt = k_p[:, T - keep:]
    new_v_prompt = v_p[:, T - keep:]
    y_sample, conv_s, h_s, k_s, v_s = trunk(x_sample, state_conv, state_h, cache_k, cache_v, *weights)
    return (y_prompt, y_sample, conv_p, h_p, new_k_prompt, new_v_prompt, conv_s, h_s, k_s, v_s)
```

```python
import functools

import jax
import jax.numpy as jnp
from jax import lax
from jax.experimental import pallas as pl
from jax.experimental.pallas import tpu as pltpu

F32 = jnp.float32
BF16 = jnp.bfloat16

D_MODEL = 2048
D_RNN = 2048
N_LRU_BLOCKS = 16
LRU_BLOCK = D_RNN // N_LRU_BLOCKS
CONV_W = 4
LRU_C = 8.0
N_HEADS = 32
N_KV_HEADS = 4
HEAD_DIM = 64
GROUP = N_HEADS // N_KV_HEADS
KV_DIM = N_KV_HEADS * HEAD_DIM
WINDOW = 128
EPS = 1e-6

SUBLANES = 8
LANES = 128
VMEM_LIMIT_BYTES = 52 * 1024 * 1024


def _params(semantics):
    return pltpu.CompilerParams(dimension_semantics=semantics,
                                vmem_limit_bytes=VMEM_LIMIT_BYTES)


def _rmsnorm_body(x_ref, *refs, n_out):
    g_refs, o_refs = refs[:n_out], refs[n_out:]
    x = x_ref[...]
    y = x * lax.rsqrt(jnp.mean(x * x, axis=-1, keepdims=True) + EPS)
    for g_ref, o_ref in zip(g_refs, o_refs):
        o_ref[...] = (y * g_ref[...]).astype(o_ref.dtype)


def _rmsnorm(x, gains, out_dtype, tm=512):
    m, d = x.shape
    tm = min(tm, m)
    n_out = len(gains)
    row_spec = pl.BlockSpec((tm, d), lambda i: (i, 0))
    gain_spec = pl.BlockSpec((1, d), lambda i: (0, 0))
    outs = pl.pallas_call(
        functools.partial(_rmsnorm_body, n_out=n_out),
        grid=(m // tm,),
        in_specs=[row_spec] + [gain_spec] * n_out,
        out_specs=[row_spec] * n_out,
        out_shape=[jax.ShapeDtypeStruct((m, d), out_dtype)] * n_out,
        compiler_params=_params(("arbitrary",)),
        name="rmsnorm",
    )(x, *[g.reshape(1, d).astype(F32) for g in gains])
    return outs


def _matmul_body(*refs, nk, act, has_resid):
    x_ref, w_ref = refs[:2]
    pos = 2
    r_ref = None
    if has_resid:
        r_ref = refs[pos]
        pos += 1
    o_ref = refs[pos]
    acc_ref = refs[pos + 1] if nk > 1 else None

    part = jnp.dot(x_ref[...], w_ref[...], preferred_element_type=F32)

    def finish(acc):
        if act == "relu2":
            r = jnp.maximum(acc, 0.0)
            acc = r * r
        elif act == "qscale":
            acc = acc * (HEAD_DIM ** -0.5)
        if r_ref is not None:
            acc = r_ref[...] + acc
        o_ref[...] = acc.astype(o_ref.dtype)

    if nk == 1:
        finish(part)
    else:
        k = pl.program_id(2)

        @pl.when(k == 0)
        def _():
            acc_ref[...] = part

        @pl.when(jnp.logical_and(k > 0, k < nk - 1))
        def _():
            acc_ref[...] += part

        @pl.when(k == nk - 1)
        def _():
            finish(acc_ref[...] + part)


def _matmul(x, w, *, out_dtype, act=None, resid=None, bm=1024, bn=1024, bk=None, name="matmul"):
    m, kdim = x.shape
    _, n = w.shape
    bm, bn = min(bm, m), min(bn, n)
    bk = kdim if bk is None else min(bk, kdim)
    nk = kdim // bk
    in_specs = [pl.BlockSpec((bm, bk), lambda i, j, k: (i, k)),
                pl.BlockSpec((bk, bn), lambda i, j, k: (k, j))]
    args = [x, w]
    if resid is not None:
        in_specs.append(pl.BlockSpec((bm, bn), lambda i, j, k: (i, j)))
        args.append(resid)
    scratch = [pltpu.VMEM((bm, bn), F32)] if nk > 1 else []
    return pl.pallas_call(
        functools.partial(_matmul_body, nk=nk, act=act, has_resid=resid is not None),
        grid=(m // bm, n // bn, nk),
        in_specs=in_specs,
        out_specs=pl.BlockSpec((bm, bn), lambda i, j, k: (i, j)),
        out_shape=jax.ShapeDtypeStruct((m, n), out_dtype),
        scratch_shapes=scratch,
        compiler_params=_params(("arbitrary", "arbitrary", "arbitrary")),
        name=name,
    )(*args)


def _gelu_tanh(x):
    return x * (0.5 * (1.0 + jnp.tanh(0.7978845608028654 * (x + 0.044715 * (x * x * x)))))


def _softplus(z):
    return jnp.maximum(z, 0.0) + jnp.log1p(jnp.exp(-jnp.abs(z)))


def _lru_gates(xc, wg, bg, sp):
    pre = jnp.dot(xc.astype(BF16), wg, preferred_element_type=F32) + bg
    r = jax.nn.sigmoid(pre[:, :LRU_BLOCK])
    i = jax.nn.sigmoid(pre[:, LRU_BLOCK:])
    a = jnp.exp((-LRU_C) * r * sp)
    u = jnp.sqrt(1.0 - a * a) * (i * xc)
    return a, u


def _lru_prompt_body(gate_ref, xr_ref, cw_ref, cb_ref, wg_ref, bg_ref, lam_ref,
                     z_ref, ht_ref, h_sc, tail_sc, *, tc_rows, tc_cols):
    t = pl.program_id(2)
    n_groups = tc_rows // SUBLANES

    @pl.when(t == 0)
    def _():
        h_sc[...] = jnp.zeros_like(h_sc)
        tail_sc[...] = jnp.zeros_like(tail_sc)

    x = xr_ref[...]
    tail = tail_sc[...]
    row8 = lax.broadcasted_iota(jnp.int32, (SUBLANES, tc_cols), 0)

    def shifted(s):
        rolled = pltpu.roll(x, s, 0)
        head = jnp.where(row8 < s, pltpu.roll(tail, s, 0), rolled[0:SUBLANES])
        return jnp.concatenate([head, rolled[SUBLANES:]], axis=0)

    cw = cw_ref[...]
    xc = cb_ref[...] + cw[0:1] * shifted(3)
    xc = xc + cw[1:2] * shifted(2)
    xc = xc + cw[2:3] * shifted(1)
    xc = xc + cw[3:4] * x
    tail_sc[...] = x[tc_rows - SUBLANES:tc_rows]

    sp_all = _softplus(-lam_ref[...])
    sub = lax.broadcasted_iota(jnp.int32, (n_groups, SUBLANES, LRU_BLOCK), 1)
    for n in range(tc_cols // LRU_BLOCK):
        cols = slice(n * LRU_BLOCK, (n + 1) * LRU_BLOCK)
        a, u = _lru_gates(xc[:, cols], wg_ref[n], bg_ref[n], sp_all[:, cols])
        a3 = a.reshape(n_groups, SUBLANES, LRU_BLOCK)
        u3 = u.reshape(n_groups, SUBLANES, LRU_BLOCK)
        for s in (1, 2, 4):
            keep = sub >= s
            u3 = jnp.where(keep, a3 * pltpu.roll(u3, s, 1) + u3, u3)
            a3 = jnp.where(keep, a3 * pltpu.roll(a3, s, 1), a3)
        h = h_sc[:, cols]
        hs = []
        for g in range(n_groups):
            hg = a3[g] * h + u3[g]
            hs.append(hg)
            h = hg[SUBLANES - 1:SUBLANES]
        h_sc[:, cols] = h
        hs = jnp.concatenate(hs, axis=0)
        z_ref[:, cols] = (_gelu_tanh(gate_ref[:, cols]) * hs).astype(z_ref.dtype)

    @pl.when(t == pl.num_programs(2) - 1)
    def _():
        ht_ref[...] = h_sc[...]


def _lru_prompt(gx3, conv_w, conv_b, w_gates, b_gates, lam, *, tc_rows=512, tc_cols=512):
    b, t, _ = gx3.shape
    c = D_RNN
    ncb = c // tc_cols
    gpb = tc_cols // LRU_BLOCK
    z, ht = pl.pallas_call(
        functools.partial(_lru_prompt_body, tc_rows=tc_rows, tc_cols=tc_cols),
        grid=(b, ncb, t // tc_rows),
        in_specs=[
            pl.BlockSpec((None, tc_rows, tc_cols), lambda bi, ci, ti: (bi, ti, ci)),
            pl.BlockSpec((None, tc_rows, tc_cols), lambda bi, ci, ti: (bi, ti, ncb + ci)),
            pl.BlockSpec((CONV_W, tc_cols), lambda bi, ci, ti: (0, ci)),
            pl.BlockSpec((1, tc_cols), lambda bi, ci, ti: (0, ci)),
            pl.BlockSpec((gpb, LRU_BLOCK, 2 * LRU_BLOCK), lambda bi, ci, ti: (ci, 0, 0)),
            pl.BlockSpec((gpb, 1, 2 * LRU_BLOCK), lambda bi, ci, ti: (ci, 0, 0)),
            pl.BlockSpec((1, tc_cols), lambda bi, ci, ti: (0, ci)),
        ],
        out_specs=[
            pl.BlockSpec((None, tc_rows, tc_cols), lambda bi, ci, ti: (bi, ti, ci)),
            pl.BlockSpec((None, 1, tc_cols), lambda bi, ci, ti: (bi, 0, ci)),
        ],
        out_shape=[jax.ShapeDtypeStruct((b, t, c), BF16),
                   jax.ShapeDtypeStruct((b, 1, c), F32)],
        scratch_shapes=[pltpu.VMEM((1, tc_cols), F32), pltpu.VMEM((SUBLANES, tc_cols), F32)],
        compiler_params=_params(("arbitrary", "arbitrary", "arbitrary")),
        name="lru_prompt",
    )(gx3, gx3, conv_w, conv_b, w_gates, b_gates, lam)
    return z, ht


def _lru_sample_body(gate_ref, xr_ref, cs_ref, h0_ref, cw_ref, cb_ref, wg_ref, bg_ref, lam_ref,
                     z_ref, ht_ref, *, n_steps, n_batch, tc_cols):
    cw = cw_ref[...]
    cb = cb_ref[...]
    slabs = [cs_ref[j] for j in range(CONV_W - 1)]
    slabs += [xr_ref[s * n_batch:(s + 1) * n_batch, :] for s in range(n_steps)]
    sp_all = _softplus(-lam_ref[...])
    h = [h0_ref[:, n * LRU_BLOCK:(n + 1) * LRU_BLOCK] for n in range(tc_cols // LRU_BLOCK)]
    for s in range(n_steps):
        rows = slice(s * n_batch, (s + 1) * n_batch)
        xc = cb + cw[0:1] * slabs[s]
        for j in range(1, CONV_W):
            xc = xc + cw[j:j + 1] * slabs[s + j]
        for n in range(tc_cols // LRU_BLOCK):
            cols = slice(n * LRU_BLOCK, (n + 1) * LRU_BLOCK)
            a, u = _lru_gates(xc[:, cols], wg_ref[n], bg_ref[n], sp_all[:, cols])
            h[n] = a * h[n] + u
            z_ref[rows, cols] = (_gelu_tanh(gate_ref[rows, cols]) * h[n]).astype(z_ref.dtype)
    for n in range(tc_cols // LRU_BLOCK):
        ht_ref[:, n * LRU_BLOCK:(n + 1) * LRU_BLOCK] = h[n]


def _lru_sample(gx, conv_state, h0, conv_w, conv_b, w_gates, b_gates, lam, *, n_steps, tc_cols=512):
    m, _ = gx.shape
    n_batch = m // n_steps
    c = D_RNN
    ncb = c // tc_cols
    gpb = tc_cols // LRU_BLOCK
    z, ht = pl.pallas_call(
        functools.partial(_lru_sample_body, n_steps=n_steps, n_batch=n_batch, tc_cols=tc_cols),
        grid=(ncb,),
        in_specs=[
            pl.BlockSpec((m, tc_cols), lambda ci: (0, ci)),
            pl.BlockSpec((m, tc_cols), lambda ci: (0, ncb + ci)),
            pl.BlockSpec((CONV_W - 1, n_batch, tc_cols), lambda ci: (0, 0, ci)),
            pl.BlockSpec((n_batch, tc_cols), lambda ci: (0, ci)),
            pl.BlockSpec((CONV_W, tc_cols), lambda ci: (0, ci)),
            pl.BlockSpec((1, tc_cols), lambda ci: (0, ci)),
            pl.BlockSpec((gpb, LRU_BLOCK, 2 * LRU_BLOCK), lambda ci: (ci, 0, 0)),
            pl.BlockSpec((gpb, 1, 2 * LRU_BLOCK), lambda ci: (ci, 0, 0)),
            pl.BlockSpec((1, tc_cols), lambda ci: (0, ci)),
        ],
        out_specs=[
            pl.BlockSpec((m, tc_cols), lambda ci: (0, ci)),
            pl.BlockSpec((n_batch, tc_cols), lambda ci: (0, ci)),
        ],
        out_shape=[jax.ShapeDtypeStruct((m, c), BF16),
                   jax.ShapeDtypeStruct((n_batch, c), F32)],
        compiler_params=_params(("arbitrary",)),
        name="lru_sample",
    )(gx, gx, conv_state, h0, conv_w, conv_b, w_gates, b_gates, lam)
    return z, ht


def _sink_softmax_pv(s, allowed, sink, v_h):
    s = jnp.where(allowed, s, -jnp.inf)
    m = jnp.maximum(jnp.max(s, axis=-1, keepdims=True), sink)
    p = jnp.exp(s - m)
    denom = jnp.sum(p, axis=-1, keepdims=True) + jnp.exp(sink - m)
    o = jnp.dot(p.astype(BF16), v_h, preferred_element_type=F32)
    return o / denom


def _sink_column(sink_ref, kv_head, rows_per_head, n_rows):
    grp = lax.broadcasted_iota(jnp.int32, (n_rows, 1), 0) // rows_per_head
    sink = jnp.zeros((n_rows, 1), F32)
    for g in range(GROUP):
        sink = jnp.where(grp == g, sink_ref[kv_head * GROUP + g], sink)
    return sink


def _attn_prompt_body(sink_ref, q_ref, kvc_ref, kvp_ref, o_ref):
    i = pl.program_id(1)
    q = q_ref[...]
    kv = jnp.concatenate([kvp_ref[...], kvc_ref[...]], axis=0).astype(BF16)
    n_rows = GROUP * WINDOW
    q_pos = lax.broadcasted_iota(jnp.int32, (n_rows, 2 * WINDOW), 0) % WINDOW
    k_pos = lax.broadcasted_iota(jnp.int32, (n_rows, 2 * WINDOW), 1) - WINDOW
    dq = q_pos - k_pos
    k_min = jnp.where(i == 0, 0, -WINDOW)
    allowed = jnp.logical_and(jnp.logical_and(dq >= 0, dq < WINDOW), k_pos >= k_min)
    for h in range(N_KV_HEADS):
        k_h = kv[:, h * HEAD_DIM:(h + 1) * HEAD_DIM]
        v_h = kv[:, KV_DIM + h * HEAD_DIM:KV_DIM + (h + 1) * HEAD_DIM]
        q_h = jnp.concatenate(
            [q[:, (h * GROUP + g) * HEAD_DIM:(h * GROUP + g + 1) * HEAD_DIM] for g in range(GROUP)], axis=0)
        s = lax.dot_general(q_h, k_h, (((1,), (1,)), ((), ())), preferred_element_type=F32)
        o_h = _sink_softmax_pv(s, allowed, _sink_column(sink_ref, h, WINDOW, n_rows), v_h)
        for g in range(GROUP):
            c0 = (h * GROUP + g) * HEAD_DIM
            o_ref[:, c0:c0 + HEAD_DIM] = o_h[g * WINDOW:(g + 1) * WINDOW].astype(o_ref.dtype)


def _attn_prompt(q3, kv3, sinks):
    b, t, _ = q3.shape
    nb = t // WINDOW
    return pl.pallas_call(
        _attn_prompt_body,
        grid=(b, nb),
        in_specs=[
            pl.BlockSpec(memory_space=pltpu.SMEM),
            pl.BlockSpec((None, WINDOW, N_HEADS * HEAD_DIM), lambda bi, i: (bi, i, 0)),
            pl.BlockSpec((None, WINDOW, 2 * KV_DIM), lambda bi, i: (bi, i, 0)),
            pl.BlockSpec((None, WINDOW, 2 * KV_DIM), lambda bi, i: (bi, jnp.maximum(i - 1, 0), 0)),
        ],
        out_specs=pl.BlockSpec((None, WINDOW, N_HEADS * HEAD_DIM), lambda bi, i: (bi, i, 0)),
        out_shape=jax.ShapeDtypeStruct((b, t, N_HEADS * HEAD_DIM), BF16),
        compiler_params=_params(("arbitrary", "arbitrary")),
        name="attn_prompt",
    )(sinks, q3, kv3, kv3)


SAMPLE_BATCH_TILE = 8


def _attn_sample_body(sink_ref, q_ref, kvn_ref, ck_ref, cv_ref, o_ref, *, n_steps):
    bb = SAMPLE_BATCH_TILE
    n_new = n_steps * bb
    n_cache = bb * WINDOW
    n_keys = n_cache + n_new
    n_keys_pad = -(-n_keys // LANES) * LANES
    q = q_ref[...].reshape(n_new, N_HEADS * HEAD_DIM)
    kvn = kvn_ref[...].reshape(n_new, 2 * KV_DIM)
    pad = jnp.zeros((n_keys_pad - n_keys, KV_DIM), F32)
    k_all = jnp.concatenate([ck_ref[...].reshape(n_cache, KV_DIM), kvn[:, :KV_DIM], pad], axis=0).astype(BF16)
    v_all = jnp.concatenate([cv_ref[...].reshape(n_cache, KV_DIM), kvn[:, KV_DIM:], pad], axis=0).astype(BF16)
    n_rows = GROUP * n_new
    r = lax.broadcasted_iota(jnp.int32, (n_rows, n_keys_pad), 0)
    c = lax.broadcasted_iota(jnp.int32, (n_rows, n_keys_pad), 1)
    b_r = r % bb
    t_r = (r // bb) % n_steps
    is_cache = c < n_cache
    c2 = jnp.maximum(c - n_cache, 0)
    b_c = jnp.where(is_cache, c // WINDOW, c2 % bb)
    k_step = jnp.where(is_cache, c % WINDOW - WINDOW, c2 // bb)
    dq = t_r - k_step
    allowed = jnp.logical_and(jnp.logical_and(b_c == b_r, c < n_keys),
                              jnp.logical_and(dq >= 0, dq < WINDOW))
    for h in range(N_KV_HEADS):
        k_h = k_all[:, h * HEAD_DIM:(h + 1) * HEAD_DIM]
        v_h = v_all[:, h * HEAD_DIM:(h + 1) * HEAD_DIM]
        q_h = jnp.concatenate(
            [q[:, (h * GROUP + g) * HEAD_DIM:(h * GROUP + g + 1) * HEAD_DIM] for g in range(GROUP)], axis=0)
        s = lax.dot_general(q_h, k_h, (((1,), (1,)), ((), ())), preferred_element_type=F32)
        o_h = _sink_softmax_pv(s, allowed, _sink_column(sink_ref, h, n_new, n_rows), v_h)
        for g in range(GROUP):
            c0 = (h * GROUP + g) * HEAD_DIM
            o_ref[:, :, c0:c0 + HEAD_DIM] = (
                o_h[g * n_new:(g + 1) * n_new].reshape(n_steps, bb, HEAD_DIM).astype(o_ref.dtype))


def _attn_sample(q3, kvn3, cache_k, cache_v, sinks):
    n_steps, b, _ = q3.shape
    bb = SAMPLE_BATCH_TILE
    return pl.pallas_call(
        functools.partial(_attn_sample_body, n_steps=n_steps),
        grid=(b // bb,),
        in_specs=[
            pl.BlockSpec(memory_space=pltpu.SMEM),
            pl.BlockSpec((n_steps, bb, N_HEADS * HEAD_DIM), lambda i: (0, i, 0)),
            pl.BlockSpec((n_steps, bb, 2 * KV_DIM), lambda i: (0, i, 0)),
            pl.BlockSpec((bb, WINDOW, KV_DIM), lambda i: (i, 0, 0)),
            pl.BlockSpec((bb, WINDOW, KV_DIM), lambda i: (i, 0, 0)),
        ],
        out_specs=pl.BlockSpec((n_steps, bb, N_HEADS * HEAD_DIM), lambda i: (0, i, 0)),
        out_shape=jax.ShapeDtypeStruct((n_steps, b, N_HEADS * HEAD_DIM), BF16),
        compiler_params=_params(("arbitrary",)),
        name="attn_sample",
    )(sinks, q3, kvn3, cache_k, cache_v)


def _mlp(h, gain, w_up, w_down):
    (xn,) = _rmsnorm(h, [gain], BF16)
    hid = _matmul(xn, w_up, out_dtype=BF16, act="relu2", name="mlp_up")
    return _matmul(hid, w_down, out_dtype=F32, resid=h, bk=2048, name="mlp_down")


def _trunk(x, lru_fn, attn_fn, w):
    (xn,) = _rmsnorm(x, [w["norm_mix"][0]], BF16)
    gx = _matmul(xn, w["rec_w_in"], out_dtype=F32, name="rec_in")
    z, h_last = lru_fn(gx)
    h = _matmul(z, w["rec_w_out"], out_dtype=F32, resid=x, name="rec_out")
    h = _mlp(h, w["norm_mlp"][0], w["mlp_w_up"][0], w["mlp_w_down"][0])
    xkv, xq = _rmsnorm(h, [w["kv_norm"], w["norm_mix"][1]], BF16)
    kv = _matmul(xkv, w["w_kv"], out_dtype=F32, name="kv_proj")
    q = _matmul(xq, w["attn_w_q"], out_dtype=BF16, act="qscale", name="q_proj")
    o = attn_fn(q, kv)
    h = _matmul(o, w["attn_w_o"], out_dtype=F32, resid=h, name="attn_out")
    h = _mlp(h, w["norm_mlp"][1], w["mlp_w_up"][1], w["mlp_w_down"][1])
    (out,) = _rmsnorm(h, [w["final_norm"]], F32)
    return out, gx, h_last, kv


def kernel(x_prompt, x_sample, state_conv, state_h, cache_k, cache_v, norm_mix, norm_mlp, rec_w_in,
           rec_conv_w, rec_conv_b, rec_gate_a_w, rec_gate_a_b, rec_gate_x_w, rec_gate_x_b, rec_lambda,
           rec_w_out, kv_norm, w_kv, attn_w_q, attn_sinks, attn_w_o, mlp_w_up, mlp_w_down, final_norm):
    b, t, d = x_prompt.shape
    sb, st, _ = x_sample.shape
    c = D_RNN

    w = {
        "norm_mix": norm_mix, "norm_mlp": norm_mlp, "kv_norm": kv_norm, "final_norm": final_norm,
        "rec_w_in": rec_w_in[0].astype(BF16), "rec_w_out": rec_w_out[0].astype(BF16),
        "w_kv": w_kv.astype(BF16), "attn_w_q": attn_w_q[0].astype(BF16), "attn_w_o": attn_w_o[0].astype(BF16),
        "mlp_w_up": mlp_w_up.astype(BF16), "mlp_w_down": mlp_w_down.astype(BF16),
    }
    conv_w = rec_conv_w[0]
    conv_b = rec_conv_b[0].reshape(1, c)
    lam = rec_lambda[0].reshape(1, c)
    w_gates = jnp.concatenate([rec_gate_a_w[0], rec_gate_x_w[0]], axis=-1).astype(BF16)
    b_gates = jnp.concatenate([rec_gate_a_b[0], rec_gate_x_b[0]], axis=-1).reshape(N_LRU_BLOCKS, 1, 2 * LRU_BLOCK)
    sinks = attn_sinks[0].astype(F32)

    def lru_prompt(gx):
        z, ht = _lru_prompt(gx.reshape(b, t, 2 * c), conv_w, conv_b, w_gates, b_gates, lam)
        return z.reshape(b * t, c), ht

    def attn_prompt(q, kv):
        o = _attn_prompt(q.reshape(b, t, N_HEADS * HEAD_DIM), kv.reshape(b, t, 2 * KV_DIM), sinks)
        return o.reshape(b * t, N_HEADS * HEAD_DIM)

    y_p, gx_p, h_p, kv_p = _trunk(x_prompt.reshape(b * t, d), lru_prompt, attn_prompt, w)
    y_prompt = y_p.reshape(b, t, d)
    conv_p = gx_p.reshape(b, t, 2 * c)[:, t - (CONV_W - 1):, c:].reshape(b, 1, CONV_W - 1, c)
    keep = min(WINDOW, t)
    kv_tail = kv_p.reshape(b, t, 2 * KV_DIM)[:, t - keep:]
    new_k_prompt = kv_tail[..., :KV_DIM].reshape(b, keep, N_KV_HEADS, HEAD_DIM)
    new_v_prompt = kv_tail[..., KV_DIM:].reshape(b, keep, N_KV_HEADS, HEAD_DIM)

    conv_state_tm = jnp.swapaxes(state_conv[:, 0], 0, 1)
    h0 = state_h[:, 0]
    ck = cache_k.reshape(sb, WINDOW, KV_DIM)
    cv = cache_v.reshape(sb, WINDOW, KV_DIM)

    def lru_sample(gx):
        return _lru_sample(gx, conv_state_tm, h0, conv_w, conv_b, w_gates, b_gates, lam, n_steps=st)

    def attn_sample(q, kv):
        o = _attn_sample(q.reshape(st, sb, N_HEADS * HEAD_DIM), kv.reshape(st, sb, 2 * KV_DIM), ck, cv, sinks)
        return o.reshape(st * sb, N_HEADS * HEAD_DIM)

    x_s = jnp.swapaxes(x_sample, 0, 1).reshape(st * sb, d)
    y_s, gx_s, h_s, kv_s = _trunk(x_s, lru_sample, attn_sample, w)
    y_sample = jnp.swapaxes(y_s.reshape(st, sb, d), 0, 1)
    conv_s = jnp.swapaxes(gx_s.reshape(st, sb, 2 * c)[st - (CONV_W - 1):, :, c:], 0, 1).reshape(sb, 1, CONV_W - 1, c)
    kv_s = jnp.swapaxes(kv_s.reshape(st, sb, 2 * KV_DIM), 0, 1)
    new_k_sample = kv_s[..., :KV_DIM].reshape(sb, st, N_KV_HEADS, HEAD_DIM)
    new_v_sample = kv_s[..., KV_DIM:].reshape(sb, st, N_KV_HEADS, HEAD_DIM)

    return (y_prompt, y_sample, conv_p, h_p.reshape(b, 1, c), new_k_prompt, new_v_prompt,
            conv_s, h_s.reshape(sb, 1, c), new_k_sample, new_v_sample)
```

```python
import functools
import math

import jax
import jax.numpy as jnp
from jax import lax
from jax.experimental import pallas as pl
from jax.experimental.pallas import tpu as pltpu

F32 = jnp.float32
BF16 = jnp.bfloat16

D_MODEL = 2048
D_RNN = 2048
N_LRU_BLOCKS = 16
LRU_BLOCK = D_RNN // N_LRU_BLOCKS
CONV_W = 4
LRU_C = 8.0
N_HEADS = 32
N_KV_HEADS = 4
HEAD_DIM = 64
GROUP = N_HEADS // N_KV_HEADS
Q_DIM = N_HEADS * HEAD_DIM
KV_DIM = N_KV_HEADS * HEAD_DIM
WINDOW = 128
EPS = 1e-6
LOG2E = math.log2(math.e)

SUBLANES = 8
LANES = 128
VMEM_LIMIT_BYTES = 52 * 1024 * 1024


def _params(semantics):
    return pltpu.CompilerParams(dimension_semantics=semantics,
                                vmem_limit_bytes=VMEM_LIMIT_BYTES)


def _rmsnorm_body(x_ref, g_ref, o_ref):
    x = x_ref[...]
    y = x * lax.rsqrt(jnp.mean(x * x, axis=-1, keepdims=True) + EPS)
    o_ref[...] = (y * g_ref[...]).astype(o_ref.dtype)


def _rmsnorm(x, gain, out_dtype, tm=512):
    m, d = x.shape
    tm = min(tm, m)
    row_spec = pl.BlockSpec((tm, d), lambda i: (i, 0))
    return pl.pallas_call(
        _rmsnorm_body,
        grid=(m // tm,),
        in_specs=[row_spec, pl.BlockSpec((1, d), lambda i: (0, 0))],
        out_specs=row_spec,
        out_shape=jax.ShapeDtypeStruct((m, d), out_dtype),
        compiler_params=_params(("arbitrary",)),
        name="rmsnorm",
    )(x, gain.reshape(1, d).astype(F32))


def _matmul_body(*refs, nk, act, has_scale, has_resid, n_gains, has_ssq, norm_dim):
    refs = list(refs)
    x_ref, w_ref = refs[:2]
    pos = 2
    s_ref = r_ref = None
    if has_scale:
        s_ref = refs[pos]
        pos += 1
    if has_resid:
        r_ref = refs[pos]
        pos += 1
    g_refs = refs[pos:pos + n_gains]
    pos += n_gains
    o_ref = refs[pos]
    pos += 1
    e_refs = refs[pos:pos + n_gains]
    pos += n_gains
    q_ref = None
    if has_ssq:
        q_ref = refs[pos]
        pos += 1
    acc_ref = refs[pos] if nk > 1 else None

    part = jnp.dot(x_ref[...], w_ref[...], preferred_element_type=F32)

    def finish(acc):
        if s_ref is not None:
            acc = acc * lax.rsqrt(jnp.sum(s_ref[...], axis=-1, keepdims=True) * (1.0 / norm_dim) + EPS)
        if act == "relu2":
            r = jnp.maximum(acc, 0.0)
            acc = r * r
        elif act == "qscale":
            acc = acc * (HEAD_DIM ** -0.5 * LOG2E)
        if r_ref is not None:
            acc = r_ref[...] + acc
        o_ref[...] = acc.astype(o_ref.dtype)
        for g_ref, e_ref in zip(g_refs, e_refs):
            e_ref[...] = (acc * g_ref[...]).astype(e_ref.dtype)
        if q_ref is not None:
            sq = acc * acc
            tot = sq[:, 0:LANES]
            for c in range(1, sq.shape[1] // LANES):
                tot = tot + sq[:, c * LANES:(c + 1) * LANES]
            q_ref[...] = tot

    if nk == 1:
        finish(part)
    else:
        k = pl.program_id(2)

        @pl.when(k == 0)
        def _():
            acc_ref[...] = part

        @pl.when(jnp.logical_and(k > 0, k < nk - 1))
        def _():
            acc_ref[...] += part

        @pl.when(k == nk - 1)
        def _():
            finish(acc_ref[...] + part)


def _matmul(x, w, *, out_dtype, act=None, ssq_in=None, resid=None, gains=(), ssq_out=False,
            x_batches=None, out_batches=None, bm=1024, bn=1024, bk=None, name="matmul"):
    kdim, n = w.shape
    m = x.shape[0] * (x_batches or 1)
    bm = min(bm, m // (x_batches or out_batches or 1))
    bn = min(bn, n)
    bk = kdim if bk is None else min(bk, kdim)
    nk, nb = kdim // bk, n // bn
    mb = m // bm
    tb = mb // (x_batches or out_batches or 1)

    if x_batches:
        assert nk == 1
        x_spec = pl.BlockSpec((bm, bk), lambda i, j, k: (i % tb, i // tb))
    else:
        x_spec = pl.BlockSpec((bm, bk), lambda i, j, k: (i, k))
    in_specs = [x_spec, pl.BlockSpec((bk, bn), lambda i, j, k: (k, j))]
    args = [x, w]
    if ssq_in is not None:
        in_specs.append(pl.BlockSpec((bm, ssq_in.shape[1]), lambda i, j, k: (i, 0)))
        args.append(ssq_in)
    tile_spec = pl.BlockSpec((bm, bn), lambda i, j, k: (i, j))
    if resid is not None:
        in_specs.append(tile_spec)
        args.append(resid)
    for g in gains:
        in_specs.append(pl.BlockSpec((1, bn), lambda i, j, k: (0, j)))
        args.append(g.reshape(1, n).astype(F32))

    if out_batches:
        out_specs = [pl.BlockSpec((bm, bn), lambda i, j, k: (i % tb, (i // tb) * nb + j))]
        out_shape = [jax.ShapeDtypeStruct((m // out_batches, out_batches * n), out_dtype)]
    else:
        out_specs = [tile_spec]
        out_shape = [jax.ShapeDtypeStruct((m, n), out_dtype)]
    for _ in gains:
        out_specs.append(tile_spec)
        out_shape.append(jax.ShapeDtypeStruct((m, n), BF16))
    if ssq_out:
        out_specs.append(pl.BlockSpec((bm, LANES), lambda i, j, k: (i, j)))
        out_shape.append(jax.ShapeDtypeStruct((m, nb * LANES), F32))

    outs = pl.pallas_call(
        functools.partial(_matmul_body, nk=nk, act=act, has_scale=ssq_in is not None,
                          has_resid=resid is not None, n_gains=len(gains), has_ssq=ssq_out, norm_dim=kdim),
        grid=(mb, nb, nk),
        in_specs=in_specs,
        out_specs=out_specs,
        out_shape=out_shape,
        scratch_shapes=[pltpu.VMEM((bm, bn), F32)] if nk > 1 else [],
        compiler_params=_params(("arbitrary", "arbitrary", "arbitrary")),
        name=name,
    )(*args)
    return outs[0] if len(outs) == 1 else outs


def _gelu_tanh(x):
    return x * (0.5 + 0.5 * jnp.tanh(0.7978845608028654 * (x + 0.044715 * (x * x * x))))


def _softplus(z):
    return jnp.maximum(z, 0.0) + jnp.log1p(jnp.exp(-jnp.abs(z)))


def _lru_body(gate_ref, xr_ref, cs_ref, h0_ref, cw_ref, cb_ref, wg_ref, bg_ref, lam_ref,
              z_ref, ht_ref, h_sc, tail_sc, *, n_steps, n_batch, tc_cols):
    t = pl.program_id(1)
    n_tail = (CONV_W - 1) * n_batch
    n_rows = n_steps * n_batch

    @pl.when(t == 0)
    def _():
        h_sc[...] = h0_ref[...]
        tail_sc[...] = cs_ref[...]

    cw = cw_ref[...]
    decay = (-0.5 * LRU_C * LOG2E) * _softplus(-lam_ref[...])
    for n in range(tc_cols // LRU_BLOCK):
        cols = slice(n * LRU_BLOCK, (n + 1) * LRU_BLOCK)
        x = xr_ref[:, cols]
        x_ext = jnp.concatenate([tail_sc[:, cols], x], axis=0)
        tail_sc[:, cols] = x_ext[n_rows:n_rows + n_tail]
        xc = cb_ref[:, cols] + cw[0:1, cols] * x_ext[0:n_rows]
        for j in range(1, CONV_W):
            xc = xc + cw[j:j + 1, cols] * x_ext[j * n_batch:j * n_batch + n_rows]
        half_pre = 0.5 * (jnp.dot(xc.astype(BF16), wg_ref[n], preferred_element_type=F32) + bg_ref[n])
        tr = jnp.tanh(half_pre[:, :LRU_BLOCK])
        ti = jnp.tanh(half_pre[:, LRU_BLOCK:])
        d = decay[:, cols]
        a = jnp.exp2(d * tr + d)
        gain2 = 1.0 - a * a
        u = (gain2 * lax.rsqrt(jnp.maximum(gain2, 1e-30))) * ((0.5 + 0.5 * ti) * xc)
        h = h_sc[:, cols]
        hs = []
        for s in range(n_steps):
            rows = slice(s * n_batch, (s + 1) * n_batch)
            h = a[rows] * h + u[rows]
            hs.append(h)
        h_sc[:, cols] = h
        hs = jnp.concatenate(hs, axis=0)
        z_ref[:, cols] = (_gelu_tanh(gate_ref[:, cols]) * hs).astype(z_ref.dtype)

    @pl.when(t == pl.num_programs(1) - 1)
    def _():
        ht_ref[...] = h_sc[...]


def _lru(gx, conv_state, h0, conv_w, conv_b, w_gates, b_gates, lam, *, n_batch, n_steps, tc_cols=512):
    m = gx.shape[0]
    c = D_RNN
    n_rows = n_steps * n_batch
    n_tail = (CONV_W - 1) * n_batch
    ncb = c // tc_cols
    gpb = tc_cols // LRU_BLOCK
    z, ht = pl.pallas_call(
        functools.partial(_lru_body, n_steps=n_steps, n_batch=n_batch, tc_cols=tc_cols),
        grid=(ncb, m // n_rows),
        in_specs=[
            pl.BlockSpec((n_rows, tc_cols), lambda ci, ti: (ti, ci)),
            pl.BlockSpec((n_rows, tc_cols), lambda ci, ti: (ti, ncb + ci)),
            pl.BlockSpec((n_tail, tc_cols), lambda ci, ti: (0, ci)),
            pl.BlockSpec((n_batch, tc_cols), lambda ci, ti: (0, ci)),
            pl.BlockSpec((CONV_W, tc_cols), lambda ci, ti: (0, ci)),
            pl.BlockSpec((1, tc_cols), lambda ci, ti: (0, ci)),
            pl.BlockSpec((gpb, LRU_BLOCK, 2 * LRU_BLOCK), lambda ci, ti: (ci, 0, 0)),
            pl.BlockSpec((gpb, 1, 2 * LRU_BLOCK), lambda ci, ti: (ci, 0, 0)),
            pl.BlockSpec((1, tc_cols), lambda ci, ti: (0, ci)),
        ],
        out_specs=[
            pl.BlockSpec((n_rows, tc_cols), lambda ci, ti: (ti, ci)),
            pl.BlockSpec((n_batch, tc_cols), lambda ci, ti: (0, ci)),
        ],
        out_shape=[jax.ShapeDtypeStruct((m, c), BF16),
                   jax.ShapeDtypeStruct((n_batch, c), F32)],
        scratch_shapes=[pltpu.VMEM((n_batch, tc_cols), F32), pltpu.VMEM((n_tail, tc_cols), F32)],
        compiler_params=_params(("arbitrary", "arbitrary")),
        name="lru",
    )(gx, gx, conv_state, h0, conv_w, conv_b, w_gates, b_gates, lam)
    return z, ht


def _attn_prompt_body(sink_ref, q_ref, kvc_ref, kvp_ref, o_ref):
    i = pl.program_id(1)
    kv = jnp.concatenate([kvp_ref[...], kvc_ref[...]], axis=0).astype(BF16)
    n_keys = 2 * WINDOW
    low_k = lax.broadcasted_iota(jnp.int32, (n_keys, LANES), 1) < HEAD_DIM
    low_q = lax.broadcasted_iota(jnp.int32, (WINDOW, LANES), 1) < HEAD_DIM
    zero = jnp.zeros((n_keys, LANES), BF16)
    one = jnp.ones((n_keys, LANES), BF16)
    q_pos = lax.broadcasted_iota(jnp.int32, (WINDOW, n_keys), 0)
    k_pos = lax.broadcasted_iota(jnp.int32, (WINDOW, n_keys), 1) - WINDOW
    dq = q_pos - k_pos
    k_min = jnp.where(i == 0, 0, -WINDOW)
    allowed = jnp.logical_and(jnp.logical_and(dq >= 0, dq < WINDOW), k_pos >= k_min)
    bias = jnp.where(allowed, 0.0, -jnp.inf).astype(F32)
    dims = (((1,), (1,)), ((), ()))

    for h in range(N_KV_HEADS):
        c0 = (h // 2) * LANES
        k_t = kv[:, c0:c0 + LANES]
        v_t = kv[:, KV_DIM + c0:KV_DIM + c0 + LANES]
        k_sw = pltpu.roll(k_t, HEAD_DIM, 1)
        v_sw = pltpu.roll(v_t, HEAD_DIM, 1)
        k_lo, k_hi = (k_t, k_sw) if h % 2 == 0 else (k_sw, k_t)
        v_lo, v_hi = (v_t, v_sw) if h % 2 == 0 else (v_sw, v_t)
        k_a = jnp.where(low_k, k_lo, zero)
        k_b = jnp.where(low_k, zero, k_hi)
        v_a = jnp.where(low_k, v_lo, one)
        v_b = jnp.where(low_k, one, v_hi)
        n_pairs = GROUP // 2
        qc0 = h * GROUP * HEAD_DIM
        q_h = jnp.concatenate([q_ref[:, qc0 + p * LANES:qc0 + (p + 1) * LANES] for p in range(n_pairs)], axis=0)
        acc, esink = [], []
        for side, (k_x, v_x) in enumerate(((k_a, v_a), (k_b, v_b))):
            s = lax.dot_general(q_h, k_x, dims, preferred_element_type=F32)
            probs, es = [], []
            for p in range(n_pairs):
                s_p = s[p * WINDOW:(p + 1) * WINDOW] + bias
                sink = sink_ref[h * GROUP + 2 * p + side] * LOG2E
                m = jnp.maximum(jnp.max(s_p, axis=-1, keepdims=True), sink)
                probs.append(jnp.exp2(s_p - m).astype(BF16))
                es.append(jnp.exp2(sink - m))
            acc.append(jnp.dot(jnp.concatenate(probs, axis=0), v_x, preferred_element_type=F32))
            esink.append(es)
        for p in range(n_pairs):
            rows = slice(p * WINDOW, (p + 1) * WINDOW)
            acc_a, acc_b = acc[0][rows], acc[1][rows]
            num = jnp.where(low_q, acc_a, acc_b)
            den = pltpu.roll(jnp.where(low_q, acc_b, acc_a), HEAD_DIM, 1) + jnp.where(low_q, esink[0][p], esink[1][p])
            o_ref[:, qc0 + p * LANES:qc0 + (p + 1) * LANES] = (num / den).astype(o_ref.dtype)


def _attn_prompt(q3, kv3, sinks):
    b, t, _ = q3.shape
    nb = t // WINDOW
    return pl.pallas_call(
        _attn_prompt_body,
        grid=(b, nb),
        in_specs=[
            pl.BlockSpec(memory_space=pltpu.SMEM),
            pl.BlockSpec((None, WINDOW, Q_DIM), lambda bi, i: (bi, i, 0)),
            pl.BlockSpec((None, WINDOW, 2 * KV_DIM), lambda bi, i: (bi, i, 0)),
            pl.BlockSpec((None, WINDOW, 2 * KV_DIM), lambda bi, i: (bi, jnp.maximum(i - 1, 0), 0)),
        ],
        out_specs=pl.BlockSpec((None, WINDOW, Q_DIM), lambda bi, i: (bi, i, 0)),
        out_shape=jax.ShapeDtypeStruct((b, t, Q_DIM), BF16),
        compiler_params=_params(("arbitrary", "arbitrary")),
        name="attn_prompt",
    )(sinks, q3, kv3, kv3)


SAMPLE_BATCH_TILE = 8


def _attn_sample_body(sink_ref, q_ref, kvn_ref, ck_ref, cv_ref, o_ref, *, n_steps):
    bb = SAMPLE_BATCH_TILE
    n_new = n_steps * bb
    n_cache = bb * WINDOW
    n_keys = n_cache + n_new
    n_keys_pad = -(-n_keys // LANES) * LANES
    q = q_ref[...].reshape(n_new, Q_DIM)
    kvn = kvn_ref[...].reshape(n_new, 2 * KV_DIM)
    pad = jnp.zeros((n_keys_pad - n_keys, KV_DIM), F32)
    k_all = jnp.concatenate([ck_ref[...].reshape(n_cache, KV_DIM), kvn[:, :KV_DIM], pad], axis=0).astype(BF16)
    v_all = jnp.concatenate([cv_ref[...].reshape(n_cache, KV_DIM), kvn[:, KV_DIM:], pad], axis=0).astype(BF16)
    n_rows = GROUP * n_new
    r = lax.broadcasted_iota(jnp.int32, (n_rows, n_keys_pad), 0)
    c = lax.broadcasted_iota(jnp.int32, (n_rows, n_keys_pad), 1)
    b_r = r % bb
    t_r = (r // bb) % n_steps
    is_cache = c < n_cache
    c2 = jnp.maximum(c - n_cache, 0)
    b_c = jnp.where(is_cache, c // WINDOW, c2 % bb)
    k_step = jnp.where(is_cache, c % WINDOW - WINDOW, c2 // bb)
    dq = t_r - k_step
    allowed = jnp.logical_and(jnp.logical_and(b_c == b_r, c < n_keys),
                              jnp.logical_and(dq >= 0, dq < WINDOW))
    grp = lax.broadcasted_iota(jnp.int32, (n_rows, 1), 0) // n_new
    dims = (((1,), (1,)), ((), ()))
    for h in range(N_KV_HEADS):
        k_h = k_all[:, h * HEAD_DIM:(h + 1) * HEAD_DIM]
        v_h = v_all[:, h * HEAD_DIM:(h + 1) * HEAD_DIM]
        q_h = jnp.concatenate(
            [q[:, (h * GROUP + g) * HEAD_DIM:(h * GROUP + g + 1) * HEAD_DIM] for g in range(GROUP)], axis=0)
        sink = jnp.zeros((n_rows, 1), F32)
        for g in range(GROUP):
            sink = jnp.where(grp == g, sink_ref[h * GROUP + g] * LOG2E, sink)
        s = lax.dot_general(q_h, k_h, dims, preferred_element_type=F32)
        s = jnp.where(allowed, s, -jnp.inf)
        m = jnp.maximum(jnp.max(s, axis=-1, keepdims=True), sink)
        p = jnp.exp2(s - m)
        denom = jnp.sum(p, axis=-1, keepdims=True) + jnp.exp2(sink - m)
        o_h = jnp.dot(p.astype(BF16), v_h, preferred_element_type=F32) / denom
        for g in range(GROUP):
            c0 = (h * GROUP + g) * HEAD_DIM
            o_ref[:, :, c0:c0 + HEAD_DIM] = (
                o_h[g * n_new:(g + 1) * n_new].reshape(n_steps, bb, HEAD_DIM).astype(o_ref.dtype))


def _attn_sample(q3, kvn3, cache_k, cache_v, sinks):
    n_steps, b, _ = q3.shape
    bb = SAMPLE_BATCH_TILE
    return pl.pallas_call(
        functools.partial(_attn_sample_body, n_steps=n_steps),
        grid=(b // bb,),
        in_specs=[
            pl.BlockSpec(memory_space=pltpu.SMEM),
            pl.BlockSpec((n_steps, bb, Q_DIM), lambda i: (0, i, 0)),
            pl.BlockSpec((n_steps, bb, 2 * KV_DIM), lambda i: (0, i, 0)),
            pl.BlockSpec((bb, WINDOW, KV_DIM), lambda i: (i, 0, 0)),
            pl.BlockSpec((bb, WINDOW, KV_DIM), lambda i: (i, 0, 0)),
        ],
        out_specs=pl.BlockSpec((n_steps, bb, Q_DIM), lambda i: (0, i, 0)),
        out_shape=jax.ShapeDtypeStruct((n_steps, b, Q_DIM), BF16),
        compiler_params=_params(("arbitrary",)),
        name="attn_sample",
    )(sinks, q3, kvn3, cache_k, cache_v)


def _trunk(x, lru_fn, attn_fn, w, *, lru_batches=None):
    xn = _rmsnorm(x, w["norm_mix"][0], BF16)
    gx = _matmul(xn, w["rec_w_in"], out_dtype=F32, out_batches=lru_batches, name="rec_in")
    z, h_last = lru_fn(gx)
    h, hb, ssq = _matmul(z, w["rec_w_out"], out_dtype=F32, resid=x, gains=[w["norm_mlp"][0]], ssq_out=True,
                         x_batches=lru_batches, name="rec_out")
    hid = _matmul(hb, w["mlp_w_up"][0], out_dtype=BF16, act="relu2", ssq_in=ssq, name="mlp_up")
    h, hkv, hq, ssq = _matmul(hid, w["mlp_w_down"][0], out_dtype=F32, resid=h,
                              gains=[w["kv_norm"], w["norm_mix"][1]], ssq_out=True, bk=2048, name="mlp_down")
    kv = _matmul(hkv, w["w_kv"], out_dtype=F32, ssq_in=ssq, name="kv_proj")
    q = _matmul(hq, w["attn_w_q"], out_dtype=BF16, act="qscale", ssq_in=ssq, name="q_proj")
    o = attn_fn(q, kv)
    h, hb, ssq = _matmul(o, w["attn_w_o"], out_dtype=F32, resid=h, gains=[w["norm_mlp"][1]], ssq_out=True,
                         name="attn_out")
    hid = _matmul(hb, w["mlp_w_up"][1], out_dtype=BF16, act="relu2", ssq_in=ssq, name="mlp_up")
    h = _matmul(hid, w["mlp_w_down"][1], out_dtype=F32, resid=h, bk=2048, name="mlp_down")
    out = _rmsnorm(h, w["final_norm"], F32)
    return out, gx, h_last, kv


def kernel(x_prompt, x_sample, state_conv, state_h, cache_k, cache_v, norm_mix, norm_mlp, rec_w_in,
           rec_conv_w, rec_conv_b, rec_gate_a_w, rec_gate_a_b, rec_gate_x_w, rec_gate_x_b, rec_lambda,
           rec_w_out, kv_norm, w_kv, attn_w_q, attn_sinks, attn_w_o, mlp_w_up, mlp_w_down, final_norm):
    b, t, d = x_prompt.shape
    sb, st, _ = x_sample.shape
    c = D_RNN
    n_tail = CONV_W - 1

    w = {
        "norm_mix": norm_mix, "norm_mlp": norm_mlp, "kv_norm": kv_norm, "final_norm": final_norm,
        "rec_w_in": rec_w_in[0].astype(BF16), "rec_w_out": rec_w_out[0].astype(BF16),
        "w_kv": w_kv.astype(BF16), "attn_w_q": attn_w_q[0].astype(BF16), "attn_w_o": attn_w_o[0].astype(BF16),
        "mlp_w_up": [mlp_w_up[l].astype(BF16) for l in range(mlp_w_up.shape[0])],
        "mlp_w_down": [mlp_w_down[l].astype(BF16) for l in range(mlp_w_down.shape[0])],
    }
    conv_w = rec_conv_w[0]
    conv_b = rec_conv_b[0].reshape(1, c)
    lam = rec_lambda[0].reshape(1, c)
    w_gates = jnp.concatenate([rec_gate_a_w[0], rec_gate_x_w[0]], axis=-1).astype(BF16)
    b_gates = jnp.concatenate([rec_gate_a_b[0], rec_gate_x_b[0]], axis=-1).reshape(N_LRU_BLOCKS, 1, 2 * LRU_BLOCK)
    sinks = attn_sinks[0].astype(F32)

    def lru_prompt(gx_tb):
        z, ht = _lru(gx_tb.reshape(t * b, 2 * c), jnp.zeros((n_tail * b, c), F32), jnp.zeros((b, c), F32),
                     conv_w, conv_b, w_gates, b_gates, lam, n_batch=b, n_steps=64)
        return z.reshape(t, b * c), ht

    def attn_prompt(q, kv):
        o = _attn_prompt(q.reshape(b, t, Q_DIM), kv.reshape(b, t, 2 * KV_DIM), sinks)
        return o.reshape(b * t, Q_DIM)

    y_p, gx_p, h_p, kv_p = _trunk(x_prompt.reshape(b * t, d), lru_prompt, attn_prompt, w, lru_batches=b)
    y_prompt = y_p.reshape(b, t, d)
    conv_p = jnp.swapaxes(gx_p.reshape(t, b, 2 * c)[t - n_tail:, :, c:], 0, 1).reshape(b, 1, n_tail, c)
    keep = min(WINDOW, t)
    kv_tail = kv_p.reshape(b, t, 2 * KV_DIM)[:, t - keep:]
    new_k_prompt = kv_tail[..., :KV_DIM].reshape(b, keep, N_KV_HEADS, HEAD_DIM)
    new_v_prompt = kv_tail[..., KV_DIM:].reshape(b, keep, N_KV_HEADS, HEAD_DIM)

    conv_state_tb = jnp.swapaxes(state_conv[:, 0], 0, 1).reshape(n_tail * sb, c)
    h0 = state_h[:, 0]
    ck = cache_k.reshape(sb, WINDOW, KV_DIM)
    cv = cache_v.reshape(sb, WINDOW, KV_DIM)

    def lru_sample(gx):
        return _lru(gx, conv_state_tb, h0, conv_w, conv_b, w_gates, b_gates, lam, n_batch=sb, n_steps=st)

    def attn_sample(q, kv):
        o = _attn_sample(q.reshape(st, sb, Q_DIM), kv.reshape(st, sb, 2 * KV_DIM), ck, cv, sinks)
        return o.reshape(st * sb, Q_DIM)

    x_s = jnp.swapaxes(x_sample, 0, 1).reshape(st * sb, d)
    y_s, gx_s, h_s, kv_s = _trunk(x_s, lru_sample, attn_sample, w)
    y_sample = jnp.swapaxes(y_s.reshape(st, sb, d), 0, 1)
    conv_s = jnp.swapaxes(gx_s.reshape(st, sb, 2 * c)[st - n_tail:, :, c:], 0, 1).reshape(sb, 1, n_tail, c)
    kv_s = jnp.swapaxes(kv_s.reshape(st, sb, 2 * KV_DIM), 0, 1)
    new_k_sample = kv_s[..., :KV_DIM].reshape(sb, st, N_KV_HEADS, HEAD_DIM)
    new_v_sample = kv_s[..., KV_DIM:].reshape(sb, st, N_KV_HEADS, HEAD_DIM)

    return (y_prompt, y_sample, conv_p, h_p.reshape(b, 1, c), new_k_prompt, new_v_prompt,
            conv_s, h_s.reshape(sb, 1, c), new_k_sample, new_v_sample)
```

```python
import functools
import math

import jax
import jax.numpy as jnp
from jax import lax
from jax.experimental import pallas as pl
from jax.experimental.pallas import tpu as pltpu

F32 = jnp.float32
BF16 = jnp.bfloat16

D_MODEL = 2048
D_RNN = 2048
N_LRU_BLOCKS = 16
LRU_BLOCK = D_RNN // N_LRU_BLOCKS
CONV_W = 4
LRU_C = 8.0
N_HEADS = 32
N_KV_HEADS = 4
HEAD_DIM = 64
GROUP = N_HEADS // N_KV_HEADS
Q_DIM = N_HEADS * HEAD_DIM
KV_DIM = N_KV_HEADS * HEAD_DIM
WINDOW = 128
EPS = 1e-6
LOG2E = math.log2(math.e)

SUBLANES = 8
LANES = 128
VMEM_LIMIT_BYTES = 52 * 1024 * 1024


def _params(semantics):
    return pltpu.CompilerParams(dimension_semantics=semantics,
                                vmem_limit_bytes=VMEM_LIMIT_BYTES)


def _rmsnorm_body(x_ref, g_ref, o_ref):
    x = x_ref[...]
    y = x * lax.rsqrt(jnp.mean(x * x, axis=-1, keepdims=True) + EPS)
    o_ref[...] = (y * g_ref[...]).astype(o_ref.dtype)


def _rmsnorm(x, gain, out_dtype, tm=512):
    m, d = x.shape
    tm = min(tm, m)
    row_spec = pl.BlockSpec((tm, d), lambda i: (i, 0))
    return pl.pallas_call(
        _rmsnorm_body,
        grid=(m // tm,),
        in_specs=[row_spec, pl.BlockSpec((1, d), lambda i: (0, 0))],
        out_specs=row_spec,
        out_shape=jax.ShapeDtypeStruct((m, d), out_dtype),
        compiler_params=_params(("arbitrary",)),
        name="rmsnorm",
    )(x, gain.reshape(1, d).astype(F32))


PERM_STEPS = 32


def _perm_matrix(n_batch, to_time_major):
    n = n_batch * PERM_STEPS
    r = lax.broadcasted_iota(jnp.int32, (n, n), 0)
    c = lax.broadcasted_iota(jnp.int32, (n, n), 1)
    if to_time_major:
        src = (r % n_batch) * PERM_STEPS + r // n_batch
    else:
        src = (r % PERM_STEPS) * n_batch + r // PERM_STEPS
    return jnp.where(c == src, 1.0, 0.0).astype(BF16)


def _rmsnorm_tb_body(x_ref, g_ref, o_ref, *, n_batch, n_steps):
    perm = _perm_matrix(n_batch, True)
    rows = n_batch * PERM_STEPS
    g = g_ref[...]
    for part in range(n_steps // PERM_STEPS):
        x = x_ref[:, part * PERM_STEPS:(part + 1) * PERM_STEPS, :].reshape(rows, x_ref.shape[-1])
        y = x * lax.rsqrt(jnp.mean(x * x, axis=-1, keepdims=True) + EPS)
        y = (y * g).astype(BF16)
        o_ref[part * rows:(part + 1) * rows, :] = jnp.dot(perm, y, preferred_element_type=F32).astype(o_ref.dtype)


def _rmsnorm_to_time_major(x3, gain, n_steps=64):
    b, t, d = x3.shape
    return pl.pallas_call(
        functools.partial(_rmsnorm_tb_body, n_batch=b, n_steps=n_steps),
        grid=(t // n_steps,),
        in_specs=[pl.BlockSpec((b, n_steps, d), lambda i: (0, i, 0)), pl.BlockSpec((1, d), lambda i: (0, 0))],
        out_specs=pl.BlockSpec((n_steps * b, d), lambda i: (i, 0)),
        out_shape=jax.ShapeDtypeStruct((t * b, d), BF16),
        compiler_params=_params(("arbitrary",)),
        name="rmsnorm_tb",
    )(x3, gain.reshape(1, d).astype(F32))


def _matmul_body(*refs, nk, act, has_scale, has_resid, n_gains, has_ssq, norm_dim):
    refs = list(refs)
    x_ref, w_ref = refs[:2]
    pos = 2
    s_ref = r_ref = None
    if has_scale:
        s_ref = refs[pos]
        pos += 1
    if has_resid:
        r_ref = refs[pos]
        pos += 1
    g_refs = refs[pos:pos + n_gains]
    pos += n_gains
    o_ref = refs[pos]
    pos += 1
    e_refs = refs[pos:pos + n_gains]
    pos += n_gains
    q_ref = None
    if has_ssq:
        q_ref = refs[pos]
        pos += 1
    acc_ref = refs[pos] if nk > 1 else None

    def partial_product():
        return jnp.dot(x_ref[...], w_ref[...], preferred_element_type=F32)

    def finish(acc):
        if s_ref is not None:
            acc = acc * lax.rsqrt(jnp.sum(s_ref[...], axis=-1, keepdims=True) * (1.0 / norm_dim) + EPS)
        if act == "relu2":
            r = jnp.maximum(acc, 0.0)
            acc = r * r
        elif act == "qscale":
            acc = acc * (HEAD_DIM ** -0.5 * LOG2E)
        if r_ref is not None:
            acc = r_ref[...] + acc
        o_ref[...] = acc.astype(o_ref.dtype)
        for g_ref, e_ref in zip(g_refs, e_refs):
            e_ref[...] = (acc * g_ref[...]).astype(e_ref.dtype)
        if q_ref is not None:
            sq = acc * acc
            tot = sq[:, 0:LANES]
            for c in range(1, sq.shape[1] // LANES):
                tot = tot + sq[:, c * LANES:(c + 1) * LANES]
            q_ref[...] = tot

    if nk == 1:
        finish(partial_product())
    else:
        k = pl.program_id(2)

        @pl.when(k == 0)
        def _():
            acc_ref[...] = partial_product()

        @pl.when(jnp.logical_and(k > 0, k < nk - 1))
        def _():
            acc_ref[...] += partial_product()

        @pl.when(k == nk - 1)
        def _():
            finish(acc_ref[...] + partial_product())


def _matmul(x, w, *, out_dtype, layer=0, act=None, ssq_in=None, resid=None, gains=(), ssq_out=False,
            bm=1024, bn=1024, bk=None, name="matmul"):
    m = x.shape[0]
    _, kdim, n = w.shape
    bm, bn = min(bm, m), min(bn, n)
    bk = kdim if bk is None else min(bk, kdim)
    nk, nb = kdim // bk, n // bn
    mb = m // bm

    in_specs = [pl.BlockSpec((bm, bk), lambda i, j, k: (i, k)),
                pl.BlockSpec((None, bk, bn), lambda i, j, k: (layer, k, j))]
    args = [x, w]
    if ssq_in is not None:
        in_specs.append(pl.BlockSpec((bm, ssq_in.shape[1]), lambda i, j, k: (i, 0)))
        args.append(ssq_in)
    tile_spec = pl.BlockSpec((bm, bn), lambda i, j, k: (i, j))
    if resid is not None:
        in_specs.append(tile_spec)
        args.append(resid)
    for g in gains:
        in_specs.append(pl.BlockSpec((1, bn), lambda i, j, k: (0, j)))
        args.append(g.reshape(1, n).astype(F32))

    out_specs = [tile_spec]
    out_shape = [jax.ShapeDtypeStruct((m, n), out_dtype)]
    for _ in gains:
        out_specs.append(tile_spec)
        out_shape.append(jax.ShapeDtypeStruct((m, n), BF16))
    if ssq_out:
        out_specs.append(pl.BlockSpec((bm, LANES), lambda i, j, k: (i, j)))
        out_shape.append(jax.ShapeDtypeStruct((m, nb * LANES), F32))

    outs = pl.pallas_call(
        functools.partial(_matmul_body, nk=nk, act=act, has_scale=ssq_in is not None,
                          has_resid=resid is not None, n_gains=len(gains), has_ssq=ssq_out, norm_dim=kdim),
        grid=(mb, nb, nk),
        in_specs=in_specs,
        out_specs=out_specs,
        out_shape=out_shape,
        scratch_shapes=[pltpu.VMEM((bm, bn), F32)] if nk > 1 else [],
        compiler_params=_params(("arbitrary", "arbitrary", "arbitrary")),
        name=name,
    )(*args)
    return outs[0] if len(outs) == 1 else outs


def _gelu_tanh(x):
    return x * (0.5 + 0.5 * jnp.tanh(0.7978845608028654 * (x + 0.044715 * (x * x * x))))


def _softplus(z):
    return jnp.maximum(z, 0.0) + jnp.log1p(jnp.exp(-jnp.abs(z)))


def _lru_body(gate_ref, xr_ref, cs_ref, h0_ref, cw_ref, cb_ref, wg_ref, bg_ref, lam_ref,
              z_ref, ht_ref, h_sc, tail_sc, *, n_steps, n_batch, tc_cols, batch_major_out):
    t = pl.program_id(1)
    n_tail = (CONV_W - 1) * n_batch
    n_rows = n_steps * n_batch
    perm = _perm_matrix(n_batch, False) if batch_major_out else None

    @pl.when(t == 0)
    def _():
        h_sc[...] = h0_ref[...]
        tail_sc[...] = cs_ref[...]

    cw = cw_ref[...]
    decay = (-0.5 * LRU_C * LOG2E) * _softplus(-lam_ref[...])
    for n in range(tc_cols // LRU_BLOCK):
        cols = slice(n * LRU_BLOCK, (n + 1) * LRU_BLOCK)
        x = xr_ref[:, cols]
        x_ext = jnp.concatenate([tail_sc[:, cols], x], axis=0)
        tail_sc[:, cols] = x_ext[n_rows:n_rows + n_tail]
        xc = cb_ref[:, cols] + cw[0:1, cols] * x_ext[0:n_rows]
        for j in range(1, CONV_W):
            xc = xc + cw[j:j + 1, cols] * x_ext[j * n_batch:j * n_batch + n_rows]
        half_pre = 0.5 * (jnp.dot(xc.astype(BF16), wg_ref[n], preferred_element_type=F32) + bg_ref[n])
        tr = jnp.tanh(half_pre[:, :LRU_BLOCK])
        ti = jnp.tanh(half_pre[:, LRU_BLOCK:])
        d = decay[:, cols]
        a = jnp.exp2(d * tr + d)
        gain2 = 1.0 - a * a
        u = (gain2 * lax.rsqrt(jnp.maximum(gain2, 1e-30))) * ((0.5 + 0.5 * ti) * xc)
        h = h_sc[:, cols]
        hs = []
        for s in range(n_steps):
            rows = slice(s * n_batch, (s + 1) * n_batch)
            h = a[rows] * h + u[rows]
            hs.append(h)
        h_sc[:, cols] = h
        hs = jnp.concatenate(hs, axis=0)
        z = (_gelu_tanh(gate_ref[:, cols]) * hs).astype(z_ref.dtype)
        if batch_major_out:
            p_rows = n_batch * PERM_STEPS
            for part in range(n_steps // PERM_STEPS):
                zp = jnp.dot(perm, z[part * p_rows:(part + 1) * p_rows], preferred_element_type=F32)
                z_ref[:, part * PERM_STEPS:(part + 1) * PERM_STEPS, cols] = (
                    zp.astype(z_ref.dtype).reshape(n_batch, PERM_STEPS, LRU_BLOCK))
        else:
            z_ref[:, cols] = z

    @pl.when(t == pl.num_programs(1) - 1)
    def _():
        ht_ref[...] = h_sc[...]


def _lru(gx, conv_state, h0, conv_w, conv_b, w_gates, b_gates, lam, *, n_batch, n_steps, batch_major_out,
         tc_cols=512):
    m = gx.shape[0]
    c = D_RNN
    n_rows = n_steps * n_batch
    n_tail = (CONV_W - 1) * n_batch
    ncb = c // tc_cols
    gpb = tc_cols // LRU_BLOCK
    if batch_major_out:
        z_spec = pl.BlockSpec((n_batch, n_steps, tc_cols), lambda ci, ti: (0, ti, ci))
        z_shape = jax.ShapeDtypeStruct((n_batch, m // n_batch, c), BF16)
    else:
        z_spec = pl.BlockSpec((n_rows, tc_cols), lambda ci, ti: (ti, ci))
        z_shape = jax.ShapeDtypeStruct((m, c), BF16)
    z, ht = pl.pallas_call(
        functools.partial(_lru_body, n_steps=n_steps, n_batch=n_batch, tc_cols=tc_cols,
                          batch_major_out=batch_major_out),
        grid=(ncb, m // n_rows),
        in_specs=[
            pl.BlockSpec((n_rows, tc_cols), lambda ci, ti: (ti, ci)),
            pl.BlockSpec((n_rows, tc_cols), lambda ci, ti: (ti, ncb + ci)),
            pl.BlockSpec((n_tail, tc_cols), lambda ci, ti: (0, ci)),
            pl.BlockSpec((n_batch, tc_cols), lambda ci, ti: (0, ci)),
            pl.BlockSpec((CONV_W, tc_cols), lambda ci, ti: (0, ci)),
            pl.BlockSpec((1, tc_cols), lambda ci, ti: (0, ci)),
            pl.BlockSpec((gpb, LRU_BLOCK, 2 * LRU_BLOCK), lambda ci, ti: (ci, 0, 0)),
            pl.BlockSpec((gpb, 1, 2 * LRU_BLOCK), lambda ci, ti: (ci, 0, 0)),
            pl.BlockSpec((1, tc_cols), lambda ci, ti: (0, ci)),
        ],
        out_specs=[z_spec, pl.BlockSpec((n_batch, tc_cols), lambda ci, ti: (0, ci))],
        out_shape=[z_shape, jax.ShapeDtypeStruct((n_batch, c), F32)],
        scratch_shapes=[pltpu.VMEM((n_batch, tc_cols), F32), pltpu.VMEM((n_tail, tc_cols), F32)],
        compiler_params=_params(("arbitrary", "arbitrary")),
        name="lru",
    )(gx, gx, conv_state, h0, conv_w, conv_b, w_gates, b_gates, lam)
    return z, ht


def _attn_prompt_body(sink_ref, q_ref, kvc_ref, kvp_ref, o_ref):
    i = pl.program_id(1)
    kv = jnp.concatenate([kvp_ref[...], kvc_ref[...]], axis=0).astype(BF16)
    n_keys = 2 * WINDOW
    low_k = lax.broadcasted_iota(jnp.int32, (n_keys, LANES), 1) < HEAD_DIM
    low_q = lax.broadcasted_iota(jnp.int32, (WINDOW, LANES), 1) < HEAD_DIM
    zero = jnp.zeros((n_keys, LANES), BF16)
    one = jnp.ones((n_keys, LANES), BF16)
    q_pos = lax.broadcasted_iota(jnp.int32, (WINDOW, n_keys), 0)
    k_pos = lax.broadcasted_iota(jnp.int32, (WINDOW, n_keys), 1) - WINDOW
    dq = q_pos - k_pos
    k_min = jnp.where(i == 0, 0, -WINDOW)
    allowed = jnp.logical_and(jnp.logical_and(dq >= 0, dq < WINDOW), k_pos >= k_min)
    bias = jnp.where(allowed, 0.0, -jnp.inf).astype(F32)
    dims = (((1,), (1,)), ((), ()))

    for h in range(N_KV_HEADS):
        c0 = (h // 2) * LANES
        k_t = kv[:, c0:c0 + LANES]
        v_t = kv[:, KV_DIM + c0:KV_DIM + c0 + LANES]
        k_sw = pltpu.roll(k_t, HEAD_DIM, 1)
        v_sw = pltpu.roll(v_t, HEAD_DIM, 1)
        k_lo, k_hi = (k_t, k_sw) if h % 2 == 0 else (k_sw, k_t)
        v_lo, v_hi = (v_t, v_sw) if h % 2 == 0 else (v_sw, v_t)
        k_a = jnp.where(low_k, k_lo, zero)
        k_b = jnp.where(low_k, zero, k_hi)
        v_a = jnp.where(low_k, v_lo, one)
        v_b = jnp.where(low_k, one, v_hi)
        n_pairs = GROUP // 2
        qc0 = h * GROUP * HEAD_DIM
        q_h = jnp.concatenate([q_ref[:, qc0 + p * LANES:qc0 + (p + 1) * LANES] for p in range(n_pairs)], axis=0)
        acc, esink = [], []
        for side, (k_x, v_x) in enumerate(((k_a, v_a), (k_b, v_b))):
            s = lax.dot_general(q_h, k_x, dims, preferred_element_type=F32)
            probs, es = [], []
            for p in range(n_pairs):
                s_p = s[p * WINDOW:(p + 1) * WINDOW] + bias
                sink = sink_ref[h * GROUP + 2 * p + side] * LOG2E
                m = jnp.maximum(jnp.max(s_p, axis=-1, keepdims=True), sink)
                probs.append(jnp.exp2(s_p - m).astype(BF16))
                es.append(jnp.exp2(sink - m))
            acc.append(jnp.dot(jnp.concatenate(probs, axis=0), v_x, preferred_element_type=F32))
            esink.append(es)
        for p in range(n_pairs):
            rows = slice(p * WINDOW, (p + 1) * WINDOW)
            acc_a, acc_b = acc[0][rows], acc[1][rows]
            num = jnp.where(low_q, acc_a, acc_b)
            den = pltpu.roll(jnp.where(low_q, acc_b, acc_a), HEAD_DIM, 1) + jnp.where(low_q, esink[0][p], esink[1][p])
            o_ref[:, qc0 + p * LANES:qc0 + (p + 1) * LANES] = (num / den).astype(o_ref.dtype)


def _attn_prompt(q3, kv3, sinks):
    b, t, _ = q3.shape
    nb = t // WINDOW
    return pl.pallas_call(
        _attn_prompt_body,
        grid=(b, nb),
        in_specs=[
            pl.BlockSpec(memory_space=pltpu.SMEM),
            pl.BlockSpec((None, WINDOW, Q_DIM), lambda bi, i: (bi, i, 0)),
            pl.BlockSpec((None, WINDOW, 2 * KV_DIM), lambda bi, i: (bi, i, 0)),
            pl.BlockSpec((None, WINDOW, 2 * KV_DIM), lambda bi, i: (bi, jnp.maximum(i - 1, 0), 0)),
        ],
        out_specs=pl.BlockSpec((None, WINDOW, Q_DIM), lambda bi, i: (bi, i, 0)),
        out_shape=jax.ShapeDtypeStruct((b, t, Q_DIM), BF16),
        compiler_params=_params(("arbitrary", "arbitrary")),
        name="attn_prompt",
    )(sinks, q3, kv3, kv3)


SAMPLE_BATCH_TILE = 8


def _attn_sample_body(sink_ref, q_ref, kvn_ref, ck_ref, cv_ref, o_ref, *, n_steps):
    bb = SAMPLE_BATCH_TILE
    n_new = n_steps * bb
    n_cache = bb * WINDOW
    n_keys = n_cache + n_new
    n_keys_pad = -(-n_keys // LANES) * LANES
    q = q_ref[...].reshape(n_new, Q_DIM)
    kvn = kvn_ref[...].reshape(n_new, 2 * KV_DIM)
    pad = jnp.zeros((n_keys_pad - n_keys, KV_DIM), F32)
    k_all = jnp.concatenate([ck_ref[...].reshape(n_cache, KV_DIM), kvn[:, :KV_DIM], pad], axis=0).astype(BF16)
    v_all = jnp.concatenate([cv_ref[...].reshape(n_cache, KV_DIM), kvn[:, KV_DIM:], pad], axis=0).astype(BF16)
    n_rows = GROUP * n_new
    r = lax.broadcasted_iota(jnp.int32, (n_rows, n_keys_pad), 0)
    c = lax.broadcasted_iota(jnp.int32, (n_rows, n_keys_pad), 1)
    b_r = r % bb
    t_r = (r // bb) % n_steps
    is_cache = c < n_cache
    c2 = jnp.maximum(c - n_cache, 0)
    b_c = jnp.where(is_cache, c // WINDOW, c2 % bb)
    k_step = jnp.where(is_cache, c % WINDOW - WINDOW, c2 // bb)
    dq = t_r - k_step
    allowed = jnp.logical_and(jnp.logical_and(b_c == b_r, c < n_keys),
                              jnp.logical_and(dq >= 0, dq < WINDOW))
    grp = lax.broadcasted_iota(jnp.int32, (n_rows, 1), 0) // n_new
    dims = (((1,), (1,)), ((), ()))
    for h in range(N_KV_HEADS):
        k_h = k_all[:, h * HEAD_DIM:(h + 1) * HEAD_DIM]
        v_h = v_all[:, h * HEAD_DIM:(h + 1) * HEAD_DIM]
        q_h = jnp.concatenate(
            [q[:, (h * GROUP + g) * HEAD_DIM:(h * GROUP + g + 1) * HEAD_DIM] for g in range(GROUP)], axis=0)
        sink = jnp.zeros((n_rows, 1), F32)
        for g in range(GROUP):
            sink = jnp.where(grp == g, sink_ref[h * GROUP + g] * LOG2E, sink)
        s = lax.dot_general(q_h, k_h, dims, preferred_element_type=F32)
        s = jnp.where(allowed, s, -jnp.inf)
        m = jnp.maximum(jnp.max(s, axis=-1, keepdims=True), sink)
        p = jnp.exp2(s - m)
        denom = jnp.sum(p, axis=-1, keepdims=True) + jnp.exp2(sink - m)
        o_h = jnp.dot(p.astype(BF16), v_h, preferred_element_type=F32) / denom
        for g in range(GROUP):
            c0 = (h * GROUP + g) * HEAD_DIM
            o_ref[:, :, c0:c0 + HEAD_DIM] = (
                o_h[g * n_new:(g + 1) * n_new].reshape(n_steps, bb, HEAD_DIM).astype(o_ref.dtype))


def _attn_sample(q3, kvn3, cache_k, cache_v, sinks):
    n_steps, b, _ = q3.shape
    bb = SAMPLE_BATCH_TILE
    return pl.pallas_call(
        functools.partial(_attn_sample_body, n_steps=n_steps),
        grid=(b // bb,),
        in_specs=[
            pl.BlockSpec(memory_space=pltpu.SMEM),
            pl.BlockSpec((n_steps, bb, Q_DIM), lambda i: (0, i, 0)),
            pl.BlockSpec((n_steps, bb, 2 * KV_DIM), lambda i: (0, i, 0)),
            pl.BlockSpec((bb, WINDOW, KV_DIM), lambda i: (i, 0, 0)),
            pl.BlockSpec((bb, WINDOW, KV_DIM), lambda i: (i, 0, 0)),
        ],
        out_specs=pl.BlockSpec((n_steps, bb, Q_DIM), lambda i: (0, i, 0)),
        out_shape=jax.ShapeDtypeStruct((n_steps, b, Q_DIM), BF16),
        compiler_params=_params(("arbitrary",)),
        name="attn_sample",
    )(sinks, q3, kvn3, cache_k, cache_v)


TILES = {
    "rec_in": dict(bm=1024, bn=2048),
    "rec_out": dict(bm=512, bn=2048),
    "attn_out": dict(bm=512, bn=2048),
    "q_proj": dict(bm=512, bn=2048),
    "kv_proj": dict(bm=1024, bn=512),
    "mlp_up": dict(bm=1024, bn=1024),
    "mlp_down": dict(bm=1024, bn=1024, bk=2048),
}


def _mm(name, x, w, **kw):
    return _matmul(x, w, name=name, **TILES[name], **kw)


def _trunk(xn, x, lru_fn, attn_fn, w):
    gx = _mm("rec_in", xn, w["rec_w_in"], out_dtype=F32)
    z, h_last = lru_fn(gx)
    h, hb, ssq = _mm("rec_out", z, w["rec_w_out"], out_dtype=F32, resid=x, gains=[w["norm_mlp"][0]], ssq_out=True)
    hid = _mm("mlp_up", hb, w["mlp_w_up"], layer=0, out_dtype=BF16, act="relu2", ssq_in=ssq)
    h, hkv, hq, ssq = _mm("mlp_down", hid, w["mlp_w_down"], layer=0, out_dtype=F32, resid=h,
                          gains=[w["kv_norm"], w["norm_mix"][1]], ssq_out=True)
    kv = _mm("kv_proj", hkv, w["w_kv"], out_dtype=F32, ssq_in=ssq)
    q = _mm("q_proj", hq, w["attn_w_q"], out_dtype=BF16, act="qscale", ssq_in=ssq)
    o = attn_fn(q, kv)
    h, hb, ssq = _mm("attn_out", o, w["attn_w_o"], out_dtype=F32, resid=h, gains=[w["norm_mlp"][1]], ssq_out=True)
    hid = _mm("mlp_up", hb, w["mlp_w_up"], layer=1, out_dtype=BF16, act="relu2", ssq_in=ssq)
    h = _mm("mlp_down", hid, w["mlp_w_down"], layer=1, out_dtype=F32, resid=h)
    out = _rmsnorm(h, w["final_norm"], F32)
    return out, gx, h_last, kv


def kernel(x_prompt, x_sample, state_conv, state_h, cache_k, cache_v, norm_mix, norm_mlp, rec_w_in,
           rec_conv_w, rec_conv_b, rec_gate_a_w, rec_gate_a_b, rec_gate_x_w, rec_gate_x_b, rec_lambda,
           rec_w_out, kv_norm, w_kv, attn_w_q, attn_sinks, attn_w_o, mlp_w_up, mlp_w_down, final_norm):
    b, t, d = x_prompt.shape
    sb, st, _ = x_sample.shape
    c = D_RNN
    n_tail = CONV_W - 1

    w = {
        "norm_mix": norm_mix, "norm_mlp": norm_mlp, "kv_norm": kv_norm, "final_norm": final_norm,
        "rec_w_in": rec_w_in.astype(BF16), "rec_w_out": rec_w_out.astype(BF16),
        "w_kv": w_kv[None].astype(BF16), "attn_w_q": attn_w_q.astype(BF16), "attn_w_o": attn_w_o.astype(BF16),
        "mlp_w_up": mlp_w_up.astype(BF16), "mlp_w_down": mlp_w_down.astype(BF16),
    }
    conv_w = rec_conv_w[0]
    conv_b = rec_conv_b[0].reshape(1, c)
    lam = rec_lambda[0].reshape(1, c)
    w_gates = jnp.concatenate([rec_gate_a_w[0], rec_gate_x_w[0]], axis=-1).astype(BF16)
    b_gates = jnp.concatenate([rec_gate_a_b[0], rec_gate_x_b[0]], axis=-1).reshape(N_LRU_BLOCKS, 1, 2 * LRU_BLOCK)
    sinks = attn_sinks[0].astype(F32)

    def lru_prompt(gx):
        z, ht = _lru(gx, jnp.zeros((n_tail * b, c), F32), jnp.zeros((b, c), F32),
                     conv_w, conv_b, w_gates, b_gates, lam, n_batch=b, n_steps=64, batch_major_out=True)
        return z.reshape(b * t, c), ht

    def attn_prompt(q, kv):
        o = _attn_prompt(q.reshape(b, t, Q_DIM), kv.reshape(b, t, 2 * KV_DIM), sinks)
        return o.reshape(b * t, Q_DIM)

    xn_p = _rmsnorm_to_time_major(x_prompt, norm_mix[0])
    y_p, gx_p, h_p, kv_p = _trunk(xn_p, x_prompt.reshape(b * t, d), lru_prompt, attn_prompt, w)
    y_prompt = y_p.reshape(b, t, d)
    conv_p = jnp.swapaxes(gx_p.reshape(t, b, 2 * c)[t - n_tail:, :, c:], 0, 1).reshape(b, 1, n_tail, c)
    keep = min(WINDOW, t)
    kv_tail = kv_p.reshape(b, t, 2 * KV_DIM)[:, t - keep:]
    new_k_prompt = kv_tail[..., :KV_DIM].reshape(b, keep, N_KV_HEADS, HEAD_DIM)
    new_v_prompt = kv_tail[..., KV_DIM:].reshape(b, keep, N_KV_HEADS, HEAD_DIM)

    conv_state_tb = jnp.swapaxes(state_conv[:, 0], 0, 1).reshape(n_tail * sb, c)
    h0 = state_h[:, 0]
    ck = cache_k.reshape(sb, WINDOW, KV_DIM)
    cv = cache_v.reshape(sb, WINDOW, KV_DIM)

    def lru_sample(gx):
        return _lru(gx, conv_state_tb, h0, conv_w, conv_b, w_gates, b_gates, lam, n_batch=sb, n_steps=st,
                    batch_major_out=False)

    def attn_sample(q, kv):
        o = _attn_sample(q.reshape(st, sb, Q_DIM), kv.reshape(st, sb, 2 * KV_DIM), ck, cv, sinks)
        return o.reshape(st * sb, Q_DIM)

    x_s = jnp.swapaxes(x_sample, 0, 1).reshape(st * sb, d)
    y_s, gx_s, h_s, kv_s = _trunk(_rmsnorm(x_s, norm_mix[0], BF16), x_s, lru_sample, attn_sample, w)
    y_sample = jnp.swapaxes(y_s.reshape(st, sb, d), 0, 1)
    conv_s = jnp.swapaxes(gx_s.reshape(st, sb, 2 * c)[st - n_tail:, :, c:], 0, 1).reshape(sb, 1, n_tail, c)
    kv_s = jnp.swapaxes(kv_s.reshape(st, sb, 2 * KV_DIM), 0, 1)
    new_k_sample = kv_s[..., :KV_DIM].reshape(sb, st, N_KV_HEADS, HEAD_DIM)
    new_v_sample = kv_s[..., KV_DIM:].reshape(sb, st, N_KV_HEADS, HEAD_DIM)

    return (y_prompt, y_sample, conv_p, h_p.reshape(b, 1, c), new_k_prompt, new_v_prompt,
            conv_s, h_s.reshape(sb, 1, c), new_k_sample, new_v_sample)
```

```python
import functools
import math

import jax
import jax.numpy as jnp
from jax import lax
from jax.experimental import pallas as pl
from jax.experimental.pallas import tpu as pltpu

F32 = jnp.float32
BF16 = jnp.bfloat16

D_MODEL = 2048
D_RNN = 2048
N_LRU_BLOCKS = 16
LRU_BLOCK = D_RNN // N_LRU_BLOCKS
CONV_W = 4
LRU_C = 8.0
N_HEADS = 32
N_KV_HEADS = 4
HEAD_DIM = 64
GROUP = N_HEADS // N_KV_HEADS
Q_DIM = N_HEADS * HEAD_DIM
KV_DIM = N_KV_HEADS * HEAD_DIM
WINDOW = 128
EPS = 1e-6
LOG2E = math.log2(math.e)

SUBLANES = 8
LANES = 128
VMEM_LIMIT_BYTES = 52 * 1024 * 1024


def _params(semantics):
    return pltpu.CompilerParams(dimension_semantics=semantics,
                                vmem_limit_bytes=VMEM_LIMIT_BYTES)


def _rmsnorm_body(x_ref, g_ref, o_ref):
    x = x_ref[...]
    y = x * lax.rsqrt(jnp.mean(x * x, axis=-1, keepdims=True) + EPS)
    o_ref[...] = (y * g_ref[...]).astype(o_ref.dtype)


def _rmsnorm(x, gain, out_dtype, tm=512):
    m, d = x.shape
    tm = min(tm, m)
    row_spec = pl.BlockSpec((tm, d), lambda i: (i, 0))
    return pl.pallas_call(
        _rmsnorm_body,
        grid=(m // tm,),
        in_specs=[row_spec, pl.BlockSpec((1, d), lambda i: (0, 0))],
        out_specs=row_spec,
        out_shape=jax.ShapeDtypeStruct((m, d), out_dtype),
        compiler_params=_params(("arbitrary",)),
        name="rmsnorm",
    )(x, gain.reshape(1, d).astype(F32))


PERM_STEPS = 32


def _perm_matrix(n_batch, to_time_major):
    n = n_batch * PERM_STEPS
    r = lax.broadcasted_iota(jnp.int32, (n, n), 0)
    c = lax.broadcasted_iota(jnp.int32, (n, n), 1)
    if to_time_major:
        src = (r % n_batch) * PERM_STEPS + r // n_batch
    else:
        src = (r % PERM_STEPS) * n_batch + r // PERM_STEPS
    return jnp.where(c == src, 1.0, 0.0).astype(BF16)


def _rmsnorm_tb_body(x_ref, g_ref, o_ref, *, n_batch, n_steps):
    perm = _perm_matrix(n_batch, True)
    rows = n_batch * PERM_STEPS
    g = g_ref[...]
    for part in range(n_steps // PERM_STEPS):
        x = x_ref[:, part * PERM_STEPS:(part + 1) * PERM_STEPS, :].reshape(rows, x_ref.shape[-1])
        y = x * lax.rsqrt(jnp.mean(x * x, axis=-1, keepdims=True) + EPS)
        y = (y * g).astype(BF16)
        o_ref[part * rows:(part + 1) * rows, :] = jnp.dot(perm, y, preferred_element_type=F32).astype(o_ref.dtype)


def _rmsnorm_to_time_major(x3, gain, n_steps=64):
    b, t, d = x3.shape
    return pl.pallas_call(
        functools.partial(_rmsnorm_tb_body, n_batch=b, n_steps=n_steps),
        grid=(t // n_steps,),
        in_specs=[pl.BlockSpec((b, n_steps, d), lambda i: (0, i, 0)), pl.BlockSpec((1, d), lambda i: (0, 0))],
        out_specs=pl.BlockSpec((n_steps * b, d), lambda i: (i, 0)),
        out_shape=jax.ShapeDtypeStruct((t * b, d), BF16),
        compiler_params=_params(("arbitrary",)),
        name="rmsnorm_tb",
    )(x3, gain.reshape(1, d).astype(F32))


def _matmul_body(*refs, nk, act, has_scale, has_resid, has_copy, has_ssq, side_gains, norm_dim):
    refs = list(refs)
    x_ref, w_ref = refs[:2]
    pos = 2
    s_ref = r_ref = b_ref = q_ref = None
    if has_scale:
        s_ref = refs[pos]
        pos += 1
    if has_resid:
        r_ref = refs[pos]
        pos += 1
    side_in = []
    for has_gain in side_gains:
        side_in.append((refs[pos], refs[pos + 1] if has_gain else None))
        pos += 2 if has_gain else 1
    o_ref = refs[pos]
    pos += 1
    if has_copy:
        b_ref = refs[pos]
        pos += 1
    if has_ssq:
        q_ref = refs[pos]
        pos += 1
    side_out = refs[pos:pos + len(side_gains)]
    pos += len(side_gains)
    acc_ref = refs[pos] if nk > 1 else None

    def partial_product():
        for (src_ref, g_ref), dst_ref in zip(side_in, side_out):
            chunk = src_ref[...]
            if g_ref is not None:
                chunk = chunk * g_ref[...]
            dst_ref[...] = chunk.astype(dst_ref.dtype)
        return jnp.dot(x_ref[...], w_ref[...], preferred_element_type=F32)

    def finish(acc):
        if s_ref is not None:
            acc = acc * lax.rsqrt(jnp.sum(s_ref[...], axis=-1, keepdims=True) * (1.0 / norm_dim) + EPS)
        if act == "relu2":
            r = jnp.maximum(acc, 0.0)
            acc = r * r
        elif act == "qscale":
            acc = acc * (HEAD_DIM ** -0.5 * LOG2E)
        if r_ref is not None:
            acc = r_ref[...] + acc
        o_ref[...] = acc.astype(o_ref.dtype)
        if b_ref is not None:
            b_ref[...] = acc.astype(b_ref.dtype)
        if q_ref is not None:
            sq = acc * acc
            tot = sq[:, 0:LANES]
            for c in range(1, sq.shape[1] // LANES):
                tot = tot + sq[:, c * LANES:(c + 1) * LANES]
            q_ref[...] = tot

    if nk == 1:
        finish(partial_product())
    else:
        k = pl.program_id(2)

        @pl.when(k == 0)
        def _():
            acc_ref[...] = partial_product()

        @pl.when(jnp.logical_and(k > 0, k < nk - 1))
        def _():
            acc_ref[...] += partial_product()

        @pl.when(k == nk - 1)
        def _():
            finish(acc_ref[...] + partial_product())


def _matmul(x, w, *, out_dtype, layer=0, act=None, ssq_in=None, resid=None, bf16_copy=False, ssq_out=False,
            side_casts=(), bm=1024, bn=1024, bk=None, name="matmul"):
    m = x.shape[0]
    _, kdim, n = w.shape
    bm, bn = min(bm, m), min(bn, n)
    bk = kdim if bk is None else min(bk, kdim)
    nk, nb = kdim // bk, n // bn
    mb = m // bm
    n_steps = mb * nb * nk

    in_specs = [pl.BlockSpec((bm, bk), lambda i, j, k: (i, k)),
                pl.BlockSpec((None, bk, bn), lambda i, j, k: (layer, k, j))]
    args = [x, w]
    if ssq_in is not None:
        in_specs.append(pl.BlockSpec((bm, ssq_in.shape[1]), lambda i, j, k: (i, 0)))
        args.append(ssq_in)
    tile_spec = pl.BlockSpec((bm, bn), lambda i, j, k: (i, j))
    if resid is not None:
        in_specs.append(tile_spec)
        args.append(resid)
    step = lambda i, j, k: (i * nb + j) * nk + k
    for src, gain in side_casts:
        rows, cols = src.shape
        chunk = rows // n_steps
        assert chunk * n_steps == rows and chunk % (2 * SUBLANES) == 0, (name, rows, n_steps)
        in_specs.append(pl.BlockSpec((chunk, cols), lambda i, j, k: (step(i, j, k), 0)))
        args.append(src)
        if gain is not None:
            in_specs.append(pl.BlockSpec((chunk, 1), lambda i, j, k: (step(i, j, k), 0)))
            args.append(gain.reshape(rows, 1).astype(F32))

    out_specs = [tile_spec]
    out_shape = [jax.ShapeDtypeStruct((m, n), out_dtype)]
    if bf16_copy:
        out_specs.append(tile_spec)
        out_shape.append(jax.ShapeDtypeStruct((m, n), BF16))
    if ssq_out:
        out_specs.append(pl.BlockSpec((bm, LANES), lambda i, j, k: (i, j)))
        out_shape.append(jax.ShapeDtypeStruct((m, nb * LANES), F32))
    for src, _ in side_casts:
        rows, cols = src.shape
        out_specs.append(pl.BlockSpec((rows // n_steps, cols), lambda i, j, k: (step(i, j, k), 0)))
        out_shape.append(jax.ShapeDtypeStruct((rows, cols), BF16))

    outs = pl.pallas_call(
        functools.partial(_matmul_body, nk=nk, act=act, has_scale=ssq_in is not None,
                          has_resid=resid is not None, has_copy=bf16_copy, has_ssq=ssq_out,
                          side_gains=tuple(g is not None for _, g in side_casts), norm_dim=kdim),
        grid=(mb, nb, nk),
        in_specs=in_specs,
        out_specs=out_specs,
        out_shape=out_shape,
        scratch_shapes=[pltpu.VMEM((bm, bn), F32)] if nk > 1 else [],
        compiler_params=_params(("arbitrary", "arbitrary", "arbitrary")),
        name=name,
    )(*args)
    return outs[0] if len(outs) == 1 else outs


def _gelu_tanh(x):
    return x * (0.5 + 0.5 * jnp.tanh(0.7978845608028654 * (x + 0.044715 * (x * x * x))))


def _softplus(z):
    return jnp.maximum(z, 0.0) + jnp.log1p(jnp.exp(-jnp.abs(z)))


def _lru_body(gate_ref, xr_ref, cs_ref, h0_ref, cw_ref, cb_ref, wg_ref, bg_ref, lam_ref,
              z_ref, ht_ref, h_sc, tail_sc, *, n_steps, n_batch, tc_cols, batch_major_out):
    t = pl.program_id(1)
    n_tail = (CONV_W - 1) * n_batch
    n_rows = n_steps * n_batch
    perm = _perm_matrix(n_batch, False) if batch_major_out else None

    @pl.when(t == 0)
    def _():
        h_sc[...] = h0_ref[...]
        tail_sc[...] = cs_ref[...]

    cw = cw_ref[...]
    decay = (-0.5 * LRU_C * LOG2E) * _softplus(-lam_ref[...])
    for n in range(tc_cols // LRU_BLOCK):
        cols = slice(n * LRU_BLOCK, (n + 1) * LRU_BLOCK)
        x = xr_ref[:, cols]
        x_ext = jnp.concatenate([tail_sc[:, cols], x], axis=0)
        tail_sc[:, cols] = x_ext[n_rows:n_rows + n_tail]
        xc = cb_ref[:, cols] + cw[0:1, cols] * x_ext[0:n_rows]
        for j in range(1, CONV_W):
            xc = xc + cw[j:j + 1, cols] * x_ext[j * n_batch:j * n_batch + n_rows]
        half_pre = 0.5 * (jnp.dot(xc.astype(BF16), wg_ref[n], preferred_element_type=F32) + bg_ref[n])
        tr = jnp.tanh(half_pre[:, :LRU_BLOCK])
        ti = jnp.tanh(half_pre[:, LRU_BLOCK:])
        d = decay[:, cols]
        a = jnp.exp2(d * tr + d)
        gain2 = 1.0 - a * a
        u = (gain2 * lax.rsqrt(jnp.maximum(gain2, 1e-30))) * ((0.5 + 0.5 * ti) * xc)
        h = h_sc[:, cols]
        hs = []
        for s in range(n_steps):
            rows = slice(s * n_batch, (s + 1) * n_batch)
            h = a[rows] * h + u[rows]
            hs.append(h)
        h_sc[:, cols] = h
        hs = jnp.concatenate(hs, axis=0)
        z = (_gelu_tanh(gate_ref[:, cols]) * hs).astype(z_ref.dtype)
        if batch_major_out:
            p_rows = n_batch * PERM_STEPS
            for part in range(n_steps // PERM_STEPS):
                zp = jnp.dot(perm, z[part * p_rows:(part + 1) * p_rows], preferred_element_type=F32)
                z_ref[:, part * PERM_STEPS:(part + 1) * PERM_STEPS, cols] = (
                    zp.astype(z_ref.dtype).reshape(n_batch, PERM_STEPS, LRU_BLOCK))
        else:
            z_ref[:, cols] = z

    @pl.when(t == pl.num_programs(1) - 1)
    def _():
        ht_ref[...] = h_sc[...]


def _lru(gx, conv_state, h0, conv_w, conv_b, w_gates, b_gates, lam, *, n_batch, n_steps, batch_major_out,
         tc_cols=512):
    m = gx.shape[0]
    c = D_RNN
    n_rows = n_steps * n_batch
    n_tail = (CONV_W - 1) * n_batch
    ncb = c // tc_cols
    gpb = tc_cols // LRU_BLOCK
    if batch_major_out:
        z_spec = pl.BlockSpec((n_batch, n_steps, tc_cols), lambda ci, ti: (0, ti, ci))
        z_shape = jax.ShapeDtypeStruct((n_batch, m // n_batch, c), BF16)
    else:
        z_spec = pl.BlockSpec((n_rows, tc_cols), lambda ci, ti: (ti, ci))
        z_shape = jax.ShapeDtypeStruct((m, c), BF16)
    z, ht = pl.pallas_call(
        functools.partial(_lru_body, n_steps=n_steps, n_batch=n_batch, tc_cols=tc_cols,
                          batch_major_out=batch_major_out),
        grid=(ncb, m // n_rows),
        in_specs=[
            pl.BlockSpec((n_rows, tc_cols), lambda ci, ti: (ti, ci)),
            pl.BlockSpec((n_rows, tc_cols), lambda ci, ti: (ti, ncb + ci)),
            pl.BlockSpec((n_tail, tc_cols), lambda ci, ti: (0, ci)),
            pl.BlockSpec((n_batch, tc_cols), lambda ci, ti: (0, ci)),
            pl.BlockSpec((CONV_W, tc_cols), lambda ci, ti: (0, ci)),
            pl.BlockSpec((1, tc_cols), lambda ci, ti: (0, ci)),
            pl.BlockSpec((gpb, LRU_BLOCK, 2 * LRU_BLOCK), lambda ci, ti: (ci, 0, 0)),
            pl.BlockSpec((gpb, 1, 2 * LRU_BLOCK), lambda ci, ti: (ci, 0, 0)),
            pl.BlockSpec((1, tc_cols), lambda ci, ti: (0, ci)),
        ],
        out_specs=[z_spec, pl.BlockSpec((n_batch, tc_cols), lambda ci, ti: (0, ci))],
        out_shape=[z_shape, jax.ShapeDtypeStruct((n_batch, c), F32)],
        scratch_shapes=[pltpu.VMEM((n_batch, tc_cols), F32), pltpu.VMEM((n_tail, tc_cols), F32)],
        compiler_params=_params(("arbitrary", "arbitrary")),
        name="lru",
    )(gx, gx, conv_state, h0, conv_w, conv_b, w_gates, b_gates, lam)
    return z, ht


def _attn_prompt_body(sink_ref, q_ref, kvc_ref, kvp_ref, o_ref):
    i = pl.program_id(1)
    kv = jnp.concatenate([kvp_ref[...], kvc_ref[...]], axis=0).astype(BF16)
    n_keys = 2 * WINDOW
    low_k = lax.broadcasted_iota(jnp.int32, (n_keys, LANES), 1) < HEAD_DIM
    low_q = lax.broadcasted_iota(jnp.int32, (WINDOW, LANES), 1) < HEAD_DIM
    zero = jnp.zeros((n_keys, LANES), BF16)
    one = jnp.ones((n_keys, LANES), BF16)
    q_pos = lax.broadcasted_iota(jnp.int32, (WINDOW, n_keys), 0)
    k_pos = lax.broadcasted_iota(jnp.int32, (WINDOW, n_keys), 1) - WINDOW
    dq = q_pos - k_pos
    k_min = jnp.where(i == 0, 0, -WINDOW)
    allowed = jnp.logical_and(jnp.logical_and(dq >= 0, dq < WINDOW), k_pos >= k_min)
    bias = jnp.where(allowed, 0.0, -jnp.inf).astype(F32)
    dims = (((1,), (1,)), ((), ()))

    for h in range(N_KV_HEADS):
        c0 = (h // 2) * LANES
        k_t = kv[:, c0:c0 + LANES]
        v_t = kv[:, KV_DIM + c0:KV_DIM + c0 + LANES]
        k_sw = pltpu.roll(k_t, HEAD_DIM, 1)
        v_sw = pltpu.roll(v_t, HEAD_DIM, 1)
        k_lo, k_hi = (k_t, k_sw) if h % 2 == 0 else (k_sw, k_t)
        v_lo, v_hi = (v_t, v_sw) if h % 2 == 0 else (v_sw, v_t)
        k_a = jnp.where(low_k, k_lo, zero)
        k_b = jnp.where(low_k, zero, k_hi)
        v_a = jnp.where(low_k, v_lo, one)
        v_b = jnp.where(low_k, one, v_hi)
        n_pairs = GROUP // 2
        qc0 = h * GROUP * HEAD_DIM
        q_h = jnp.concatenate([q_ref[:, qc0 + p * LANES:qc0 + (p + 1) * LANES] for p in range(n_pairs)], axis=0)
        acc, esink = [], []
        for side, (k_x, v_x) in enumerate(((k_a, v_a), (k_b, v_b))):
            s = lax.dot_general(q_h, k_x, dims, preferred_element_type=F32)
            probs, es = [], []
            for p in range(n_pairs):
                s_p = s[p * WINDOW:(p + 1) * WINDOW] + bias
                sink = sink_ref[h * GROUP + 2 * p + side] * LOG2E
                m = jnp.maximum(jnp.max(s_p, axis=-1, keepdims=True), sink)
                probs.append(jnp.exp2(s_p - m).astype(BF16))
                es.append(jnp.exp2(sink - m))
            acc.append(jnp.dot(jnp.concatenate(probs, axis=0), v_x, preferred_element_type=F32))
            esink.append(es)
        for p in range(n_pairs):
            rows = slice(p * WINDOW, (p + 1) * WINDOW)
            acc_a, acc_b = acc[0][rows], acc[1][rows]
            num = jnp.where(low_q, acc_a, acc_b)
            den = pltpu.roll(jnp.where(low_q, acc_b, acc_a), HEAD_DIM, 1) + jnp.where(low_q, esink[0][p], esink[1][p])
            o_ref[:, qc0 + p * LANES:qc0 + (p + 1) * LANES] = (num / den).astype(o_ref.dtype)


def _attn_prompt(q3, kv3, sinks):
    b, t, _ = q3.shape
    nb = t // WINDOW
    return pl.pallas_call(
        _attn_prompt_body,
        grid=(b, nb),
        in_specs=[
            pl.BlockSpec(memory_space=pltpu.SMEM),
            pl.BlockSpec((None, WINDOW, Q_DIM), lambda bi, i: (bi, i, 0)),
            pl.BlockSpec((None, WINDOW, 2 * KV_DIM), lambda bi, i: (bi, i, 0)),
            pl.BlockSpec((None, WINDOW, 2 * KV_DIM), lambda bi, i: (bi, jnp.maximum(i - 1, 0), 0)),
        ],
        out_specs=pl.BlockSpec((None, WINDOW, Q_DIM), lambda bi, i: (bi, i, 0)),
        out_shape=jax.ShapeDtypeStruct((b, t, Q_DIM), BF16),
        compiler_params=_params(("arbitrary", "arbitrary")),
        name="attn_prompt",
    )(sinks, q3, kv3, kv3)


SAMPLE_BATCH_TILE = 8


def _attn_sample_body(sink_ref, q_ref, kvn_ref, ck_ref, cv_ref, o_ref, *, n_steps):
    bb = SAMPLE_BATCH_TILE
    n_new = n_steps * bb
    n_cache = bb * WINDOW
    n_keys = n_cache + n_new
    n_keys_pad = -(-n_keys // LANES) * LANES
    q = q_ref[...].reshape(n_new, Q_DIM)
    kvn = kvn_ref[...].reshape(n_new, 2 * KV_DIM)
    pad = jnp.zeros((n_keys_pad - n_keys, KV_DIM), F32)
    k_all = jnp.concatenate([ck_ref[...].reshape(n_cache, KV_DIM), kvn[:, :KV_DIM], pad], axis=0).astype(BF16)
    v_all = jnp.concatenate([cv_ref[...].reshape(n_cache, KV_DIM), kvn[:, KV_DIM:], pad], axis=0).astype(BF16)
    n_rows = GROUP * n_new
    r = lax.broadcasted_iota(jnp.int32, (n_rows, n_keys_pad), 0)
    c = lax.broadcasted_iota(jnp.int32, (n_rows, n_keys_pad), 1)
    b_r = r % bb
    t_r = (r // bb) % n_steps
    is_cache = c < n_cache
    c2 = jnp.maximum(c - n_cache, 0)
    b_c = jnp.where(is_cache, c // WINDOW, c2 % bb)
    k_step = jnp.where(is_cache, c % WINDOW - WINDOW, c2 // bb)
    dq = t_r - k_step
    allowed = jnp.logical_and(jnp.logical_and(b_c == b_r, c < n_keys),
                              jnp.logical_and(dq >= 0, dq < WINDOW))
    grp = lax.broadcasted_iota(jnp.int32, (n_rows, 1), 0) // n_new
    dims = (((1,), (1,)), ((), ()))
    for h in range(N_KV_HEADS):
        k_h = k_all[:, h * HEAD_DIM:(h + 1) * HEAD_DIM]
        v_h = v_all[:, h * HEAD_DIM:(h + 1) * HEAD_DIM]
        q_h = jnp.concatenate(
            [q[:, (h * GROUP + g) * HEAD_DIM:(h * GROUP + g + 1) * HEAD_DIM] for g in range(GROUP)], axis=0)
        sink = jnp.zeros((n_rows, 1), F32)
        for g in range(GROUP):
            sink = jnp.where(grp == g, sink_ref[h * GROUP + g] * LOG2E, sink)
        s = lax.dot_general(q_h, k_h, dims, preferred_element_type=F32)
        s = jnp.where(allowed, s, -jnp.inf)
        m = jnp.maximum(jnp.max(s, axis=-1, keepdims=True), sink)
        p = jnp.exp2(s - m)
        denom = jnp.sum(p, axis=-1, keepdims=True) + jnp.exp2(sink - m)
        o_h = jnp.dot(p.astype(BF16), v_h, preferred_element_type=F32) / denom
        for g in range(GROUP):
            c0 = (h * GROUP + g) * HEAD_DIM
            o_ref[:, :, c0:c0 + HEAD_DIM] = (
                o_h[g * n_new:(g + 1) * n_new].reshape(n_steps, bb, HEAD_DIM).astype(o_ref.dtype))


def _attn_sample(q3, kvn3, cache_k, cache_v, sinks):
    n_steps, b, _ = q3.shape
    bb = SAMPLE_BATCH_TILE
    return pl.pallas_call(
        functools.partial(_attn_sample_body, n_steps=n_steps),
        grid=(b // bb,),
        in_specs=[
            pl.BlockSpec(memory_space=pltpu.SMEM),
            pl.BlockSpec((n_steps, bb, Q_DIM), lambda i: (0, i, 0)),
            pl.BlockSpec((n_steps, bb, 2 * KV_DIM), lambda i: (0, i, 0)),
            pl.BlockSpec((bb, WINDOW, KV_DIM), lambda i: (i, 0, 0)),
            pl.BlockSpec((bb, WINDOW, KV_DIM), lambda i: (i, 0, 0)),
        ],
        out_specs=pl.BlockSpec((n_steps, bb, Q_DIM), lambda i: (0, i, 0)),
        out_shape=jax.ShapeDtypeStruct((n_steps, b, Q_DIM), BF16),
        compiler_params=_params(("arbitrary",)),
        name="attn_sample",
    )(sinks, q3, kvn3, cache_k, cache_v)


TILES = {
    "rec_in": dict(bm=512, bn=2048),
    "rec_out": dict(bm=512, bn=2048),
    "attn_out": dict(bm=512, bn=2048),
    "q_proj": dict(bm=512, bn=2048),
    "kv_proj": dict(bm=1024, bn=512),
    "mlp_up": dict(bm=1024, bn=1024),
    "mlp_down": dict(bm=1024, bn=1024, bk=2048),
}


def _mm(name, x, w, **kw):
    return _matmul(x, w, name=name, **TILES[name], **kw)


def _flat(w):
    return w.reshape(-1, w.shape[-1])


def _trunk(xn, x, lru_fn, attn_fn, raw, w):
    if "rec_w_out" not in w:
        gx, w_out, w_up = _mm("rec_in", xn, w["rec_w_in"], out_dtype=F32, side_casts=[
            (_flat(raw["rec_w_out"]), None), (_flat(raw["mlp_w_up"]), raw["norm_mlp"].reshape(-1))])
        w["rec_w_out"] = w_out.reshape(raw["rec_w_out"].shape)
        w["mlp_w_up"] = w_up.reshape(raw["mlp_w_up"].shape)
    else:
        gx = _mm("rec_in", xn, w["rec_w_in"], out_dtype=F32)
    z, h_last = lru_fn(gx)
    h, hb, ssq = _mm("rec_out", z, w["rec_w_out"], out_dtype=F32, resid=x, bf16_copy=True, ssq_out=True)
    if "mlp_w_down" not in w:
        hid, w_down, w_kv, w_q, w_o = _mm("mlp_up", hb, w["mlp_w_up"], layer=0, out_dtype=BF16, act="relu2",
                                          ssq_in=ssq, side_casts=[
            (_flat(raw["mlp_w_down"]), None), (raw["w_kv"], raw["kv_norm"]),
            (_flat(raw["attn_w_q"]), raw["norm_mix"][1]), (_flat(raw["attn_w_o"]), None)])
        w["mlp_w_down"] = w_down.reshape(raw["mlp_w_down"].shape)
        w["w_kv"] = w_kv[None]
        w["attn_w_q"] = w_q.reshape(raw["attn_w_q"].shape)
        w["attn_w_o"] = w_o.reshape(raw["attn_w_o"].shape)
    else:
        hid = _mm("mlp_up", hb, w["mlp_w_up"], layer=0, out_dtype=BF16, act="relu2", ssq_in=ssq)
    h, hb, ssq = _mm("mlp_down", hid, w["mlp_w_down"], layer=0, out_dtype=F32, resid=h, bf16_copy=True, ssq_out=True)
    kv = _mm("kv_proj", hb, w["w_kv"], out_dtype=F32, ssq_in=ssq)
    q = _mm("q_proj", hb, w["attn_w_q"], out_dtype=BF16, act="qscale", ssq_in=ssq)
    o = attn_fn(q, kv)
    h, hb, ssq = _mm("attn_out", o, w["attn_w_o"], out_dtype=F32, resid=h, bf16_copy=True, ssq_out=True)
    hid = _mm("mlp_up", hb, w["mlp_w_up"], layer=1, out_dtype=BF16, act="relu2", ssq_in=ssq)
    h = _mm("mlp_down", hid, w["mlp_w_down"], layer=1, out_dtype=F32, resid=h)
    out = _rmsnorm(h, raw["final_norm"], F32)
    return out, gx, h_last, kv


def kernel(x_prompt, x_sample, state_conv, state_h, cache_k, cache_v, norm_mix, norm_mlp, rec_w_in,
           rec_conv_w, rec_conv_b, rec_gate_a_w, rec_gate_a_b, rec_gate_x_w, rec_gate_x_b, rec_lambda,
           rec_w_out, kv_norm, w_kv, attn_w_q, attn_sinks, attn_w_o, mlp_w_up, mlp_w_down, final_norm):
    b, t, d = x_prompt.shape
    sb, st, _ = x_sample.shape
    c = D_RNN
    n_tail = CONV_W - 1

    raw = {
        "norm_mix": norm_mix, "norm_mlp": norm_mlp, "kv_norm": kv_norm, "final_norm": final_norm,
        "rec_w_out": rec_w_out, "w_kv": w_kv, "attn_w_q": attn_w_q, "attn_w_o": attn_w_o,
        "mlp_w_up": mlp_w_up, "mlp_w_down": mlp_w_down,
    }
    w = {"rec_w_in": rec_w_in.astype(BF16)}
    conv_w = rec_conv_w[0]
    conv_b = rec_conv_b[0].reshape(1, c)
    lam = rec_lambda[0].reshape(1, c)
    w_gates = jnp.concatenate([rec_gate_a_w[0], rec_gate_x_w[0]], axis=-1).astype(BF16)
    b_gates = jnp.concatenate([rec_gate_a_b[0], rec_gate_x_b[0]], axis=-1).reshape(N_LRU_BLOCKS, 1, 2 * LRU_BLOCK)
    sinks = attn_sinks[0].astype(F32)

    def lru_prompt(gx):
        z, ht = _lru(gx, jnp.zeros((n_tail * b, c), F32), jnp.zeros((b, c), F32),
                     conv_w, conv_b, w_gates, b_gates, lam, n_batch=b, n_steps=64, batch_major_out=True)
        return z.reshape(b * t, c), ht

    def attn_prompt(q, kv):
        o = _attn_prompt(q.reshape(b, t, Q_DIM), kv.reshape(b, t, 2 * KV_DIM), sinks)
        return o.reshape(b * t, Q_DIM)

    xn_p = _rmsnorm_to_time_major(x_prompt, norm_mix[0])
    y_p, gx_p, h_p, kv_p = _trunk(xn_p, x_prompt.reshape(b * t, d), lru_prompt, attn_prompt, raw, w)
    y_prompt = y_p.reshape(b, t, d)
    conv_p = jnp.swapaxes(gx_p.reshape(t, b, 2 * c)[t - n_tail:, :, c:], 0, 1).reshape(b, 1, n_tail, c)
    keep = min(WINDOW, t)
    kv_tail = kv_p.reshape(b, t, 2 * KV_DIM)[:, t - keep:]
    new_k_prompt = kv_tail[..., :KV_DIM].reshape(b, keep, N_KV_HEADS, HEAD_DIM)
    new_v_prompt = kv_tail[..., KV_DIM:].reshape(b, keep, N_KV_HEADS, HEAD_DIM)

    conv_state_tb = jnp.swapaxes(state_conv[:, 0], 0, 1).reshape(n_tail * sb, c)
    h0 = state_h[:, 0]
    ck = cache_k.reshape(sb, WINDOW, KV_DIM)
    cv = cache_v.reshape(sb, WINDOW, KV_DIM)

    def lru_sample(gx):
        return _lru(gx, conv_state_tb, h0, conv_w, conv_b, w_gates, b_gates, lam, n_batch=sb, n_steps=st,
                    batch_major_out=False)

    def attn_sample(q, kv):
        o = _attn_sample(q.reshape(st, sb, Q_DIM), kv.reshape(st, sb, 2 * KV_DIM), ck, cv, sinks)
        return o.reshape(st * sb, Q_DIM)

    x_s = jnp.swapaxes(x_sample, 0, 1).reshape(st * sb, d)
    y_s, gx_s, h_s, kv_s = _trunk(_rmsnorm(x_s, norm_mix[0], BF16), x_s, lru_sample, attn_sample, raw, w)
    y_sample = jnp.swapaxes(y_s.reshape(st, sb, d), 0, 1)
    conv_s = jnp.swapaxes(gx_s.reshape(st, sb, 2 * c)[st - n_tail:, :, c:], 0, 1).reshape(sb, 1, n_tail, c)
    kv_s = jnp.swapaxes(kv_s.reshape(st, sb, 2 * KV_DIM), 0, 1)
    new_k_sample = kv_s[..., :KV_DIM].reshape(sb, st, N_KV_HEADS, HEAD_DIM)
    new_v_sample = kv_s[..., KV_DIM:].reshape(sb, st, N_KV_HEADS, HEAD_DIM)

    return (y_prompt, y_sample, conv_p, h_p.reshape(b, 1, c), new_k_prompt, new_v_prompt,
            conv_s, h_s.reshape(sb, 1, c), new_k_sample, new_v_sample)
```

```python
import functools
import math

import jax
import jax.numpy as jnp
from jax import lax
from jax.experimental import pallas as pl
from jax.experimental.pallas import tpu as pltpu

F32 = jnp.float32
BF16 = jnp.bfloat16

D_MODEL = 2048
D_RNN = 2048
N_LRU_BLOCKS = 16
LRU_BLOCK = D_RNN // N_LRU_BLOCKS
CONV_W = 4
LRU_C = 8.0
N_HEADS = 32
N_KV_HEADS = 4
HEAD_DIM = 64
GROUP = N_HEADS // N_KV_HEADS
Q_DIM = N_HEADS * HEAD_DIM
KV_DIM = N_KV_HEADS * HEAD_DIM
WINDOW = 128
EPS = 1e-6
LOG2E = math.log2(math.e)

SUBLANES = 8
LANES = 128
VMEM_LIMIT_BYTES = 52 * 1024 * 1024


def _params(semantics):
    return pltpu.CompilerParams(dimension_semantics=semantics,
                                vmem_limit_bytes=VMEM_LIMIT_BYTES)


def _rmsnorm_body(x_ref, g_ref, o_ref):
    x = x_ref[...]
    y = x * lax.rsqrt(jnp.mean(x * x, axis=-1, keepdims=True) + EPS)
    o_ref[...] = (y * g_ref[...]).astype(o_ref.dtype)


def _rmsnorm(x, gain, out_dtype, tm=512):
    m, d = x.shape
    tm = min(tm, m)
    row_spec = pl.BlockSpec((tm, d), lambda i: (i, 0))
    return pl.pallas_call(
        _rmsnorm_body,
        grid=(m // tm,),
        in_specs=[row_spec, pl.BlockSpec((1, d), lambda i: (0, 0))],
        out_specs=row_spec,
        out_shape=jax.ShapeDtypeStruct((m, d), out_dtype),
        compiler_params=_params(("arbitrary",)),
        name="rmsnorm",
    )(x, gain.reshape(1, d).astype(F32))


PERM_STEPS = 32


def _perm_matrix(n_batch, to_time_major):
    n = n_batch * PERM_STEPS
    r = lax.broadcasted_iota(jnp.int32, (n, n), 0)
    c = lax.broadcasted_iota(jnp.int32, (n, n), 1)
    if to_time_major:
        src = (r % n_batch) * PERM_STEPS + r // n_batch
    else:
        src = (r % PERM_STEPS) * n_batch + r // PERM_STEPS
    return jnp.where(c == src, 1.0, 0.0).astype(BF16)


def _rmsnorm_tb_body(x_ref, g_ref, o_ref, *, n_batch, n_steps):
    perm = _perm_matrix(n_batch, True)
    rows = n_batch * PERM_STEPS
    g = g_ref[...]
    for part in range(n_steps // PERM_STEPS):
        x = x_ref[:, part * PERM_STEPS:(part + 1) * PERM_STEPS, :].reshape(rows, x_ref.shape[-1])
        y = x * lax.rsqrt(jnp.mean(x * x, axis=-1, keepdims=True) + EPS)
        y = (y * g).astype(BF16)
        o_ref[part * rows:(part + 1) * rows, :] = jnp.dot(perm, y, preferred_element_type=F32).astype(o_ref.dtype)


def _rmsnorm_to_time_major(x3, gain, n_steps=64):
    b, t, d = x3.shape
    return pl.pallas_call(
        functools.partial(_rmsnorm_tb_body, n_batch=b, n_steps=n_steps),
        grid=(t // n_steps,),
        in_specs=[pl.BlockSpec((b, n_steps, d), lambda i: (0, i, 0)), pl.BlockSpec((1, d), lambda i: (0, 0))],
        out_specs=pl.BlockSpec((n_steps * b, d), lambda i: (i, 0)),
        out_shape=jax.ShapeDtypeStruct((t * b, d), BF16),
        compiler_params=_params(("arbitrary",)),
        name="rmsnorm_tb",
    )(x3, gain.reshape(1, d).astype(F32))


def _matmul_body(*refs, nk, act, has_scale, has_resid, has_copy, has_ssq, side_gains, norm_dim):
    refs = list(refs)
    x_ref, w_ref = refs[:2]
    pos = 2
    s_ref = r_ref = b_ref = q_ref = None
    if has_scale:
        s_ref = refs[pos]
        pos += 1
    if has_resid:
        r_ref = refs[pos]
        pos += 1
    side_in = []
    for has_gain in side_gains:
        side_in.append((refs[pos], refs[pos + 1] if has_gain else None))
        pos += 2 if has_gain else 1
    o_ref = refs[pos]
    pos += 1
    if has_copy:
        b_ref = refs[pos]
        pos += 1
    if has_ssq:
        q_ref = refs[pos]
        pos += 1
    side_out = refs[pos:pos + len(side_gains)]
    pos += len(side_gains)
    acc_ref = refs[pos] if nk > 1 else None

    def partial_product():
        for (src_ref, g_ref), dst_ref in zip(side_in, side_out):
            chunk = src_ref[...]
            if g_ref is not None:
                chunk = chunk * g_ref[...]
            dst_ref[...] = chunk.astype(dst_ref.dtype)
        return jnp.dot(x_ref[...], w_ref[...], preferred_element_type=F32)

    def finish(acc):
        if s_ref is not None:
            acc = acc * lax.rsqrt(jnp.sum(s_ref[...], axis=-1, keepdims=True) * (1.0 / norm_dim) + EPS)
        if act == "relu2":
            r = jnp.maximum(acc, 0.0)
            acc = r * r
        if r_ref is not None:
            acc = r_ref[...] + acc
        o_ref[...] = acc.astype(o_ref.dtype)
        if b_ref is not None:
            b_ref[...] = acc.astype(b_ref.dtype)
        if q_ref is not None:
            sq = acc * acc
            tot = sq[:, 0:LANES]
            for c in range(1, sq.shape[1] // LANES):
                tot = tot + sq[:, c * LANES:(c + 1) * LANES]
            q_ref[...] = tot

    if nk == 1:
        finish(partial_product())
    else:
        k = pl.program_id(2)

        @pl.when(k == 0)
        def _():
            acc_ref[...] = partial_product()

        @pl.when(jnp.logical_and(k > 0, k < nk - 1))
        def _():
            acc_ref[...] += partial_product()

        @pl.when(k == nk - 1)
        def _():
            finish(acc_ref[...] + partial_product())


def _matmul(x, w, *, out_dtype, layer=0, act=None, ssq_in=None, resid=None, bf16_copy=False, ssq_out=False,
            side_casts=(), cols_outer=False, bm=1024, bn=1024, bk=None, name="matmul"):
    m = x.shape[0]
    _, kdim, n = w.shape
    bm, bn = min(bm, m), min(bn, n)
    bk = kdim if bk is None else min(bk, kdim)
    nk, nb = kdim // bk, n // bn
    mb = m // bm
    n_steps = mb * nb * nk

    def spec(shape, index_map):
        if cols_outer:
            return pl.BlockSpec(shape, lambda j, i, k: index_map(i, j, k))
        return pl.BlockSpec(shape, index_map)

    in_specs = [spec((bm, bk), lambda i, j, k: (i, k)),
                spec((None, bk, bn), lambda i, j, k: (layer, k, j))]
    args = [x, w]
    if ssq_in is not None:
        in_specs.append(spec((bm, ssq_in.shape[1]), lambda i, j, k: (i, 0)))
        args.append(ssq_in)
    tile_spec = spec((bm, bn), lambda i, j, k: (i, j))
    if resid is not None:
        in_specs.append(tile_spec)
        args.append(resid)
    step = lambda i, j, k: (i * nb + j) * nk + k
    for src, gain in side_casts:
        rows, cols = src.shape
        chunk = rows // n_steps
        assert chunk * n_steps == rows and chunk % (2 * SUBLANES) == 0, (name, rows, n_steps)
        in_specs.append(spec((chunk, cols), lambda i, j, k: (step(i, j, k), 0)))
        args.append(src)
        if gain is not None:
            in_specs.append(spec((chunk, 1), lambda i, j, k: (step(i, j, k), 0)))
            args.append(gain.reshape(rows, 1).astype(F32))

    out_specs = [tile_spec]
    out_shape = [jax.ShapeDtypeStruct((m, n), out_dtype)]
    if bf16_copy:
        out_specs.append(tile_spec)
        out_shape.append(jax.ShapeDtypeStruct((m, n), BF16))
    if ssq_out:
        out_specs.append(spec((bm, LANES), lambda i, j, k: (i, j)))
        out_shape.append(jax.ShapeDtypeStruct((m, nb * LANES), F32))
    for src, _ in side_casts:
        rows, cols = src.shape
        out_specs.append(spec((rows // n_steps, cols), lambda i, j, k: (step(i, j, k), 0)))
        out_shape.append(jax.ShapeDtypeStruct((rows, cols), BF16))

    outs = pl.pallas_call(
        functools.partial(_matmul_body, nk=nk, act=act, has_scale=ssq_in is not None,
                          has_resid=resid is not None, has_copy=bf16_copy, has_ssq=ssq_out,
                          side_gains=tuple(g is not None for _, g in side_casts), norm_dim=kdim),
        grid=(nb, mb, nk) if cols_outer else (mb, nb, nk),
        in_specs=in_specs,
        out_specs=out_specs,
        out_shape=out_shape,
        scratch_shapes=[pltpu.VMEM((bm, bn), F32)] if nk > 1 else [],
        compiler_params=_params(("arbitrary", "arbitrary", "arbitrary")),
        name=name,
    )(*args)
    return outs[0] if len(outs) == 1 else outs


def _qkv_body(x_ref, wq_ref, wkv_ref, s_ref, q_ref, kv_ref):
    x = x_ref[...]
    scale = lax.rsqrt(jnp.sum(s_ref[...], axis=-1, keepdims=True) * (1.0 / x.shape[1]) + EPS)
    q = jnp.dot(x, wq_ref[...], preferred_element_type=F32)
    q_ref[...] = (q * (scale * (HEAD_DIM ** -0.5 * LOG2E))).astype(q_ref.dtype)
    kv_ref[...] = jnp.dot(x, wkv_ref[...], preferred_element_type=F32) * scale


def _qkv_proj(x, ssq, w_q, w_kv, bm=512):
    m, kdim = x.shape
    bm = min(bm, m)
    return pl.pallas_call(
        _qkv_body,
        grid=(m // bm,),
        in_specs=[pl.BlockSpec((bm, kdim), lambda i: (i, 0)),
                  pl.BlockSpec((kdim, Q_DIM), lambda i: (0, 0)),
                  pl.BlockSpec((kdim, 2 * KV_DIM), lambda i: (0, 0)),
                  pl.BlockSpec((bm, ssq.shape[1]), lambda i: (i, 0))],
        out_specs=[pl.BlockSpec((bm, Q_DIM), lambda i: (i, 0)),
                   pl.BlockSpec((bm, 2 * KV_DIM), lambda i: (i, 0))],
        out_shape=[jax.ShapeDtypeStruct((m, Q_DIM), BF16), jax.ShapeDtypeStruct((m, 2 * KV_DIM), F32)],
        compiler_params=_params(("arbitrary",)),
        name="qkv_proj",
    )(x, w_q, w_kv, ssq)


def _gelu_tanh(x):
    return x * (0.5 + 0.5 * jnp.tanh(0.7978845608028654 * (x + 0.044715 * (x * x * x))))


def _softplus(z):
    return jnp.maximum(z, 0.0) + jnp.log1p(jnp.exp(-jnp.abs(z)))


def _lru_body(gate_ref, xr_ref, cs_ref, h0_ref, cw_ref, cb_ref, wg_ref, bg_ref, lam_ref,
              z_ref, ht_ref, h_sc, tail_sc, *, n_steps, n_batch, tc_cols, batch_major_out):
    t = pl.program_id(1)
    n_tail = (CONV_W - 1) * n_batch
    n_rows = n_steps * n_batch
    perm = _perm_matrix(n_batch, False) if batch_major_out else None

    @pl.when(t == 0)
    def _():
        h_sc[...] = h0_ref[...]
        tail_sc[...] = cs_ref[...]

    cw = cw_ref[...]
    decay = (-0.5 * LRU_C * LOG2E) * _softplus(-lam_ref[...])
    for n in range(tc_cols // LRU_BLOCK):
        cols = slice(n * LRU_BLOCK, (n + 1) * LRU_BLOCK)
        x = xr_ref[:, cols]
        x_ext = jnp.concatenate([tail_sc[:, cols], x], axis=0)
        tail_sc[:, cols] = x_ext[n_rows:n_rows + n_tail]
        xc = cb_ref[:, cols] + cw[0:1, cols] * x_ext[0:n_rows]
        for j in range(1, CONV_W):
            xc = xc + cw[j:j + 1, cols] * x_ext[j * n_batch:j * n_batch + n_rows]
        half_pre = 0.5 * (jnp.dot(xc.astype(BF16), wg_ref[n], preferred_element_type=F32) + bg_ref[n])
        tr = jnp.tanh(half_pre[:, :LRU_BLOCK])
        ti = jnp.tanh(half_pre[:, LRU_BLOCK:])
        d = decay[:, cols]
        a = jnp.exp2(d * tr + d)
        gain2 = 1.0 - a * a
        u = (gain2 * lax.rsqrt(jnp.maximum(gain2, 1e-30))) * ((0.5 + 0.5 * ti) * xc)
        h = h_sc[:, cols]
        hs = []
        for s in range(n_steps):
            rows = slice(s * n_batch, (s + 1) * n_batch)
            h = a[rows] * h + u[rows]
            hs.append(h)
        h_sc[:, cols] = h
        hs = jnp.concatenate(hs, axis=0)
        z = (_gelu_tanh(gate_ref[:, cols]) * hs).astype(z_ref.dtype)
        if batch_major_out:
            p_rows = n_batch * PERM_STEPS
            for part in range(n_steps // PERM_STEPS):
                zp = jnp.dot(perm, z[part * p_rows:(part + 1) * p_rows], preferred_element_type=F32)
                z_ref[:, part * PERM_STEPS:(part + 1) * PERM_STEPS, cols] = (
                    zp.astype(z_ref.dtype).reshape(n_batch, PERM_STEPS, LRU_BLOCK))
        else:
            z_ref[:, cols] = z

    @pl.when(t == pl.num_programs(1) - 1)
    def _():
        ht_ref[...] = h_sc[...]


def _lru(gx, conv_state, h0, conv_w, conv_b, w_gates, b_gates, lam, *, n_batch, n_steps, batch_major_out,
         tc_cols=512):
    m = gx.shape[0]
    c = D_RNN
    n_rows = n_steps * n_batch
    n_tail = (CONV_W - 1) * n_batch
    ncb = c // tc_cols
    gpb = tc_cols // LRU_BLOCK
    if batch_major_out:
        z_spec = pl.BlockSpec((n_batch, n_steps, tc_cols), lambda ci, ti: (0, ti, ci))
        z_shape = jax.ShapeDtypeStruct((n_batch, m // n_batch, c), BF16)
    else:
        z_spec = pl.BlockSpec((n_rows, tc_cols), lambda ci, ti: (ti, ci))
        z_shape = jax.ShapeDtypeStruct((m, c), BF16)
    z, ht = pl.pallas_call(
        functools.partial(_lru_body, n_steps=n_steps, n_batch=n_batch, tc_cols=tc_cols,
                          batch_major_out=batch_major_out),
        grid=(ncb, m // n_rows),
        in_specs=[
            pl.BlockSpec((n_rows, tc_cols), lambda ci, ti: (ti, ci)),
            pl.BlockSpec((n_rows, tc_cols), lambda ci, ti: (ti, ncb + ci)),
            pl.BlockSpec((n_tail, tc_cols), lambda ci, ti: (0, ci)),
            pl.BlockSpec((n_batch, tc_cols), lambda ci, ti: (0, ci)),
            pl.BlockSpec((CONV_W, tc_cols), lambda ci, ti: (0, ci)),
            pl.BlockSpec((1, tc_cols), lambda ci, ti: (0, ci)),
            pl.BlockSpec((gpb, LRU_BLOCK, 2 * LRU_BLOCK), lambda ci, ti: (ci, 0, 0)),
            pl.BlockSpec((gpb, 1, 2 * LRU_BLOCK), lambda ci, ti: (ci, 0, 0)),
            pl.BlockSpec((1, tc_cols), lambda ci, ti: (0, ci)),
        ],
        out_specs=[z_spec, pl.BlockSpec((n_batch, tc_cols), lambda ci, ti: (0, ci))],
        out_shape=[z_shape, jax.ShapeDtypeStruct((n_batch, c), F32)],
        scratch_shapes=[pltpu.VMEM((n_batch, tc_cols), F32), pltpu.VMEM((n_tail, tc_cols), F32)],
        compiler_params=_params(("arbitrary", "arbitrary")),
        name="lru",
    )(gx, gx, conv_state, h0, conv_w, conv_b, w_gates, b_gates, lam)
    return z, ht


def _attn_prompt_body(sink_ref, q_ref, kvc_ref, kvp_ref, o_ref):
    i = pl.program_id(1)
    kv = jnp.concatenate([kvp_ref[...], kvc_ref[...]], axis=0).astype(BF16)
    n_keys = 2 * WINDOW
    low_k = lax.broadcasted_iota(jnp.int32, (n_keys, LANES), 1) < HEAD_DIM
    low_q = lax.broadcasted_iota(jnp.int32, (WINDOW, LANES), 1) < HEAD_DIM
    zero = jnp.zeros((n_keys, LANES), BF16)
    one = jnp.ones((n_keys, LANES), BF16)
    from_prev = (lax.broadcasted_iota(jnp.int32, (WINDOW, WINDOW), 1)
                 > lax.broadcasted_iota(jnp.int32, (WINDOW, WINDOW), 0))
    prev_bias = jnp.where(i == 0, -jnp.inf, 0.0)
    dims = (((1,), (1,)), ((), ()))

    for h in range(N_KV_HEADS):
        c0 = (h // 2) * LANES
        k_t = kv[:, c0:c0 + LANES]
        v_t = kv[:, KV_DIM + c0:KV_DIM + c0 + LANES]
        k_sw = pltpu.roll(k_t, HEAD_DIM, 1)
        v_sw = pltpu.roll(v_t, HEAD_DIM, 1)
        k_lo, k_hi = (k_t, k_sw) if h % 2 == 0 else (k_sw, k_t)
        v_lo, v_hi = (v_t, v_sw) if h % 2 == 0 else (v_sw, v_t)
        k_a = jnp.where(low_k, k_lo, zero)
        k_b = jnp.where(low_k, zero, k_hi)
        v_a = jnp.where(low_k, v_lo, one)
        v_b = jnp.where(low_k, one, v_hi)
        n_pairs = GROUP // 2
        qc0 = h * GROUP * HEAD_DIM
        q_h = jnp.concatenate([q_ref[:, qc0 + p * LANES:qc0 + (p + 1) * LANES] for p in range(n_pairs)], axis=0)
        acc, esink = [], []
        for side, (k_x, v_x) in enumerate(((k_a, v_a), (k_b, v_b))):
            s = lax.dot_general(q_h, k_x, dims, preferred_element_type=F32)
            probs, es = [], []
            for p in range(n_pairs):
                s_p = s[p * WINDOW:(p + 1) * WINDOW]
                s_v = jnp.where(from_prev, s_p[:, :WINDOW] + prev_bias, s_p[:, WINDOW:])
                sink = sink_ref[h * GROUP + 2 * p + side] * LOG2E
                m = jnp.maximum(jnp.max(s_v, axis=-1, keepdims=True), sink)
                e = jnp.exp2(s_v - m)
                probs.append(jnp.concatenate([jnp.where(from_prev, e, 0.0), jnp.where(from_prev, 0.0, e)],
                                             axis=1).astype(BF16))
                es.append(jnp.exp2(sink - m))
            acc.append(jnp.dot(jnp.concatenate(probs, axis=0), v_x, preferred_element_type=F32))
            esink.append(es)
        for p in range(n_pairs):
            rows = slice(p * WINDOW, (p + 1) * WINDOW)
            acc_a, acc_b = acc[0][rows], acc[1][rows]
            num = jnp.where(low_q, acc_a, acc_b)
            den = pltpu.roll(jnp.where(low_q, acc_b, acc_a), HEAD_DIM, 1) + jnp.where(low_q, esink[0][p], esink[1][p])
            o_ref[:, qc0 + p * LANES:qc0 + (p + 1) * LANES] = (num / den).astype(o_ref.dtype)


def _attn_prompt(q3, kv3, sinks):
    b, t, _ = q3.shape
    nb = t // WINDOW
    return pl.pallas_call(
        _attn_prompt_body,
        grid=(b, nb),
        in_specs=[
            pl.BlockSpec(memory_space=pltpu.SMEM),
            pl.BlockSpec((None, WINDOW, Q_DIM), lambda bi, i: (bi, i, 0)),
            pl.BlockSpec((None, WINDOW, 2 * KV_DIM), lambda bi, i: (bi, i, 0)),
            pl.BlockSpec((None, WINDOW, 2 * KV_DIM), lambda bi, i: (bi, jnp.maximum(i - 1, 0), 0)),
        ],
        out_specs=pl.BlockSpec((None, WINDOW, Q_DIM), lambda bi, i: (bi, i, 0)),
        out_shape=jax.ShapeDtypeStruct((b, t, Q_DIM), BF16),
        compiler_params=_params(("arbitrary", "arbitrary")),
        name="attn_prompt",
    )(sinks, q3, kv3, kv3)


SAMPLE_BATCH_TILE = 8


def _attn_sample_body(sink_ref, q_ref, kvn_ref, ck_ref, cv_ref, o_ref, *, n_steps):
    bb = SAMPLE_BATCH_TILE
    n_new = n_steps * bb
    n_cache = bb * WINDOW
    n_keys = n_cache + n_new
    n_keys_pad = -(-n_keys // LANES) * LANES
    q = q_ref[...].reshape(n_new, Q_DIM)
    kvn = kvn_ref[...].reshape(n_new, 2 * KV_DIM)
    pad = jnp.zeros((n_keys_pad - n_keys, KV_DIM), F32)
    k_all = jnp.concatenate([ck_ref[...].reshape(n_cache, KV_DIM), kvn[:, :KV_DIM], pad], axis=0).astype(BF16)
    v_all = jnp.concatenate([cv_ref[...].reshape(n_cache, KV_DIM), kvn[:, KV_DIM:], pad], axis=0).astype(BF16)
    n_rows = GROUP * n_new
    r = lax.broadcasted_iota(jnp.int32, (n_rows, n_keys_pad), 0)
    c = lax.broadcasted_iota(jnp.int32, (n_rows, n_keys_pad), 1)
    b_r = r % bb
    t_r = (r // bb) % n_steps
    is_cache = c < n_cache
    c2 = jnp.maximum(c - n_cache, 0)
    b_c = jnp.where(is_cache, c // WINDOW, c2 % bb)
    k_step = jnp.where(is_cache, c % WINDOW - WINDOW, c2 // bb)
    dq = t_r - k_step
    allowed = jnp.logical_and(jnp.logical_and(b_c == b_r, c < n_keys),
                              jnp.logical_and(dq >= 0, dq < WINDOW))
    grp = lax.broadcasted_iota(jnp.int32, (n_rows, 1), 0) // n_new
    dims = (((1,), (1,)), ((), ()))
    for h in range(N_KV_HEADS):
        k_h = k_all[:, h * HEAD_DIM:(h + 1) * HEAD_DIM]
        v_h = v_all[:, h * HEAD_DIM:(h + 1) * HEAD_DIM]
        q_h = jnp.concatenate(
            [q[:, (h * GROUP + g) * HEAD_DIM:(h * GROUP + g + 1) * HEAD_DIM] for g in range(GROUP)], axis=0)
        sink = jnp.zeros((n_rows, 1), F32)
        for g in range(GROUP):
            sink = jnp.where(grp == g, sink_ref[h * GROUP + g] * LOG2E, sink)
        s = lax.dot_general(q_h, k_h, dims, preferred_element_type=F32)
        s = jnp.where(allowed, s, -jnp.inf)
        m = jnp.maximum(jnp.max(s, axis=-1, keepdims=True), sink)
        p = jnp.exp2(s - m)
        denom = jnp.sum(p, axis=-1, keepdims=True) + jnp.exp2(sink - m)
        o_h = jnp.dot(p.astype(BF16), v_h, preferred_element_type=F32) / denom
        for g in range(GROUP):
            c0 = (h * GROUP + g) * HEAD_DIM
            o_ref[:, :, c0:c0 + HEAD_DIM] = (
                o_h[g * n_new:(g + 1) * n_new].reshape(n_steps, bb, HEAD_DIM).astype(o_ref.dtype))


def _attn_sample(q3, kvn3, cache_k, cache_v, sinks):
    n_steps, b, _ = q3.shape
    bb = SAMPLE_BATCH_TILE
    return pl.pallas_call(
        functools.partial(_attn_sample_body, n_steps=n_steps),
        grid=(b // bb,),
        in_specs=[
            pl.BlockSpec(memory_space=pltpu.SMEM),
            pl.BlockSpec((n_steps, bb, Q_DIM), lambda i: (0, i, 0)),
            pl.BlockSpec((n_steps, bb, 2 * KV_DIM), lambda i: (0, i, 0)),
            pl.BlockSpec((bb, WINDOW, KV_DIM), lambda i: (i, 0, 0)),
            pl.BlockSpec((bb, WINDOW, KV_DIM), lambda i: (i, 0, 0)),
        ],
        out_specs=pl.BlockSpec((n_steps, bb, Q_DIM), lambda i: (0, i, 0)),
        out_shape=jax.ShapeDtypeStruct((n_steps, b, Q_DIM), BF16),
        compiler_params=_params(("arbitrary",)),
        name="attn_sample",
    )(sinks, q3, kvn3, cache_k, cache_v)


TILES = {
    "rec_in": dict(bm=512, bn=2048, cols_outer=True),
    "rec_out": dict(bm=512, bn=2048),
    "attn_out": dict(bm=512, bn=2048),
    "mlp_up": dict(bm=1024, bn=1024),
    "mlp_down": dict(bm=1024, bn=1024, bk=2048),
}


def _mm(name, x, w, **kw):
    return _matmul(x, w, name=name, **TILES[name], **kw)


def _flat(w):
    return w.reshape(-1, w.shape[-1])


def _trunk(xn, x, lru_fn, attn_fn, raw, w):
    if "rec_w_out" not in w:
        gx, w_out, w_up = _mm("rec_in", xn, w["rec_w_in"], out_dtype=F32, side_casts=[
            (_flat(raw["rec_w_out"]), None), (_flat(raw["mlp_w_up"]), raw["norm_mlp"].reshape(-1))])
        w["rec_w_out"] = w_out.reshape(raw["rec_w_out"].shape)
        w["mlp_w_up"] = w_up.reshape(raw["mlp_w_up"].shape)
    else:
        gx = _mm("rec_in", xn, w["rec_w_in"], out_dtype=F32)
    z, h_last = lru_fn(gx)
    h, hb, ssq = _mm("rec_out", z, w["rec_w_out"], out_dtype=F32, resid=x, bf16_copy=True, ssq_out=True)
    if "mlp_w_down" not in w:
        hid, w_down, w_kv, w_q, w_o = _mm("mlp_up", hb, w["mlp_w_up"], layer=0, out_dtype=BF16, act="relu2",
                                          ssq_in=ssq, side_casts=[
            (_flat(raw["mlp_w_down"]), None), (raw["w_kv"], raw["kv_norm"]),
            (_flat(raw["attn_w_q"]), raw["norm_mix"][1]), (_flat(raw["attn_w_o"]), None)])
        w["mlp_w_down"] = w_down.reshape(raw["mlp_w_down"].shape)
        w["w_kv"] = w_kv[None]
        w["attn_w_q"] = w_q.reshape(raw["attn_w_q"].shape)
        w["attn_w_o"] = w_o.reshape(raw["attn_w_o"].shape)
    else:
        hid = _mm("mlp_up", hb, w["mlp_w_up"], layer=0, out_dtype=BF16, act="relu2", ssq_in=ssq)
    h, hb, ssq = _mm("mlp_down", hid, w["mlp_w_down"], layer=0, out_dtype=F32, resid=h, bf16_copy=True, ssq_out=True)
    q, kv = _qkv_proj(hb, ssq, w["attn_w_q"][0], w["w_kv"][0])
    o = attn_fn(q, kv)
    h, hb, ssq = _mm("attn_out", o, w["attn_w_o"], out_dtype=F32, resid=h, bf16_copy=True, ssq_out=True)
    hid = _mm("mlp_up", hb, w["mlp_w_up"], layer=1, out_dtype=BF16, act="relu2", ssq_in=ssq)
    h = _mm("mlp_down", hid, w["mlp_w_down"], layer=1, out_dtype=F32, resid=h)
    out = _rmsnorm(h, raw["final_norm"], F32)
    return out, gx, h_last, kv


def kernel(x_prompt, x_sample, state_conv, state_h, cache_k, cache_v, norm_mix, norm_mlp, rec_w_in,
           rec_conv_w, rec_conv_b, rec_gate_a_w, rec_gate_a_b, rec_gate_x_w, rec_gate_x_b, rec_lambda,
           rec_w_out, kv_norm, w_kv, attn_w_q, attn_sinks, attn_w_o, mlp_w_up, mlp_w_down, final_norm):
    b, t, d = x_prompt.shape
    sb, st, _ = x_sample.shape
    c = D_RNN
    n_tail = CONV_W - 1

    raw = {
        "norm_mix": norm_mix, "norm_mlp": norm_mlp, "kv_norm": kv_norm, "final_norm": final_norm,
        "rec_w_out": rec_w_out, "w_kv": w_kv, "attn_w_q": attn_w_q, "attn_w_o": attn_w_o,
        "mlp_w_up": mlp_w_up, "mlp_w_down": mlp_w_down,
    }
    w = {"rec_w_in": rec_w_in.astype(BF16)}
    conv_w = rec_conv_w[0]
    conv_b = rec_conv_b[0].reshape(1, c)
    lam = rec_lambda[0].reshape(1, c)
    w_gates = jnp.concatenate([rec_gate_a_w[0], rec_gate_x_w[0]], axis=-1).astype(BF16)
    b_gates = jnp.concatenate([rec_gate_a_b[0], rec_gate_x_b[0]], axis=-1).reshape(N_LRU_BLOCKS, 1, 2 * LRU_BLOCK)
    sinks = attn_sinks[0].astype(F32)

    def lru_prompt(gx):
        z, ht = _lru(gx, jnp.zeros((n_tail * b, c), F32), jnp.zeros((b, c), F32),
                     conv_w, conv_b, w_gates, b_gates, lam, n_batch=b, n_steps=64, batch_major_out=True)
        return z.reshape(b * t, c), ht

    def attn_prompt(q, kv):
        o = _attn_prompt(q.reshape(b, t, Q_DIM), kv.reshape(b, t, 2 * KV_DIM), sinks)
        return o.reshape(b * t, Q_DIM)

    xn_p = _rmsnorm_to_time_major(x_prompt, norm_mix[0])
    y_p, gx_p, h_p, kv_p = _trunk(xn_p, x_prompt.reshape(b * t, d), lru_prompt, attn_prompt, raw, w)
    y_prompt = y_p.reshape(b, t, d)
    conv_p = jnp.swapaxes(gx_p.reshape(t, b, 2 * c)[t - n_tail:, :, c:], 0, 1).reshape(b, 1, n_tail, c)
    keep = min(WINDOW, t)
    kv_tail = kv_p.reshape(b, t, 2 * KV_DIM)[:, t - keep:]
    new_k_prompt = kv_tail[..., :KV_DIM].reshape(b, keep, N_KV_HEADS, HEAD_DIM)
    new_v_prompt = kv_tail[..., KV_DIM:].reshape(b, keep, N_KV_HEADS, HEAD_DIM)

    conv_state_tb = jnp.swapaxes(state_conv[:, 0], 0, 1).reshape(n_tail * sb, c)
    h0 = state_h[:, 0]
    ck = cache_k.reshape(sb, WINDOW, KV_DIM)
    cv = cache_v.reshape(sb, WINDOW, KV_DIM)

    def lru_sample(gx):
        return _lru(gx, conv_state_tb, h0, conv_w, conv_b, w_gates, b_gates, lam, n_batch=sb, n_steps=st,
                    batch_major_out=False)

    def attn_sample(q, kv):
        o = _attn_sample(q.reshape(st, sb, Q_DIM), kv.reshape(st, sb, 2 * KV_DIM), ck, cv, sinks)
        return o.reshape(st * sb, Q_DIM)

    x_s = jnp.swapaxes(x_sample, 0, 1).reshape(st * sb, d)
    y_s, gx_s, h_s, kv_s = _trunk(_rmsnorm(x_s, norm_mix[0], BF16), x_s, lru_sample, attn_sample, raw, w)
    y_sample = jnp.swapaxes(y_s.reshape(st, sb, d), 0, 1)
    conv_s = jnp.swapaxes(gx_s.reshape(st, sb, 2 * c)[st - n_tail:, :, c:], 0, 1).reshape(sb, 1, n_tail, c)
    kv_s = jnp.swapaxes(kv_s.reshape(st, sb, 2 * KV_DIM), 0, 1)
    new_k_sample = kv_s[..., :KV_DIM].reshape(sb, st, N_KV_HEADS, HEAD_DIM)
    new_v_sample = kv_s[..., KV_DIM:].reshape(sb, st, N_KV_HEADS, HEAD_DIM)

    return (y_prompt, y_sample, conv_p, h_p.reshape(b, 1, c), new_k_prompt, new_v_prompt,
            conv_s, h_s.reshape(sb, 1, c), new_k_sample, new_v_sample)
```

```python
import functools
import math

import jax
import jax.numpy as jnp
from jax import lax
from jax.experimental import pallas as pl
from jax.experimental.pallas import tpu as pltpu

F32 = jnp.float32
BF16 = jnp.bfloat16

D_MODEL = 2048
D_RNN = 2048
N_LRU_BLOCKS = 16
LRU_BLOCK = D_RNN // N_LRU_BLOCKS
CONV_W = 4
LRU_C = 8.0
N_HEADS = 32
N_KV_HEADS = 4
HEAD_DIM = 64
GROUP = N_HEADS // N_KV_HEADS
Q_DIM = N_HEADS * HEAD_DIM
KV_DIM = N_KV_HEADS * HEAD_DIM
WINDOW = 128
EPS = 1e-6
LOG2E = math.log2(math.e)

SUBLANES = 8
LANES = 128
VMEM_LIMIT_BYTES = 52 * 1024 * 1024


def _params(semantics):
    return pltpu.CompilerParams(dimension_semantics=semantics,
                                vmem_limit_bytes=VMEM_LIMIT_BYTES)


def _rmsnorm_body(x_ref, g_ref, o_ref):
    x = x_ref[...]
    y = x * lax.rsqrt(jnp.mean(x * x, axis=-1, keepdims=True) + EPS)
    o_ref[...] = (y * g_ref[...]).astype(o_ref.dtype)


def _rmsnorm(x, gain, out_dtype, tm=512):
    m, d = x.shape
    tm = min(tm, m)
    row_spec = pl.BlockSpec((tm, d), lambda i: (i, 0))
    return pl.pallas_call(
        _rmsnorm_body,
        grid=(m // tm,),
        in_specs=[row_spec, pl.BlockSpec((1, d), lambda i: (0, 0))],
        out_specs=row_spec,
        out_shape=jax.ShapeDtypeStruct((m, d), out_dtype),
        compiler_params=_params(("arbitrary",)),
        name="rmsnorm",
    )(x, gain.reshape(1, d).astype(F32))


PERM_STEPS = 32


def _perm_matrix(n_batch, to_time_major):
    n = n_batch * PERM_STEPS
    r = lax.broadcasted_iota(jnp.int32, (n, n), 0)
    c = lax.broadcasted_iota(jnp.int32, (n, n), 1)
    if to_time_major:
        src = (r % n_batch) * PERM_STEPS + r // n_batch
    else:
        src = (r % PERM_STEPS) * n_batch + r // PERM_STEPS
    return jnp.where(c == src, 1.0, 0.0).astype(BF16)


def _rmsnorm_tb_body(x_ref, g_ref, o_ref, *, n_batch, n_steps):
    perm = _perm_matrix(n_batch, True)
    rows = n_batch * PERM_STEPS
    g = g_ref[...]
    for part in range(n_steps // PERM_STEPS):
        x = x_ref[:, part * PERM_STEPS:(part + 1) * PERM_STEPS, :].reshape(rows, x_ref.shape[-1])
        y = x * lax.rsqrt(jnp.mean(x * x, axis=-1, keepdims=True) + EPS)
        y = (y * g).astype(BF16)
        o_ref[part * rows:(part + 1) * rows, :] = jnp.dot(perm, y, preferred_element_type=F32).astype(o_ref.dtype)


def _rmsnorm_to_time_major(x3, gain, n_steps=64):
    b, t, d = x3.shape
    return pl.pallas_call(
        functools.partial(_rmsnorm_tb_body, n_batch=b, n_steps=n_steps),
        grid=(t // n_steps,),
        in_specs=[pl.BlockSpec((b, n_steps, d), lambda i: (0, i, 0)), pl.BlockSpec((1, d), lambda i: (0, 0))],
        out_specs=pl.BlockSpec((n_steps * b, d), lambda i: (i, 0)),
        out_shape=jax.ShapeDtypeStruct((t * b, d), BF16),
        compiler_params=_params(("arbitrary",)),
        name="rmsnorm_tb",
    )(x3, gain.reshape(1, d).astype(F32))


def _matmul_body(*refs, nk, act, has_scale, has_resid, has_norm, has_copy, has_ssq, side_gains, norm_dim,
                 col_axis):
    refs = list(refs)
    x_ref, w_ref = refs[:2]
    pos = 2
    s_ref = r_ref = b_ref = q_ref = None
    if has_scale:
        s_ref = refs[pos]
        pos += 1
    if has_resid:
        r_ref = refs[pos]
        pos += 1
    n_ref = None
    if has_norm:
        n_ref = refs[pos]
        pos += 1
    side_in = []
    for has_gain in side_gains:
        side_in.append((refs[pos], refs[pos + 1] if has_gain else None))
        pos += 2 if has_gain else 1
    o_ref = refs[pos]
    pos += 1
    if has_copy:
        b_ref = refs[pos]
        pos += 1
    if has_ssq:
        q_ref = refs[pos]
        pos += 1
    side_out = refs[pos:pos + len(side_gains)]
    pos += len(side_gains)
    acc_ref = refs[pos] if nk > 1 else None

    def partial_product():
        for (src_ref, g_ref), dst_ref in zip(side_in, side_out):
            chunk = src_ref[...]
            if g_ref is not None:
                chunk = chunk * g_ref[...]
            dst_ref[...] = chunk.astype(dst_ref.dtype)
        return jnp.dot(x_ref[...], w_ref[...], preferred_element_type=F32)

    def finish(acc, gelu=False):
        if s_ref is not None:
            acc = acc * lax.rsqrt(jnp.sum(s_ref[...], axis=-1, keepdims=True) * (1.0 / norm_dim) + EPS)
        if act == "relu2":
            r = jnp.maximum(acc, 0.0)
            acc = r * r
        if gelu:
            acc = _gelu_tanh(acc)
        if r_ref is not None:
            acc = r_ref[...] + acc
        if n_ref is not None:
            acc = acc * lax.rsqrt(jnp.mean(acc * acc, axis=-1, keepdims=True) + EPS) * n_ref[...]
        o_ref[...] = acc.astype(o_ref.dtype)
        if b_ref is not None:
            b_ref[...] = acc.astype(b_ref.dtype)
        if q_ref is not None:
            sq = acc * acc
            tot = sq[:, 0:LANES]
            for c in range(1, sq.shape[1] // LANES):
                tot = tot + sq[:, c * LANES:(c + 1) * LANES]
            q_ref[...] = tot

    if act == "gelu_first_block":
        assert nk == 1
        first = pl.program_id(col_axis) == 0

        @pl.when(first)
        def _():
            finish(partial_product(), gelu=True)

        @pl.when(jnp.logical_not(first))
        def _():
            finish(partial_product())
    elif nk == 1:
        finish(partial_product())
    else:
        k = pl.program_id(2)

        @pl.when(k == 0)
        def _():
            acc_ref[...] = partial_product()

        @pl.when(jnp.logical_and(k > 0, k < nk - 1))
        def _():
            acc_ref[...] += partial_product()

        @pl.when(k == nk - 1)
        def _():
            finish(acc_ref[...] + partial_product())


def _matmul(x, w, *, out_dtype, layer=0, act=None, ssq_in=None, resid=None, norm_gain=None, bf16_copy=False,
            ssq_out=False,
            side_casts=(), cols_outer=False, bm=1024, bn=1024, bk=None, name="matmul"):
    m = x.shape[0]
    _, kdim, n = w.shape
    bm, bn = min(bm, m), min(bn, n)
    bk = kdim if bk is None else min(bk, kdim)
    nk, nb = kdim // bk, n // bn
    mb = m // bm
    n_steps = mb * nb * nk

    def spec(shape, index_map):
        if cols_outer:
            return pl.BlockSpec(shape, lambda j, i, k: index_map(i, j, k))
        return pl.BlockSpec(shape, index_map)

    in_specs = [spec((bm, bk), lambda i, j, k: (i, k)),
                spec((None, bk, bn), lambda i, j, k: (layer, k, j))]
    args = [x, w]
    if ssq_in is not None:
        in_specs.append(spec((bm, ssq_in.shape[1]), lambda i, j, k: (i, 0)))
        args.append(ssq_in)
    tile_spec = spec((bm, bn), lambda i, j, k: (i, j))
    if resid is not None:
        in_specs.append(tile_spec)
        args.append(resid)
    if norm_gain is not None:
        assert nb == 1, "the fused output RMSNorm needs whole rows"
        in_specs.append(spec((1, n), lambda i, j, k: (0, 0)))
        args.append(norm_gain.reshape(1, n).astype(F32))
    step = lambda i, j, k: (i * nb + j) * nk + k
    for src, gain in side_casts:
        rows, cols = src.shape
        chunk = rows // n_steps
        assert chunk * n_steps == rows and chunk % (2 * SUBLANES) == 0, (name, rows, n_steps)
        in_specs.append(spec((chunk, cols), lambda i, j, k: (step(i, j, k), 0)))
        args.append(src)
        if gain is not None:
            in_specs.append(spec((chunk, 1), lambda i, j, k: (step(i, j, k), 0)))
            args.append(gain.reshape(rows, 1).astype(F32))

    out_specs = [tile_spec]
    out_shape = [jax.ShapeDtypeStruct((m, n), out_dtype)]
    if bf16_copy:
        out_specs.append(tile_spec)
        out_shape.append(jax.ShapeDtypeStruct((m, n), BF16))
    if ssq_out:
        out_specs.append(spec((bm, LANES), lambda i, j, k: (i, j)))
        out_shape.append(jax.ShapeDtypeStruct((m, nb * LANES), F32))
    for src, _ in side_casts:
        rows, cols = src.shape
        out_specs.append(spec((rows // n_steps, cols), lambda i, j, k: (step(i, j, k), 0)))
        out_shape.append(jax.ShapeDtypeStruct((rows, cols), BF16))

    outs = pl.pallas_call(
        functools.partial(_matmul_body, nk=nk, act=act, has_scale=ssq_in is not None,
                          has_resid=resid is not None, has_norm=norm_gain is not None, has_copy=bf16_copy,
                          has_ssq=ssq_out,
                          side_gains=tuple(g is not None for _, g in side_casts), norm_dim=kdim,
                          col_axis=0 if cols_outer else 1),
        grid=(nb, mb, nk) if cols_outer else (mb, nb, nk),
        in_specs=in_specs,
        out_specs=out_specs,
        out_shape=out_shape,
        scratch_shapes=[pltpu.VMEM((bm, bn), F32)] if nk > 1 else [],
        compiler_params=_params(("arbitrary", "arbitrary", "arbitrary")),
        name=name,
    )(*args)
    return outs[0] if len(outs) == 1 else outs


def _qkv_body(x_ref, wq_ref, wkv_ref, s_ref, q_ref, kv_ref):
    x = x_ref[...]
    scale = lax.rsqrt(jnp.sum(s_ref[...], axis=-1, keepdims=True) * (1.0 / x.shape[1]) + EPS)
    q = jnp.dot(x, wq_ref[...], preferred_element_type=F32)
    q_ref[...] = (q * (scale * (HEAD_DIM ** -0.5 * LOG2E))).astype(q_ref.dtype)
    kv_ref[...] = jnp.dot(x, wkv_ref[...], preferred_element_type=F32) * scale


def _qkv_proj(x, ssq, w_q, w_kv, bm=512):
    m, kdim = x.shape
    bm = min(bm, m)
    return pl.pallas_call(
        _qkv_body,
        grid=(m // bm,),
        in_specs=[pl.BlockSpec((bm, kdim), lambda i: (i, 0)),
                  pl.BlockSpec((kdim, Q_DIM), lambda i: (0, 0)),
                  pl.BlockSpec((kdim, 2 * KV_DIM), lambda i: (0, 0)),
                  pl.BlockSpec((bm, ssq.shape[1]), lambda i: (i, 0))],
        out_specs=[pl.BlockSpec((bm, Q_DIM), lambda i: (i, 0)),
                   pl.BlockSpec((bm, 2 * KV_DIM), lambda i: (i, 0))],
        out_shape=[jax.ShapeDtypeStruct((m, Q_DIM), BF16), jax.ShapeDtypeStruct((m, 2 * KV_DIM), F32)],
        compiler_params=_params(("arbitrary",)),
        name="qkv_proj",
    )(x, w_q, w_kv, ssq)


def _gelu_tanh(x):
    return x * (0.5 + 0.5 * jnp.tanh(0.7978845608028654 * (x + 0.044715 * (x * x * x))))


def _softplus(z):
    return jnp.maximum(z, 0.0) + jnp.log1p(jnp.exp(-jnp.abs(z)))


def _lru_body(gate_ref, xr_ref, cs_ref, h0_ref, cw_ref, cb_ref, wg_ref, bg_ref, lam_ref,
              z_ref, ht_ref, h_sc, tail_sc, *, n_steps, n_batch, tc_cols, batch_major_out):
    t = pl.program_id(1)
    n_tail = (CONV_W - 1) * n_batch
    n_rows = n_steps * n_batch
    perm = _perm_matrix(n_batch, False) if batch_major_out else None

    @pl.when(t == 0)
    def _():
        h_sc[...] = h0_ref[...]
        tail_sc[...] = cs_ref[...]

    cw = cw_ref[...]
    decay = (-0.5 * LRU_C * LOG2E) * _softplus(-lam_ref[...])
    for n in range(tc_cols // LRU_BLOCK):
        cols = slice(n * LRU_BLOCK, (n + 1) * LRU_BLOCK)
        x = xr_ref[:, cols]
        x_ext = jnp.concatenate([tail_sc[:, cols], x], axis=0)
        tail_sc[:, cols] = x_ext[n_rows:n_rows + n_tail]
        xc = cb_ref[:, cols] + cw[0:1, cols] * x_ext[0:n_rows]
        for j in range(1, CONV_W):
            xc = xc + cw[j:j + 1, cols] * x_ext[j * n_batch:j * n_batch + n_rows]
        half_pre = 0.5 * (jnp.dot(xc.astype(BF16), wg_ref[n], preferred_element_type=F32) + bg_ref[n])
        tr = jnp.tanh(half_pre[:, :LRU_BLOCK])
        ti = jnp.tanh(half_pre[:, LRU_BLOCK:])
        d = decay[:, cols]
        a = jnp.exp2(d * tr + d)
        gain2 = 1.0 - a * a
        u = (gain2 * lax.rsqrt(jnp.maximum(gain2, 1e-30))) * ((0.5 + 0.5 * ti) * xc)
        h = h_sc[:, cols]
        hs = []
        for s in range(n_steps):
            rows = slice(s * n_batch, (s + 1) * n_batch)
            h = a[rows] * h + u[rows]
            hs.append(h)
        h_sc[:, cols] = h
        hs = jnp.concatenate(hs, axis=0)
        z = (gate_ref[:, cols] * hs).astype(z_ref.dtype)
        if batch_major_out:
            p_rows = n_batch * PERM_STEPS
            for part in range(n_steps // PERM_STEPS):
                zp = jnp.dot(perm, z[part * p_rows:(part + 1) * p_rows], preferred_element_type=F32)
                z_ref[:, part * PERM_STEPS:(part + 1) * PERM_STEPS, cols] = (
                    zp.astype(z_ref.dtype).reshape(n_batch, PERM_STEPS, LRU_BLOCK))
        else:
            z_ref[:, cols] = z

    @pl.when(t == pl.num_programs(1) - 1)
    def _():
        ht_ref[...] = h_sc[...]


def _lru(gx, conv_state, h0, conv_w, conv_b, w_gates, b_gates, lam, *, n_batch, n_steps, batch_major_out,
         tc_cols=512):
    m = gx.shape[0]
    c = D_RNN
    n_rows = n_steps * n_batch
    n_tail = (CONV_W - 1) * n_batch
    ncb = c // tc_cols
    gpb = tc_cols // LRU_BLOCK
    if batch_major_out:
        z_spec = pl.BlockSpec((n_batch, n_steps, tc_cols), lambda ci, ti: (0, ti, ci))
        z_shape = jax.ShapeDtypeStruct((n_batch, m // n_batch, c), BF16)
    else:
        z_spec = pl.BlockSpec((n_rows, tc_cols), lambda ci, ti: (ti, ci))
        z_shape = jax.ShapeDtypeStruct((m, c), BF16)
    z, ht = pl.pallas_call(
        functools.partial(_lru_body, n_steps=n_steps, n_batch=n_batch, tc_cols=tc_cols,
                          batch_major_out=batch_major_out),
        grid=(ncb, m // n_rows),
        in_specs=[
            pl.BlockSpec((n_rows, tc_cols), lambda ci, ti: (ti, ci)),
            pl.BlockSpec((n_rows, tc_cols), lambda ci, ti: (ti, ncb + ci)),
            pl.BlockSpec((n_tail, tc_cols), lambda ci, ti: (0, ci)),
            pl.BlockSpec((n_batch, tc_cols), lambda ci, ti: (0, ci)),
            pl.BlockSpec((CONV_W, tc_cols), lambda ci, ti: (0, ci)),
            pl.BlockSpec((1, tc_cols), lambda ci, ti: (0, ci)),
            pl.BlockSpec((gpb, LRU_BLOCK, 2 * LRU_BLOCK), lambda ci, ti: (ci, 0, 0)),
            pl.BlockSpec((gpb, 1, 2 * LRU_BLOCK), lambda ci, ti: (ci, 0, 0)),
            pl.BlockSpec((1, tc_cols), lambda ci, ti: (0, ci)),
        ],
        out_specs=[z_spec, pl.BlockSpec((n_batch, tc_cols), lambda ci, ti: (0, ci))],
        out_shape=[z_shape, jax.ShapeDtypeStruct((n_batch, c), F32)],
        scratch_shapes=[pltpu.VMEM((n_batch, tc_cols), F32), pltpu.VMEM((n_tail, tc_cols), F32)],
        compiler_params=_params(("arbitrary", "arbitrary")),
        name="lru",
    )(gx, gx, conv_state, h0, conv_w, conv_b, w_gates, b_gates, lam)
    return z, ht


def _attn_prompt_body(sink_ref, q_ref, kvc_ref, kvp_ref, o_ref):
    i = pl.program_id(1)
    kv = jnp.concatenate([kvp_ref[...], kvc_ref[...]], axis=0).astype(BF16)
    n_keys = 2 * WINDOW
    low_k = lax.broadcasted_iota(jnp.int32, (n_keys, LANES), 1) < HEAD_DIM
    low_q = lax.broadcasted_iota(jnp.int32, (WINDOW, LANES), 1) < HEAD_DIM
    zero = jnp.zeros((n_keys, LANES), BF16)
    one = jnp.ones((n_keys, LANES), BF16)
    from_prev = (lax.broadcasted_iota(jnp.int32, (WINDOW, WINDOW), 1)
                 > lax.broadcasted_iota(jnp.int32, (WINDOW, WINDOW), 0))
    prev_bias = jnp.where(i == 0, -jnp.inf, 0.0)
    dims = (((1,), (1,)), ((), ()))

    for h in range(N_KV_HEADS):
        c0 = (h // 2) * LANES
        k_t = kv[:, c0:c0 + LANES]
        v_t = kv[:, KV_DIM + c0:KV_DIM + c0 + LANES]
        k_sw = pltpu.roll(k_t, HEAD_DIM, 1)
        v_sw = pltpu.roll(v_t, HEAD_DIM, 1)
        k_lo, k_hi = (k_t, k_sw) if h % 2 == 0 else (k_sw, k_t)
        v_lo, v_hi = (v_t, v_sw) if h % 2 == 0 else (v_sw, v_t)
        k_a = jnp.where(low_k, k_lo, zero)
        k_b = jnp.where(low_k, zero, k_hi)
        v_a = jnp.where(low_k, v_lo, one)
        v_b = jnp.where(low_k, one, v_hi)
        n_pairs = GROUP // 2
        qc0 = h * GROUP * HEAD_DIM
        q_h = jnp.concatenate([q_ref[:, qc0 + p * LANES:qc0 + (p + 1) * LANES] for p in range(n_pairs)], axis=0)
        acc, esink = [], []
        for side, (k_x, v_x) in enumerate(((k_a, v_a), (k_b, v_b))):
            s = lax.dot_general(q_h, k_x, dims, preferred_element_type=F32)
            probs, es = [], []
            for p in range(n_pairs):
                s_p = s[p * WINDOW:(p + 1) * WINDOW]
                s_v = jnp.where(from_prev, s_p[:, :WINDOW] + prev_bias, s_p[:, WINDOW:])
                sink = sink_ref[h * GROUP + 2 * p + side] * LOG2E
                m = jnp.maximum(jnp.max(s_v, axis=-1, keepdims=True), sink)
                e = jnp.exp2(s_v - m)
                probs.append(jnp.concatenate([jnp.where(from_prev, e, 0.0), jnp.where(from_prev, 0.0, e)],
                                             axis=1).astype(BF16))
                es.append(jnp.exp2(sink - m))
            acc.append(jnp.dot(jnp.concatenate(probs, axis=0), v_x, preferred_element_type=F32))
            esink.append(es)
        for p in range(n_pairs):
            rows = slice(p * WINDOW, (p + 1) * WINDOW)
            acc_a, acc_b = acc[0][rows], acc[1][rows]
            num = jnp.where(low_q, acc_a, acc_b)
            den = pltpu.roll(jnp.where(low_q, acc_b, acc_a), HEAD_DIM, 1) + jnp.where(low_q, esink[0][p], esink[1][p])
            o_ref[:, qc0 + p * LANES:qc0 + (p + 1) * LANES] = (num / den).astype(o_ref.dtype)


def _attn_prompt(q3, kv3, sinks):
    b, t, _ = q3.shape
    nb = t // WINDOW
    return pl.pallas_call(
        _attn_prompt_body,
        grid=(b, nb),
        in_specs=[
            pl.BlockSpec(memory_space=pltpu.SMEM),
            pl.BlockSpec((None, WINDOW, Q_DIM), lambda bi, i: (bi, i, 0)),
            pl.BlockSpec((None, WINDOW, 2 * KV_DIM), lambda bi, i: (bi, i, 0)),
            pl.BlockSpec((None, WINDOW, 2 * KV_DIM), lambda bi, i: (bi, jnp.maximum(i - 1, 0), 0)),
        ],
        out_specs=pl.BlockSpec((None, WINDOW, Q_DIM), lambda bi, i: (bi, i, 0)),
        out_shape=jax.ShapeDtypeStruct((b, t, Q_DIM), BF16),
        compiler_params=_params(("arbitrary", "arbitrary")),
        name="attn_prompt",
    )(sinks, q3, kv3, kv3)


SAMPLE_BATCH_TILE = 8


def _attn_sample_body(sink_ref, q_ref, kvn_ref, ck_ref, cv_ref, o_ref, *, n_steps):
    bb = SAMPLE_BATCH_TILE
    n_new = n_steps * bb
    n_cache = bb * WINDOW
    n_keys = n_cache + n_new
    n_keys_pad = -(-n_keys // LANES) * LANES
    q = q_ref[...].reshape(n_new, Q_DIM)
    kvn = kvn_ref[...].reshape(n_new, 2 * KV_DIM)
    pad = jnp.zeros((n_keys_pad - n_keys, KV_DIM), F32)
    k_all = jnp.concatenate([ck_ref[...].reshape(n_cache, KV_DIM), kvn[:, :KV_DIM], pad], axis=0).astype(BF16)
    v_all = jnp.concatenate([cv_ref[...].reshape(n_cache, KV_DIM), kvn[:, KV_DIM:], pad], axis=0).astype(BF16)
    n_rows = GROUP * n_new
    r = lax.broadcasted_iota(jnp.int32, (n_rows, n_keys_pad), 0)
    c = lax.broadcasted_iota(jnp.int32, (n_rows, n_keys_pad), 1)
    b_r = r % bb
    t_r = (r // bb) % n_steps
    is_cache = c < n_cache
    c2 = jnp.maximum(c - n_cache, 0)
    b_c = jnp.where(is_cache, c // WINDOW, c2 % bb)
    k_step = jnp.where(is_cache, c % WINDOW - WINDOW, c2 // bb)
    dq = t_r - k_step
    allowed = jnp.logical_and(jnp.logical_and(b_c == b_r, c < n_keys),
                              jnp.logical_and(dq >= 0, dq < WINDOW))
    grp = lax.broadcasted_iota(jnp.int32, (n_rows, 1), 0) // n_new
    dims = (((1,), (1,)), ((), ()))
    for h in range(N_KV_HEADS):
        k_h = k_all[:, h * HEAD_DIM:(h + 1) * HEAD_DIM]
        v_h = v_all[:, h * HEAD_DIM:(h + 1) * HEAD_DIM]
        q_h = jnp.concatenate(
            [q[:, (h * GROUP + g) * HEAD_DIM:(h * GROUP + g + 1) * HEAD_DIM] for g in range(GROUP)], axis=0)
        sink = jnp.zeros((n_rows, 1), F32)
        for g in range(GROUP):
            sink = jnp.where(grp == g, sink_ref[h * GROUP + g] * LOG2E, sink)
        s = lax.dot_general(q_h, k_h, dims, preferred_element_type=F32)
        s = jnp.where(allowed, s, -jnp.inf)
        m = jnp.maximum(jnp.max(s, axis=-1, keepdims=True), sink)
        p = jnp.exp2(s - m)
        denom = jnp.sum(p, axis=-1, keepdims=True) + jnp.exp2(sink - m)
        o_h = jnp.dot(p.astype(BF16), v_h, preferred_element_type=F32) / denom
        for g in range(GROUP):
            c0 = (h * GROUP + g) * HEAD_DIM
            o_ref[:, :, c0:c0 + HEAD_DIM] = (
                o_h[g * n_new:(g + 1) * n_new].reshape(n_steps, bb, HEAD_DIM).astype(o_ref.dtype))


def _attn_sample(q3, kvn3, cache_k, cache_v, sinks):
    n_steps, b, _ = q3.shape
    bb = SAMPLE_BATCH_TILE
    return pl.pallas_call(
        functools.partial(_attn_sample_body, n_steps=n_steps),
        grid=(b // bb,),
        in_specs=[
            pl.BlockSpec(memory_space=pltpu.SMEM),
            pl.BlockSpec((n_steps, bb, Q_DIM), lambda i: (0, i, 0)),
            pl.BlockSpec((n_steps, bb, 2 * KV_DIM), lambda i: (0, i, 0)),
            pl.BlockSpec((bb, WINDOW, KV_DIM), lambda i: (i, 0, 0)),
            pl.BlockSpec((bb, WINDOW, KV_DIM), lambda i: (i, 0, 0)),
        ],
        out_specs=pl.BlockSpec((n_steps, bb, Q_DIM), lambda i: (0, i, 0)),
        out_shape=jax.ShapeDtypeStruct((n_steps, b, Q_DIM), BF16),
        compiler_params=_params(("arbitrary",)),
        name="attn_sample",
    )(sinks, q3, kvn3, cache_k, cache_v)


TILES = {
    "rec_in": dict(bm=512, bn=2048, cols_outer=True),
    "rec_out": dict(bm=512, bn=2048),
    "attn_out": dict(bm=512, bn=2048),
    "mlp_up": dict(bm=1024, bn=2048),
    "mlp_down": dict(bm=1024, bn=1024, bk=2048),
    "mlp_down_last": dict(bm=512, bn=2048, bk=1024),
}


def _mm(name, x, w, **kw):
    return _matmul(x, w, name=name, **TILES[name], **kw)


def _flat(w):
    return w.reshape(-1, w.shape[-1])


def _trunk(xn, x, lru_fn, attn_fn, raw, w):
    assert TILES["rec_in"]["bn"] == D_RNN
    if "rec_w_out" not in w:
        gx, w_out, w_up = _mm("rec_in", xn, w["rec_w_in"], out_dtype=F32, act="gelu_first_block", side_casts=[
            (_flat(raw["rec_w_out"]), None), (_flat(raw["mlp_w_up"]), raw["norm_mlp"].reshape(-1))])
        w["rec_w_out"] = w_out.reshape(raw["rec_w_out"].shape)
        w["mlp_w_up"] = w_up.reshape(raw["mlp_w_up"].shape)
    else:
        gx = _mm("rec_in", xn, w["rec_w_in"], out_dtype=F32, act="gelu_first_block")
    z, h_last = lru_fn(gx)
    h, hb, ssq = _mm("rec_out", z, w["rec_w_out"], out_dtype=F32, resid=x, bf16_copy=True, ssq_out=True)
    if "mlp_w_down" not in w:
        hid, w_down, w_kv, w_q, w_o = _mm("mlp_up", hb, w["mlp_w_up"], layer=0, out_dtype=BF16, act="relu2",
                                          ssq_in=ssq, side_casts=[
            (_flat(raw["mlp_w_down"]), None), (raw["w_kv"], raw["kv_norm"]),
            (_flat(raw["attn_w_q"]), raw["norm_mix"][1]), (_flat(raw["attn_w_o"]), None)])
        w["mlp_w_down"] = w_down.reshape(raw["mlp_w_down"].shape)
        w["w_kv"] = w_kv[None]
        w["attn_w_q"] = w_q.reshape(raw["attn_w_q"].shape)
        w["attn_w_o"] = w_o.reshape(raw["attn_w_o"].shape)
    else:
        hid = _mm("mlp_up", hb, w["mlp_w_up"], layer=0, out_dtype=BF16, act="relu2", ssq_in=ssq)
    h, hb, ssq = _mm("mlp_down", hid, w["mlp_w_down"], layer=0, out_dtype=F32, resid=h, bf16_copy=True, ssq_out=True)
    q, kv = _qkv_proj(hb, ssq, w["attn_w_q"][0], w["w_kv"][0])
    o = attn_fn(q, kv)
    h, hb, ssq = _mm("attn_out", o, w["attn_w_o"], out_dtype=F32, resid=h, bf16_copy=True, ssq_out=True)
    hid = _mm("mlp_up", hb, w["mlp_w_up"], layer=1, out_dtype=BF16, act="relu2", ssq_in=ssq)
    out = _mm("mlp_down_last", hid, w["mlp_w_down"], layer=1, out_dtype=F32, resid=h, norm_gain=raw["final_norm"])
    return out, gx, h_last, kv


def kernel(x_prompt, x_sample, state_conv, state_h, cache_k, cache_v, norm_mix, norm_mlp, rec_w_in,
           rec_conv_w, rec_conv_b, rec_gate_a_w, rec_gate_a_b, rec_gate_x_w, rec_gate_x_b, rec_lambda,
           rec_w_out, kv_norm, w_kv, attn_w_q, attn_sinks, attn_w_o, mlp_w_up, mlp_w_down, final_norm):
    b, t, d = x_prompt.shape
    sb, st, _ = x_sample.shape
    c = D_RNN
    n_tail = CONV_W - 1

    raw = {
        "norm_mix": norm_mix, "norm_mlp": norm_mlp, "kv_norm": kv_norm, "final_norm": final_norm,
        "rec_w_out": rec_w_out, "w_kv": w_kv, "attn_w_q": attn_w_q, "attn_w_o": attn_w_o,
        "mlp_w_up": mlp_w_up, "mlp_w_down": mlp_w_down,
    }
    w = {"rec_w_in": rec_w_in.astype(BF16)}
    conv_w = rec_conv_w[0]
    conv_b = rec_conv_b[0].reshape(1, c)
    lam = rec_lambda[0].reshape(1, c)
    w_gates = jnp.concatenate([rec_gate_a_w[0], rec_gate_x_w[0]], axis=-1).astype(BF16)
    b_gates = jnp.concatenate([rec_gate_a_b[0], rec_gate_x_b[0]], axis=-1).reshape(N_LRU_BLOCKS, 1, 2 * LRU_BLOCK)
    sinks = attn_sinks[0].astype(F32)

    def lru_prompt(gx):
        z, ht = _lru(gx, jnp.zeros((n_tail * b, c), F32), jnp.zeros((b, c), F32),
                     conv_w, conv_b, w_gates, b_gates, lam, n_batch=b, n_steps=64, batch_major_out=True)
        return z.reshape(b * t, c), ht

    def attn_prompt(q, kv):
        o = _attn_prompt(q.reshape(b, t, Q_DIM), kv.reshape(b, t, 2 * KV_DIM), sinks)
        return o.reshape(b * t, Q_DIM)

    xn_p = _rmsnorm_to_time_major(x_prompt, norm_mix[0])
    y_p, gx_p, h_p, kv_p = _trunk(xn_p, x_prompt.reshape(b * t, d), lru_prompt, attn_prompt, raw, w)
    y_prompt = y_p.reshape(b, t, d)
    conv_p = jnp.swapaxes(gx_p.reshape(t, b, 2 * c)[t - n_tail:, :, c:], 0, 1).reshape(b, 1, n_tail, c)
    keep = min(WINDOW, t)
    kv_tail = kv_p.reshape(b, t, 2 * KV_DIM)[:, t - keep:]
    new_k_prompt = kv_tail[..., :KV_DIM].reshape(b, keep, N_KV_HEADS, HEAD_DIM)
    new_v_prompt = kv_tail[..., KV_DIM:].reshape(b, keep, N_KV_HEADS, HEAD_DIM)

    conv_state_tb = jnp.swapaxes(state_conv[:, 0], 0, 1).reshape(n_tail * sb, c)
    h0 = state_h[:, 0]
    ck = cache_k.reshape(sb, WINDOW, KV_DIM)
    cv = cache_v.reshape(sb, WINDOW, KV_DIM)

    def lru_sample(gx):
        return _lru(gx, conv_state_tb, h0, conv_w, conv_b, w_gates, b_gates, lam, n_batch=sb, n_steps=st,
                    batch_major_out=False)

    def attn_sample(q, kv):
        o = _attn_sample(q.reshape(st, sb, Q_DIM), kv.reshape(st, sb, 2 * KV_DIM), ck, cv, sinks)
        return o.reshape(st * sb, Q_DIM)

    x_s = jnp.swapaxes(x_sample, 0, 1).reshape(st * sb, d)
    y_s, gx_s, h_s, kv_s = _trunk(_rmsnorm(x_s, norm_mix[0], BF16), x_s, lru_sample, attn_sample, raw, w)
    y_sample = jnp.swapaxes(y_s.reshape(st, sb, d), 0, 1)
    conv_s = jnp.swapaxes(gx_s.reshape(st, sb, 2 * c)[st - n_tail:, :, c:], 0, 1).reshape(sb, 1, n_tail, c)
    kv_s = jnp.swapaxes(kv_s.reshape(st, sb, 2 * KV_DIM), 0, 1)
    new_k_sample = kv_s[..., :KV_DIM].reshape(sb, st, N_KV_HEADS, HEAD_DIM)
    new_v_sample = kv_s[..., KV_DIM:].reshape(sb, st, N_KV_HEADS, HEAD_DIM)

    return (y_prompt, y_sample, conv_p, h_p.reshape(b, 1, c), new_k_prompt, new_v_prompt,
            conv_s, h_s.reshape(sb, 1, c), new_k_sample, new_v_sample)
```

```python
import functools
import math

import jax
import jax.numpy as jnp
from jax import lax
from jax.experimental import pallas as pl
from jax.experimental.pallas import tpu as pltpu

F32 = jnp.float32
BF16 = jnp.bfloat16

D_MODEL = 2048
D_RNN = 2048
N_LRU_BLOCKS = 16
LRU_BLOCK = D_RNN // N_LRU_BLOCKS
CONV_W = 4
LRU_C = 8.0
N_HEADS = 32
N_KV_HEADS = 4
HEAD_DIM = 64
GROUP = N_HEADS // N_KV_HEADS
Q_DIM = N_HEADS * HEAD_DIM
KV_DIM = N_KV_HEADS * HEAD_DIM
WINDOW = 128
EPS = 1e-6
LOG2E = math.log2(math.e)

SUBLANES = 8
LANES = 128
VMEM_LIMIT_BYTES = 52 * 1024 * 1024


def _params(semantics):
    return pltpu.CompilerParams(dimension_semantics=semantics,
                                vmem_limit_bytes=VMEM_LIMIT_BYTES)


def _rmsnorm_body(x_ref, g_ref, o_ref):
    x = x_ref[...]
    y = x * lax.rsqrt(jnp.mean(x * x, axis=-1, keepdims=True) + EPS)
    o_ref[...] = (y * g_ref[...]).astype(o_ref.dtype)


def _rmsnorm(x, gain, out_dtype, tm=512):
    m, d = x.shape
    tm = min(tm, m)
    row_spec = pl.BlockSpec((tm, d), lambda i: (i, 0))
    return pl.pallas_call(
        _rmsnorm_body,
        grid=(m // tm,),
        in_specs=[row_spec, pl.BlockSpec((1, d), lambda i: (0, 0))],
        out_specs=row_spec,
        out_shape=jax.ShapeDtypeStruct((m, d), out_dtype),
        compiler_params=_params(("arbitrary",)),
        name="rmsnorm",
    )(x, gain.reshape(1, d).astype(F32))


PERM_STEPS = 32


def _perm_matrix(n_batch, to_time_major):
    n = n_batch * PERM_STEPS
    r = lax.broadcasted_iota(jnp.int32, (n, n), 0)
    c = lax.broadcasted_iota(jnp.int32, (n, n), 1)
    if to_time_major:
        src = (r % n_batch) * PERM_STEPS + r // n_batch
    else:
        src = (r % PERM_STEPS) * n_batch + r // PERM_STEPS
    return jnp.where(c == src, 1.0, 0.0).astype(BF16)


def _rmsnorm_tb_body(x_ref, g_ref, o_ref, *, n_batch, n_steps):
    perm = _perm_matrix(n_batch, True)
    rows = n_batch * PERM_STEPS
    g = g_ref[...]
    for part in range(n_steps // PERM_STEPS):
        x = x_ref[:, part * PERM_STEPS:(part + 1) * PERM_STEPS, :].reshape(rows, x_ref.shape[-1])
        y = x * lax.rsqrt(jnp.mean(x * x, axis=-1, keepdims=True) + EPS)
        y = (y * g).astype(BF16)
        o_ref[part * rows:(part + 1) * rows, :] = jnp.dot(perm, y, preferred_element_type=F32).astype(o_ref.dtype)


def _rmsnorm_to_time_major(x3, gain, n_steps=64):
    b, t, d = x3.shape
    return pl.pallas_call(
        functools.partial(_rmsnorm_tb_body, n_batch=b, n_steps=n_steps),
        grid=(t // n_steps,),
        in_specs=[pl.BlockSpec((b, n_steps, d), lambda i: (0, i, 0)), pl.BlockSpec((1, d), lambda i: (0, 0))],
        out_specs=pl.BlockSpec((n_steps * b, d), lambda i: (i, 0)),
        out_shape=jax.ShapeDtypeStruct((t * b, d), BF16),
        compiler_params=_params(("arbitrary",)),
        name="rmsnorm_tb",
    )(x3, gain.reshape(1, d).astype(F32))


def _matmul_body(*refs, nk, act, has_scale, has_resid, has_copy, has_ssq, side_gains, norm_dim):
    refs = list(refs)
    x_ref, w_ref = refs[:2]
    pos = 2
    s_ref = r_ref = b_ref = q_ref = None
    if has_scale:
        s_ref = refs[pos]
        pos += 1
    if has_resid:
        r_ref = refs[pos]
        pos += 1
    side_in = []
    for has_gain in side_gains:
        side_in.append((refs[pos], refs[pos + 1] if has_gain else None))
        pos += 2 if has_gain else 1
    o_ref = refs[pos]
    pos += 1
    if has_copy:
        b_ref = refs[pos]
        pos += 1
    if has_ssq:
        q_ref = refs[pos]
        pos += 1
    side_out = refs[pos:pos + len(side_gains)]
    pos += len(side_gains)
    acc_ref = refs[pos] if nk > 1 else None

    def partial_product():
        for (src_ref, g_ref), dst_ref in zip(side_in, side_out):
            chunk = src_ref[...]
            if g_ref is not None:
                chunk = chunk * g_ref[...]
            dst_ref[...] = chunk.astype(dst_ref.dtype)
        return jnp.dot(x_ref[...], w_ref[...], preferred_element_type=F32)

    def finish(acc):
        if s_ref is not None:
            acc = acc * lax.rsqrt(jnp.sum(s_ref[...], axis=-1, keepdims=True) * (1.0 / norm_dim) + EPS)
        if act == "relu2":
            r = jnp.maximum(acc, 0.0)
            acc = r * r
        if r_ref is not None:
            acc = r_ref[...] + acc
        o_ref[...] = acc.astype(o_ref.dtype)
        if b_ref is not None:
            b_ref[...] = acc.astype(b_ref.dtype)
        if q_ref is not None:
            sq = acc * acc
            tot = sq[:, 0:LANES]
            for c in range(1, sq.shape[1] // LANES):
                tot = tot + sq[:, c * LANES:(c + 1) * LANES]
            q_ref[...] = tot

    if nk == 1:
        finish(partial_product())
    else:
        k = pl.program_id(2)

        @pl.when(k == 0)
        def _():
            acc_ref[...] = partial_product()

        @pl.when(jnp.logical_and(k > 0, k < nk - 1))
        def _():
            acc_ref[...] += partial_product()

        @pl.when(k == nk - 1)
        def _():
            finish(acc_ref[...] + partial_product())


def _matmul(x, w, *, out_dtype, layer=0, act=None, ssq_in=None, resid=None, bf16_copy=False, ssq_out=False,
            side_casts=(), cols_outer=False, bm=1024, bn=1024, bk=None, name="matmul"):
    m = x.shape[0]
    _, kdim, n = w.shape
    bm, bn = min(bm, m), min(bn, n)
    bk = kdim if bk is None else min(bk, kdim)
    nk, nb = kdim // bk, n // bn
    mb = m // bm
    n_steps = mb * nb * nk

    def spec(shape, index_map):
        if cols_outer:
            return pl.BlockSpec(shape, lambda j, i, k: index_map(i, j, k))
        return pl.BlockSpec(shape, index_map)

    in_specs = [spec((bm, bk), lambda i, j, k: (i, k)),
                spec((None, bk, bn), lambda i, j, k: (layer, k, j))]
    args = [x, w]
    if ssq_in is not None:
        in_specs.append(spec((bm, ssq_in.shape[1]), lambda i, j, k: (i, 0)))
        args.append(ssq_in)
    tile_spec = spec((bm, bn), lambda i, j, k: (i, j))
    if resid is not None:
        in_specs.append(tile_spec)
        args.append(resid)
    step = lambda i, j, k: (i * nb + j) * nk + k
    for src, gain in side_casts:
        rows, cols = src.shape
        chunk = rows // n_steps
        assert chunk * n_steps == rows and chunk % (2 * SUBLANES) == 0, (name, rows, n_steps)
        in_specs.append(spec((chunk, cols), lambda i, j, k: (step(i, j, k), 0)))
        args.append(src)
        if gain is not None:
            in_specs.append(spec((chunk, 1), lambda i, j, k: (step(i, j, k), 0)))
            args.append(gain.reshape(rows, 1).astype(F32))

    out_specs = [tile_spec]
    out_shape = [jax.ShapeDtypeStruct((m, n), out_dtype)]
    if bf16_copy:
        out_specs.append(tile_spec)
        out_shape.append(jax.ShapeDtypeStruct((m, n), BF16))
    if ssq_out:
        out_specs.append(spec((bm, LANES), lambda i, j, k: (i, j)))
        out_shape.append(jax.ShapeDtypeStruct((m, nb * LANES), F32))
    for src, _ in side_casts:
        rows, cols = src.shape
        out_specs.append(spec((rows // n_steps, cols), lambda i, j, k: (step(i, j, k), 0)))
        out_shape.append(jax.ShapeDtypeStruct((rows, cols), BF16))

    outs = pl.pallas_call(
        functools.partial(_matmul_body, nk=nk, act=act, has_scale=ssq_in is not None,
                          has_resid=resid is not None, has_copy=bf16_copy, has_ssq=ssq_out,
                          side_gains=tuple(g is not None for _, g in side_casts), norm_dim=kdim),
        grid=(nb, mb, nk) if cols_outer else (mb, nb, nk),
        in_specs=in_specs,
        out_specs=out_specs,
        out_shape=out_shape,
        scratch_shapes=[pltpu.VMEM((bm, bn), F32)] if nk > 1 else [],
        compiler_params=_params(("arbitrary", "arbitrary", "arbitrary")),
        name=name,
    )(*args)
    return outs[0] if len(outs) == 1 else outs


def _qkv_body(x_ref, wq_ref, wkv_ref, s_ref, q_ref, kv_ref):
    x = x_ref[...]
    scale = lax.rsqrt(jnp.sum(s_ref[...], axis=-1, keepdims=True) * (1.0 / x.shape[1]) + EPS)
    q = jnp.dot(x, wq_ref[...], preferred_element_type=F32)
    q_ref[...] = (q * (scale * (HEAD_DIM ** -0.5 * LOG2E))).astype(q_ref.dtype)
    kv_ref[...] = jnp.dot(x, wkv_ref[...], preferred_element_type=F32) * scale


def _qkv_proj(x, ssq, w_q, w_kv, bm=512):
    m, kdim = x.shape
    bm = min(bm, m)
    return pl.pallas_call(
        _qkv_body,
        grid=(m // bm,),
        in_specs=[pl.BlockSpec((bm, kdim), lambda i: (i, 0)),
                  pl.BlockSpec((kdim, Q_DIM), lambda i: (0, 0)),
                  pl.BlockSpec((kdim, 2 * KV_DIM), lambda i: (0, 0)),
                  pl.BlockSpec((bm, ssq.shape[1]), lambda i: (i, 0))],
        out_specs=[pl.BlockSpec((bm, Q_DIM), lambda i: (i, 0)),
                   pl.BlockSpec((bm, 2 * KV_DIM), lambda i: (i, 0))],
        out_shape=[jax.ShapeDtypeStruct((m, Q_DIM), BF16), jax.ShapeDtypeStruct((m, 2 * KV_DIM), F32)],
        compiler_params=_params(("arbitrary",)),
        name="qkv_proj",
    )(x, w_q, w_kv, ssq)


def _gelu_tanh(x):
    return x * (0.5 + 0.5 * jnp.tanh(0.7978845608028654 * (x + 0.044715 * (x * x * x))))


def _softplus(z):
    return jnp.maximum(z, 0.0) + jnp.log1p(jnp.exp(-jnp.abs(z)))


def _lru_decay(lam):
    return (-0.5 * LRU_C * LOG2E) * _softplus(-lam)


def _lru_columns(gate, x, tail, h, cw, cb, wg, bg, d, *, n_steps, n_batch):
    n_rows = n_steps * n_batch
    x_ext = jnp.concatenate([tail, x], axis=0)
    xc = cb + cw[0:1] * x_ext[0:n_rows]
    for j in range(1, CONV_W):
        xc = xc + cw[j:j + 1] * x_ext[j * n_batch:j * n_batch + n_rows]
    half_pre = 0.5 * (jnp.dot(xc.astype(BF16), wg, preferred_element_type=F32) + bg)
    tr = jnp.tanh(half_pre[:, :LRU_BLOCK])
    ti = jnp.tanh(half_pre[:, LRU_BLOCK:])
    a = jnp.exp2(d * tr + d)
    gain2 = 1.0 - a * a
    u = (gain2 * lax.rsqrt(jnp.maximum(gain2, 1e-30))) * ((0.5 + 0.5 * ti) * xc)
    hs = []
    for s in range(n_steps):
        rows = slice(s * n_batch, (s + 1) * n_batch)
        h = a[rows] * h + u[rows]
        hs.append(h)
    return _gelu_tanh(gate) * jnp.concatenate(hs, axis=0), x_ext[n_rows:], h


def _lru_body(gate_ref, xr_ref, cs_ref, h0_ref, cw_ref, cb_ref, wg_ref, bg_ref, lam_ref,
              z_ref, ht_ref, h_sc, tail_sc, *, n_steps, n_batch, tc_cols):
    t = pl.program_id(1)

    @pl.when(t == 0)
    def _():
        h_sc[...] = h0_ref[...]
        tail_sc[...] = cs_ref[...]

    decay = _lru_decay(lam_ref[...])
    for n in range(tc_cols // LRU_BLOCK):
        cols = slice(n * LRU_BLOCK, (n + 1) * LRU_BLOCK)
        z, tail_sc[:, cols], h_sc[:, cols] = _lru_columns(
            gate_ref[:, cols], xr_ref[:, cols], tail_sc[:, cols], h_sc[:, cols], cw_ref[:, cols], cb_ref[:, cols],
            wg_ref[n], bg_ref[n], decay[:, cols], n_steps=n_steps, n_batch=n_batch)
        z_ref[:, cols] = z.astype(z_ref.dtype)

    @pl.when(t == pl.num_programs(1) - 1)
    def _():
        ht_ref[...] = h_sc[...]


def _lru(gx, conv_state, h0, conv_w, conv_b, w_gates, b_gates, lam, *, n_batch, n_steps, tc_cols=512):
    m = gx.shape[0]
    c = D_RNN
    n_rows = n_steps * n_batch
    n_tail = (CONV_W - 1) * n_batch
    ncb = c // tc_cols
    gpb = tc_cols // LRU_BLOCK
    z_spec = pl.BlockSpec((n_rows, tc_cols), lambda ci, ti: (ti, ci))
    z_shape = jax.ShapeDtypeStruct((m, c), BF16)
    z, ht = pl.pallas_call(
        functools.partial(_lru_body, n_steps=n_steps, n_batch=n_batch, tc_cols=tc_cols),
        grid=(ncb, m // n_rows),
        in_specs=[
            pl.BlockSpec((n_rows, tc_cols), lambda ci, ti: (ti, ci)),
            pl.BlockSpec((n_rows, tc_cols), lambda ci, ti: (ti, ncb + ci)),
            pl.BlockSpec((n_tail, tc_cols), lambda ci, ti: (0, ci)),
            pl.BlockSpec((n_batch, tc_cols), lambda ci, ti: (0, ci)),
            pl.BlockSpec((CONV_W, tc_cols), lambda ci, ti: (0, ci)),
            pl.BlockSpec((1, tc_cols), lambda ci, ti: (0, ci)),
            pl.BlockSpec((gpb, LRU_BLOCK, 2 * LRU_BLOCK), lambda ci, ti: (ci, 0, 0)),
            pl.BlockSpec((gpb, 1, 2 * LRU_BLOCK), lambda ci, ti: (ci, 0, 0)),
            pl.BlockSpec((1, tc_cols), lambda ci, ti: (0, ci)),
        ],
        out_specs=[z_spec, pl.BlockSpec((n_batch, tc_cols), lambda ci, ti: (0, ci))],
        out_shape=[z_shape, jax.ShapeDtypeStruct((n_batch, c), F32)],
        scratch_shapes=[pltpu.VMEM((n_batch, tc_cols), F32), pltpu.VMEM((n_tail, tc_cols), F32)],
        compiler_params=_params(("arbitrary", "arbitrary")),
        name="lru",
    )(gx, gx, conv_state, h0, conv_w, conv_b, w_gates, b_gates, lam)
    return z, ht


def _rec_front_body(*refs, n_batch, side_gains):
    xn_ref, win_ref, cw_ref, cb_ref, wg_ref, bg_ref, lam_ref = refs[:7]
    pos = 7
    side_in = []
    for has_gain in side_gains:
        side_in.append((refs[pos], refs[pos + 1] if has_gain else None))
        pos += 2 if has_gain else 1
    z_ref, ht_ref, tail_ref = refs[pos:pos + 3]
    pos += 3
    side_out = refs[pos:pos + len(side_gains)]
    gx_sc, h_sc, tail_sc = refs[pos + len(side_gains):]
    s = pl.program_id(0)
    c = D_RNN

    @pl.when(s <= 1)
    def _():
        h_sc[...] = jnp.zeros_like(h_sc)
        tail_sc[...] = jnp.zeros_like(tail_sc)

    @pl.when(s == 0)
    def _():
        gx_sc[1] = jnp.zeros(gx_sc.shape[1:], gx_sc.dtype)

    for (src_ref, g_ref), dst_ref in zip(side_in, side_out):
        chunk = src_ref[...]
        if g_ref is not None:
            chunk = chunk * g_ref[...]
        dst_ref[...] = chunk.astype(dst_ref.dtype)

    slot = s % 2
    gx_sc[slot] = jnp.dot(xn_ref[...], win_ref[...], preferred_element_type=F32)

    prev = 1 - slot
    perm = _perm_matrix(n_batch, False)
    decay = _lru_decay(lam_ref[...])
    for n in range(N_LRU_BLOCKS):
        cols = slice(n * LRU_BLOCK, (n + 1) * LRU_BLOCK)
        z, tail_sc[:, cols], h_sc[:, cols] = _lru_columns(
            gx_sc[prev, :, n * LRU_BLOCK:(n + 1) * LRU_BLOCK], gx_sc[prev, :, c + n * LRU_BLOCK:c + (n + 1) * LRU_BLOCK],
            tail_sc[:, cols], h_sc[:, cols], cw_ref[:, cols], cb_ref[:, cols], wg_ref[n], bg_ref[n], decay[:, cols],
            n_steps=PERM_STEPS, n_batch=n_batch)
        zp = jnp.dot(perm, z.astype(BF16), preferred_element_type=F32)
        z_ref[:, :, cols] = zp.astype(z_ref.dtype).reshape(n_batch, PERM_STEPS, LRU_BLOCK)

    @pl.when(s == pl.num_programs(0) - 1)
    def _():
        ht_ref[...] = h_sc[...]
        tail_ref[...] = tail_sc[...]


def _rec_front(xn, w_in, conv_w, conv_b, w_gates, b_gates, lam, *, n_batch, side_casts=()):
    m, d = xn.shape
    c = D_RNN
    n_rows = n_batch * PERM_STEPS
    n_blocks = m // n_rows
    n_tail = (CONV_W - 1) * n_batch
    blk = lambda s: jnp.minimum(s, n_blocks - 1)
    const = lambda *shape: pl.BlockSpec(shape, lambda s: (0,) * len(shape))

    in_specs = [pl.BlockSpec((n_rows, d), lambda s: (blk(s), 0)),
                pl.BlockSpec((d, 2 * c), lambda s: (0, 0), pipeline_mode=pl.Buffered(1)),
                const(CONV_W, c), const(1, c), const(N_LRU_BLOCKS, LRU_BLOCK, 2 * LRU_BLOCK),
                const(N_LRU_BLOCKS, 1, 2 * LRU_BLOCK), const(1, c)]
    args = [xn, w_in, conv_w, conv_b, w_gates, b_gates, lam]
    out_specs = [pl.BlockSpec((n_batch, PERM_STEPS, c), lambda s: (0, jnp.maximum(s - 1, 0), 0)),
                 const(n_batch, c), const(n_tail, c)]
    out_shape = [jax.ShapeDtypeStruct((n_batch, m // n_batch, c), BF16),
                 jax.ShapeDtypeStruct((n_batch, c), F32), jax.ShapeDtypeStruct((n_tail, c), F32)]
    for src, gain in side_casts:
        rows, cols = src.shape
        chunk = rows // n_blocks
        assert chunk * n_blocks == rows and chunk % (2 * SUBLANES) == 0, (rows, n_blocks)
        in_specs.append(pl.BlockSpec((chunk, cols), lambda s: (blk(s), 0)))
        args.append(src)
        if gain is not None:
            in_specs.append(pl.BlockSpec((chunk, 1), lambda s: (blk(s), 0)))
            args.append(gain.reshape(rows, 1).astype(F32))
        out_specs.append(pl.BlockSpec((chunk, cols), lambda s: (blk(s), 0)))
        out_shape.append(jax.ShapeDtypeStruct((rows, cols), BF16))

    return pl.pallas_call(
        functools.partial(_rec_front_body, n_batch=n_batch, side_gains=tuple(g is not None for _, g in side_casts)),
        grid=(n_blocks + 1,),
        in_specs=in_specs,
        out_specs=out_specs,
        out_shape=out_shape,
        scratch_shapes=[pltpu.VMEM((2, n_rows, 2 * c), F32), pltpu.VMEM((n_batch, c), F32),
                        pltpu.VMEM((n_tail, c), F32)],
        compiler_params=_params(("arbitrary",)),
        name="rec_front",
    )(*args)


def _attn_prompt_body(sink_ref, q_ref, kvc_ref, kvp_ref, o_ref):
    i = pl.program_id(1)
    kv = jnp.concatenate([kvp_ref[...], kvc_ref[...]], axis=0).astype(BF16)
    n_keys = 2 * WINDOW
    low_k = lax.broadcasted_iota(jnp.int32, (n_keys, LANES), 1) < HEAD_DIM
    low_q = lax.broadcasted_iota(jnp.int32, (WINDOW, LANES), 1) < HEAD_DIM
    zero = jnp.zeros((n_keys, LANES), BF16)
    one = jnp.ones((n_keys, LANES), BF16)
    from_prev = (lax.broadcasted_iota(jnp.int32, (WINDOW, WINDOW), 1)
                 > lax.broadcasted_iota(jnp.int32, (WINDOW, WINDOW), 0))
    prev_bias = jnp.where(i == 0, -jnp.inf, 0.0)
    dims = (((1,), (1,)), ((), ()))

    for h in range(N_KV_HEADS):
        c0 = (h // 2) * LANES
        k_t = kv[:, c0:c0 + LANES]
        v_t = kv[:, KV_DIM + c0:KV_DIM + c0 + LANES]
        k_sw = pltpu.roll(k_t, HEAD_DIM, 1)
        v_sw = pltpu.roll(v_t, HEAD_DIM, 1)
        k_lo, k_hi = (k_t, k_sw) if h % 2 == 0 else (k_sw, k_t)
        v_lo, v_hi = (v_t, v_sw) if h % 2 == 0 else (v_sw, v_t)
        k_a = jnp.where(low_k, k_lo, zero)
        k_b = jnp.where(low_k, zero, k_hi)
        v_a = jnp.where(low_k, v_lo, one)
        v_b = jnp.where(low_k, one, v_hi)
        n_pairs = GROUP // 2
        qc0 = h * GROUP * HEAD_DIM
        q_h = jnp.concatenate([q_ref[:, qc0 + p * LANES:qc0 + (p + 1) * LANES] for p in range(n_pairs)], axis=0)
        acc, esink = [], []
        for side, (k_x, v_x) in enumerate(((k_a, v_a), (k_b, v_b))):
            s = lax.dot_general(q_h, k_x, dims, preferred_element_type=F32)
            probs, es = [], []
            for p in range(n_pairs):
                s_p = s[p * WINDOW:(p + 1) * WINDOW]
                s_v = jnp.where(from_prev, s_p[:, :WINDOW] + prev_bias, s_p[:, WINDOW:])
                sink = sink_ref[h * GROUP + 2 * p + side] * LOG2E
                m = jnp.maximum(jnp.max(s_v, axis=-1, keepdims=True), sink)
                e = jnp.exp2(s_v - m)
                probs.append(jnp.concatenate([jnp.where(from_prev, e, 0.0), jnp.where(from_prev, 0.0, e)],
                                             axis=1).astype(BF16))
                es.append(jnp.exp2(sink - m))
            acc.append(jnp.dot(jnp.concatenate(probs, axis=0), v_x, preferred_element_type=F32))
            esink.append(es)
        for p in range(n_pairs):
            rows = slice(p * WINDOW, (p + 1) * WINDOW)
            acc_a, acc_b = acc[0][rows], acc[1][rows]
            num = jnp.where(low_q, acc_a, acc_b)
            den = pltpu.roll(jnp.where(low_q, acc_b, acc_a), HEAD_DIM, 1) + jnp.where(low_q, esink[0][p], esink[1][p])
            o_ref[:, qc0 + p * LANES:qc0 + (p + 1) * LANES] = (num / den).astype(o_ref.dtype)


def _attn_prompt(q3, kv3, sinks):
    b, t, _ = q3.shape
    nb = t // WINDOW
    return pl.pallas_call(
        _attn_prompt_body,
        grid=(b, nb),
        in_specs=[
            pl.BlockSpec(memory_space=pltpu.SMEM),
            pl.BlockSpec((None, WINDOW, Q_DIM), lambda bi, i: (bi, i, 0)),
            pl.BlockSpec((None, WINDOW, 2 * KV_DIM), lambda bi, i: (bi, i, 0)),
            pl.BlockSpec((None, WINDOW, 2 * KV_DIM), lambda bi, i: (bi, jnp.maximum(i - 1, 0), 0)),
        ],
        out_specs=pl.BlockSpec((None, WINDOW, Q_DIM), lambda bi, i: (bi, i, 0)),
        out_shape=jax.ShapeDtypeStruct((b, t, Q_DIM), BF16),
        compiler_params=_params(("arbitrary", "arbitrary")),
        name="attn_prompt",
    )(sinks, q3, kv3, kv3)


SAMPLE_BATCH_TILE = 8


def _attn_sample_body(sink_ref, q_ref, kvn_ref, ck_ref, cv_ref, o_ref, *, n_steps):
    bb = SAMPLE_BATCH_TILE
    n_new = n_steps * bb
    n_cache = bb * WINDOW
    n_keys = n_cache + n_new
    n_keys_pad = -(-n_keys // LANES) * LANES
    q = q_ref[...].reshape(n_new, Q_DIM)
    kvn = kvn_ref[...].reshape(n_new, 2 * KV_DIM)
    pad = jnp.zeros((n_keys_pad - n_keys, KV_DIM), F32)
    k_all = jnp.concatenate([ck_ref[...].reshape(n_cache, KV_DIM), kvn[:, :KV_DIM], pad], axis=0).astype(BF16)
    v_all = jnp.concatenate([cv_ref[...].reshape(n_cache, KV_DIM), kvn[:, KV_DIM:], pad], axis=0).astype(BF16)
    n_rows = GROUP * n_new
    r = lax.broadcasted_iota(jnp.int32, (n_rows, n_keys_pad), 0)
    c = lax.broadcasted_iota(jnp.int32, (n_rows, n_keys_pad), 1)
    b_r = r % bb
    t_r = (r // bb) % n_steps
    is_cache = c < n_cache
    c2 = jnp.maximum(c - n_cache, 0)
    b_c = jnp.where(is_cache, c // WINDOW, c2 % bb)
    k_step = jnp.where(is_cache, c % WINDOW - WINDOW, c2 // bb)
    dq = t_r - k_step
    allowed = jnp.logical_and(jnp.logical_and(b_c == b_r, c < n_keys),
                              jnp.logical_and(dq >= 0, dq < WINDOW))
    grp = lax.broadcasted_iota(jnp.int32, (n_rows, 1), 0) // n_new
    dims = (((1,), (1,)), ((), ()))
    for h in range(N_KV_HEADS):
        k_h = k_all[:, h * HEAD_DIM:(h + 1) * HEAD_DIM]
        v_h = v_all[:, h * HEAD_DIM:(h + 1) * HEAD_DIM]
        q_h = jnp.concatenate(
            [q[:, (h * GROUP + g) * HEAD_DIM:(h * GROUP + g + 1) * HEAD_DIM] for g in range(GROUP)], axis=0)
        sink = jnp.zeros((n_rows, 1), F32)
        for g in range(GROUP):
            sink = jnp.where(grp == g, sink_ref[h * GROUP + g] * LOG2E, sink)
        s = lax.dot_general(q_h, k_h, dims, preferred_element_type=F32)
        s = jnp.where(allowed, s, -jnp.inf)
        m = jnp.maximum(jnp.max(s, axis=-1, keepdims=True), sink)
        p = jnp.exp2(s - m)
        denom = jnp.sum(p, axis=-1, keepdims=True) + jnp.exp2(sink - m)
        o_h = jnp.dot(p.astype(BF16), v_h, preferred_element_type=F32) / denom
        for g in range(GROUP):
            c0 = (h * GROUP + g) * HEAD_DIM
            o_ref[:, :, c0:c0 + HEAD_DIM] = (
                o_h[g * n_new:(g + 1) * n_new].reshape(n_steps, bb, HEAD_DIM).astype(o_ref.dtype))


def _attn_sample(q3, kvn3, cache_k, cache_v, sinks):
    n_steps, b, _ = q3.shape
    bb = SAMPLE_BATCH_TILE
    return pl.pallas_call(
        functools.partial(_attn_sample_body, n_steps=n_steps),
        grid=(b // bb,),
        in_specs=[
            pl.BlockSpec(memory_space=pltpu.SMEM),
            pl.BlockSpec((n_steps, bb, Q_DIM), lambda i: (0, i, 0)),
            pl.BlockSpec((n_steps, bb, 2 * KV_DIM), lambda i: (0, i, 0)),
            pl.BlockSpec((bb, WINDOW, KV_DIM), lambda i: (i, 0, 0)),
            pl.BlockSpec((bb, WINDOW, KV_DIM), lambda i: (i, 0, 0)),
        ],
        out_specs=pl.BlockSpec((n_steps, bb, Q_DIM), lambda i: (0, i, 0)),
        out_shape=jax.ShapeDtypeStruct((n_steps, b, Q_DIM), BF16),
        compiler_params=_params(("arbitrary",)),
        name="attn_sample",
    )(sinks, q3, kvn3, cache_k, cache_v)


TILES = {
    "rec_in": dict(bm=512, bn=2048),
    "rec_out": dict(bm=512, bn=2048),
    "attn_out": dict(bm=512, bn=2048),
    "mlp_up": dict(bm=1024, bn=2048),
    "mlp_down": dict(bm=1024, bn=1024, bk=2048),
}


def _mm(name, x, w, **kw):
    return _matmul(x, w, name=name, **TILES[name], **kw)


def _flat(w):
    return w.reshape(-1, w.shape[-1])


def _trunk(z, x, attn_fn, raw, w):
    h, hb, ssq = _mm("rec_out", z, w["rec_w_out"], out_dtype=F32, resid=x, bf16_copy=True, ssq_out=True)
    if "mlp_w_down" not in w:
        hid, w_down, w_kv, w_q, w_o = _mm("mlp_up", hb, w["mlp_w_up"], layer=0, out_dtype=BF16, act="relu2",
                                          ssq_in=ssq, side_casts=[
            (_flat(raw["mlp_w_down"]), None), (raw["w_kv"], raw["kv_norm"]),
            (_flat(raw["attn_w_q"]), raw["norm_mix"][1]), (_flat(raw["attn_w_o"]), None)])
        w["mlp_w_down"] = w_down.reshape(raw["mlp_w_down"].shape)
        w["w_kv"] = w_kv
        w["attn_w_q"] = w_q
        w["attn_w_o"] = w_o.reshape(raw["attn_w_o"].shape)
    else:
        hid = _mm("mlp_up", hb, w["mlp_w_up"], layer=0, out_dtype=BF16, act="relu2", ssq_in=ssq)
    h, hb, ssq = _mm("mlp_down", hid, w["mlp_w_down"], layer=0, out_dtype=F32, resid=h, bf16_copy=True, ssq_out=True)
    q, kv = _qkv_proj(hb, ssq, w["attn_w_q"], w["w_kv"])
    o = attn_fn(q, kv)
    h, hb, ssq = _mm("attn_out", o, w["attn_w_o"], out_dtype=F32, resid=h, bf16_copy=True, ssq_out=True)
    hid = _mm("mlp_up", hb, w["mlp_w_up"], layer=1, out_dtype=BF16, act="relu2", ssq_in=ssq)
    h = _mm("mlp_down", hid, w["mlp_w_down"], layer=1, out_dtype=F32, resid=h)
    return _rmsnorm(h, raw["final_norm"], F32), kv


def kernel(x_prompt, x_sample, state_conv, state_h, cache_k, cache_v, norm_mix, norm_mlp, rec_w_in,
           rec_conv_w, rec_conv_b, rec_gate_a_w, rec_gate_a_b, rec_gate_x_w, rec_gate_x_b, rec_lambda,
           rec_w_out, kv_norm, w_kv, attn_w_q, attn_sinks, attn_w_o, mlp_w_up, mlp_w_down, final_norm):
    b, t, d = x_prompt.shape
    sb, st, _ = x_sample.shape
    c = D_RNN
    n_tail = CONV_W - 1

    raw = {
        "norm_mix": norm_mix, "kv_norm": kv_norm, "final_norm": final_norm,
        "w_kv": w_kv, "attn_w_q": attn_w_q, "attn_w_o": attn_w_o, "mlp_w_down": mlp_w_down,
    }
    w = {"rec_w_in": rec_w_in.astype(BF16)}
    conv_w = rec_conv_w[0]
    conv_b = rec_conv_b[0].reshape(1, c)
    lam = rec_lambda[0].reshape(1, c)
    w_gates = jnp.concatenate([rec_gate_a_w[0], rec_gate_x_w[0]], axis=-1).astype(BF16)
    b_gates = jnp.concatenate([rec_gate_a_b[0], rec_gate_x_b[0]], axis=-1).reshape(N_LRU_BLOCKS, 1, 2 * LRU_BLOCK)
    sinks = attn_sinks[0].astype(F32)

    def attn_prompt(q, kv):
        o = _attn_prompt(q.reshape(b, t, Q_DIM), kv.reshape(b, t, 2 * KV_DIM), sinks)
        return o.reshape(b * t, Q_DIM)

    xn_p = _rmsnorm_to_time_major(x_prompt, norm_mix[0])
    z_p, h_p, tail_p, w_out, w_up = _rec_front(
        xn_p, w["rec_w_in"][0], conv_w, conv_b, w_gates, b_gates, lam, n_batch=b,
        side_casts=[(_flat(rec_w_out), None), (_flat(mlp_w_up), norm_mlp.reshape(-1))])
    w["rec_w_out"] = w_out.reshape(rec_w_out.shape)
    w["mlp_w_up"] = w_up.reshape(mlp_w_up.shape)
    y_p, kv_p = _trunk(z_p.reshape(b * t, c), x_prompt.reshape(b * t, d), attn_prompt, raw, w)
    y_prompt = y_p.reshape(b, t, d)
    conv_p = jnp.swapaxes(tail_p.reshape(n_tail, b, c), 0, 1).reshape(b, 1, n_tail, c)
    keep = min(WINDOW, t)
    kv_tail = kv_p.reshape(b, t, 2 * KV_DIM)[:, t - keep:]
    new_k_prompt = kv_tail[..., :KV_DIM].reshape(b, keep, N_KV_HEADS, HEAD_DIM)
    new_v_prompt = kv_tail[..., KV_DIM:].reshape(b, keep, N_KV_HEADS, HEAD_DIM)

    conv_state_tb = jnp.swapaxes(state_conv[:, 0], 0, 1).reshape(n_tail * sb, c)
    h0 = state_h[:, 0]
    ck = cache_k.reshape(sb, WINDOW, KV_DIM)
    cv = cache_v.reshape(sb, WINDOW, KV_DIM)

    def attn_sample(q, kv):
        o = _attn_sample(q.reshape(st, sb, Q_DIM), kv.reshape(st, sb, 2 * KV_DIM), ck, cv, sinks)
        return o.reshape(st * sb, Q_DIM)

    x_s = jnp.swapaxes(x_sample, 0, 1).reshape(st * sb, d)
    gx_s = _mm("rec_in", _rmsnorm(x_s, norm_mix[0], BF16), w["rec_w_in"], out_dtype=F32)
    z_s, h_s = _lru(gx_s, conv_state_tb, h0, conv_w, conv_b, w_gates, b_gates, lam, n_batch=sb, n_steps=st)
    y_s, kv_s = _trunk(z_s, x_s, attn_sample, raw, w)
    y_sample = jnp.swapaxes(y_s.reshape(st, sb, d), 0, 1)
    conv_s = jnp.swapaxes(gx_s.reshape(st, sb, 2 * c)[st - n_tail:, :, c:], 0, 1).reshape(sb, 1, n_tail, c)
    kv_s = jnp.swapaxes(kv_s.reshape(st, sb, 2 * KV_DIM), 0, 1)
    new_k_sample = kv_s[..., :KV_DIM].reshape(sb, st, N_KV_HEADS, HEAD_DIM)
    new_v_sample = kv_s[..., KV_DIM:].reshape(sb, st, N_KV_HEADS, HEAD_DIM)

    return (y_prompt, y_sample, conv_p, h_p.reshape(b, 1, c), new_k_prompt, new_v_prompt,
            conv_s, h_s.reshape(sb, 1, c), new_k_sample, new_v_sample)
```

```python
import functools
import math

import jax
import jax.numpy as jnp
from jax import lax
from jax.experimental import pallas as pl
from jax.experimental.pallas import tpu as pltpu

F32 = jnp.float32
BF16 = jnp.bfloat16

D_MODEL = 2048
D_RNN = 2048
N_LRU_BLOCKS = 16
LRU_BLOCK = D_RNN // N_LRU_BLOCKS
CONV_W = 4
LRU_C = 8.0
N_HEADS = 32
N_KV_HEADS = 4
HEAD_DIM = 64
GROUP = N_HEADS // N_KV_HEADS
Q_DIM = N_HEADS * HEAD_DIM
KV_DIM = N_KV_HEADS * HEAD_DIM
WINDOW = 128
EPS = 1e-6
LOG2E = math.log2(math.e)

SUBLANES = 8
LANES = 128
VMEM_LIMIT_BYTES = 52 * 1024 * 1024


def _params(semantics):
    return pltpu.CompilerParams(dimension_semantics=semantics,
                                vmem_limit_bytes=VMEM_LIMIT_BYTES)


def _rmsnorm_body(x_ref, g_ref, o_ref):
    x = x_ref[...]
    y = x * lax.rsqrt(jnp.mean(x * x, axis=-1, keepdims=True) + EPS)
    o_ref[...] = (y * g_ref[...]).astype(o_ref.dtype)


def _rmsnorm(x, gain, out_dtype, tm=512):
    m, d = x.shape
    tm = min(tm, m)
    row_spec = pl.BlockSpec((tm, d), lambda i: (i, 0))
    return pl.pallas_call(
        _rmsnorm_body,
        grid=(m // tm,),
        in_specs=[row_spec, pl.BlockSpec((1, d), lambda i: (0, 0))],
        out_specs=row_spec,
        out_shape=jax.ShapeDtypeStruct((m, d), out_dtype),
        compiler_params=_params(("arbitrary",)),
        name="rmsnorm",
    )(x, gain.reshape(1, d).astype(F32))


PERM_STEPS = 32


def _perm_matrix(n_batch, to_time_major):
    n = n_batch * PERM_STEPS
    r = lax.broadcasted_iota(jnp.int32, (n, n), 0)
    c = lax.broadcasted_iota(jnp.int32, (n, n), 1)
    if to_time_major:
        src = (r % n_batch) * PERM_STEPS + r // n_batch
    else:
        src = (r % PERM_STEPS) * n_batch + r // PERM_STEPS
    return jnp.where(c == src, 1.0, 0.0).astype(BF16)


def _matmul_body(*refs, nk, act, has_scale, has_resid, has_copy, has_ssq, side_gains, norm_dim):
    refs = list(refs)
    x_ref, w_ref = refs[:2]
    pos = 2
    s_ref = r_ref = b_ref = q_ref = None
    if has_scale:
        s_ref = refs[pos]
        pos += 1
    if has_resid:
        r_ref = refs[pos]
        pos += 1
    side_in = []
    for has_gain in side_gains:
        side_in.append((refs[pos], refs[pos + 1] if has_gain else None))
        pos += 2 if has_gain else 1
    o_ref = refs[pos]
    pos += 1
    if has_copy:
        b_ref = refs[pos]
        pos += 1
    if has_ssq:
        q_ref = refs[pos]
        pos += 1
    side_out = refs[pos:pos + len(side_gains)]
    pos += len(side_gains)
    acc_ref = refs[pos] if nk > 1 else None

    def partial_product():
        for (src_ref, g_ref), dst_ref in zip(side_in, side_out):
            chunk = src_ref[...]
            if g_ref is not None:
                chunk = chunk * g_ref[...]
            dst_ref[...] = chunk.astype(dst_ref.dtype)
        return jnp.dot(x_ref[...], w_ref[...], preferred_element_type=F32)

    def finish(acc):
        if s_ref is not None:
            acc = acc * lax.rsqrt(jnp.sum(s_ref[...], axis=-1, keepdims=True) * (1.0 / norm_dim) + EPS)
        if act == "relu2":
            r = jnp.maximum(acc, 0.0)
            acc = r * r
        if r_ref is not None:
            acc = r_ref[...] + acc
        o_ref[...] = acc.astype(o_ref.dtype)
        if b_ref is not None:
            b_ref[...] = acc.astype(b_ref.dtype)
        if q_ref is not None:
            sq = acc * acc
            tot = sq[:, 0:LANES]
            for c in range(1, sq.shape[1] // LANES):
                tot = tot + sq[:, c * LANES:(c + 1) * LANES]
            q_ref[...] = tot

    if nk == 1:
        finish(partial_product())
    else:
        k = pl.program_id(2)

        @pl.when(k == 0)
        def _():
            acc_ref[...] = partial_product()

        @pl.when(jnp.logical_and(k > 0, k < nk - 1))
        def _():
            acc_ref[...] += partial_product()

        @pl.when(k == nk - 1)
        def _():
            finish(acc_ref[...] + partial_product())


def _matmul(x, w, *, out_dtype, layer=0, act=None, ssq_in=None, resid=None, bf16_copy=False, ssq_out=False,
            side_casts=(), cols_outer=False, bm=1024, bn=1024, bk=None, name="matmul"):
    m = x.shape[0]
    _, kdim, n = w.shape
    bm, bn = min(bm, m), min(bn, n)
    bk = kdim if bk is None else min(bk, kdim)
    nk, nb = kdim // bk, n // bn
    mb = m // bm
    n_steps = mb * nb * nk

    def spec(shape, index_map):
        if cols_outer:
            return pl.BlockSpec(shape, lambda j, i, k: index_map(i, j, k))
        return pl.BlockSpec(shape, index_map)

    in_specs = [spec((bm, bk), lambda i, j, k: (i, k)),
                spec((None, bk, bn), lambda i, j, k: (layer, k, j))]
    args = [x, w]
    if ssq_in is not None:
        in_specs.append(spec((bm, ssq_in.shape[1]), lambda i, j, k: (i, 0)))
        args.append(ssq_in)
    tile_spec = spec((bm, bn), lambda i, j, k: (i, j))
    if resid is not None:
        in_specs.append(tile_spec)
        args.append(resid)
    step = lambda i, j, k: (i * nb + j) * nk + k
    for src, gain in side_casts:
        rows, cols = src.shape
        chunk = rows // n_steps
        assert chunk * n_steps == rows and chunk % (2 * SUBLANES) == 0, (name, rows, n_steps)
        in_specs.append(spec((chunk, cols), lambda i, j, k: (step(i, j, k), 0)))
        args.append(src)
        if gain is not None:
            in_specs.append(spec((chunk, 1), lambda i, j, k: (step(i, j, k), 0)))
            args.append(gain.reshape(rows, 1).astype(F32))

    out_specs = [tile_spec]
    out_shape = [jax.ShapeDtypeStruct((m, n), out_dtype)]
    if bf16_copy:
        out_specs.append(tile_spec)
        out_shape.append(jax.ShapeDtypeStruct((m, n), BF16))
    if ssq_out:
        out_specs.append(spec((bm, LANES), lambda i, j, k: (i, j)))
        out_shape.append(jax.ShapeDtypeStruct((m, nb * LANES), F32))
    for src, _ in side_casts:
        rows, cols = src.shape
        out_specs.append(spec((rows // n_steps, cols), lambda i, j, k: (step(i, j, k), 0)))
        out_shape.append(jax.ShapeDtypeStruct((rows, cols), BF16))

    outs = pl.pallas_call(
        functools.partial(_matmul_body, nk=nk, act=act, has_scale=ssq_in is not None,
                          has_resid=resid is not None, has_copy=bf16_copy, has_ssq=ssq_out,
                          side_gains=tuple(g is not None for _, g in side_casts), norm_dim=kdim),
        grid=(nb, mb, nk) if cols_outer else (mb, nb, nk),
        in_specs=in_specs,
        out_specs=out_specs,
        out_shape=out_shape,
        scratch_shapes=[pltpu.VMEM((bm, bn), F32)] if nk > 1 else [],
        compiler_params=_params(("arbitrary", "arbitrary", "arbitrary")),
        name=name,
    )(*args)
    return outs[0] if len(outs) == 1 else outs


def _qkv_body(x_ref, wq_ref, wkv_ref, s_ref, q_ref, kv_ref):
    x = x_ref[...]
    scale = lax.rsqrt(jnp.sum(s_ref[...], axis=-1, keepdims=True) * (1.0 / x.shape[1]) + EPS)
    q = jnp.dot(x, wq_ref[...], preferred_element_type=F32)
    q_ref[...] = (q * (scale * (HEAD_DIM ** -0.5 * LOG2E))).astype(q_ref.dtype)
    kv_ref[...] = jnp.dot(x, wkv_ref[...], preferred_element_type=F32) * scale


def _qkv_proj(x, ssq, w_q, w_kv, bm=512):
    m, kdim = x.shape
    bm = min(bm, m)
    return pl.pallas_call(
        _qkv_body,
        grid=(m // bm,),
        in_specs=[pl.BlockSpec((bm, kdim), lambda i: (i, 0)),
                  pl.BlockSpec((kdim, Q_DIM), lambda i: (0, 0)),
                  pl.BlockSpec((kdim, 2 * KV_DIM), lambda i: (0, 0)),
                  pl.BlockSpec((bm, ssq.shape[1]), lambda i: (i, 0))],
        out_specs=[pl.BlockSpec((bm, Q_DIM), lambda i: (i, 0)),
                   pl.BlockSpec((bm, 2 * KV_DIM), lambda i: (i, 0))],
        out_shape=[jax.ShapeDtypeStruct((m, Q_DIM), BF16), jax.ShapeDtypeStruct((m, 2 * KV_DIM), F32)],
        compiler_params=_params(("arbitrary",)),
        name="qkv_proj",
    )(x, w_q, w_kv, ssq)


def _gelu_tanh(x):
    return x * (0.5 + 0.5 * jnp.tanh(0.7978845608028654 * (x + 0.044715 * (x * x * x))))


def _softplus(z):
    return jnp.maximum(z, 0.0) + jnp.log1p(jnp.exp(-jnp.abs(z)))


def _lru_decay(lam):
    return (-0.5 * LRU_C * LOG2E) * _softplus(-lam)


def _lru_columns(gate, x, tail, h, cw, cb, wg, bg, d, *, n_steps, n_batch):
    n_rows = n_steps * n_batch
    x_ext = jnp.concatenate([tail, x], axis=0)
    xc = cb + cw[0:1] * x_ext[0:n_rows]
    for j in range(1, CONV_W):
        xc = xc + cw[j:j + 1] * x_ext[j * n_batch:j * n_batch + n_rows]
    half_pre = 0.5 * (jnp.dot(xc.astype(BF16), wg, preferred_element_type=F32) + bg)
    tr = jnp.tanh(half_pre[:, :LRU_BLOCK])
    ti = jnp.tanh(half_pre[:, LRU_BLOCK:])
    a = jnp.exp2(d * tr + d)
    gain2 = 1.0 - a * a
    u = (gain2 * lax.rsqrt(jnp.maximum(gain2, 1e-30))) * ((0.5 + 0.5 * ti) * xc)
    hs = []
    for s in range(n_steps):
        rows = slice(s * n_batch, (s + 1) * n_batch)
        h = a[rows] * h + u[rows]
        hs.append(h)
    return _gelu_tanh(gate) * jnp.concatenate(hs, axis=0), x_ext[n_rows:], h


def _lru_body(gate_ref, xr_ref, cs_ref, h0_ref, cw_ref, cb_ref, wg_ref, bg_ref, lam_ref,
              z_ref, ht_ref, h_sc, tail_sc, *, n_steps, n_batch, tc_cols):
    t = pl.program_id(1)

    @pl.when(t == 0)
    def _():
        h_sc[...] = h0_ref[...]
        tail_sc[...] = cs_ref[...]

    decay = _lru_decay(lam_ref[...])
    for n in range(tc_cols // LRU_BLOCK):
        cols = slice(n * LRU_BLOCK, (n + 1) * LRU_BLOCK)
        z, tail_sc[:, cols], h_sc[:, cols] = _lru_columns(
            gate_ref[:, cols], xr_ref[:, cols], tail_sc[:, cols], h_sc[:, cols], cw_ref[:, cols], cb_ref[:, cols],
            wg_ref[n], bg_ref[n], decay[:, cols], n_steps=n_steps, n_batch=n_batch)
        z_ref[:, cols] = z.astype(z_ref.dtype)

    @pl.when(t == pl.num_programs(1) - 1)
    def _():
        ht_ref[...] = h_sc[...]


def _lru(gx, conv_state, h0, conv_w, conv_b, w_gates, b_gates, lam, *, n_batch, n_steps, tc_cols=512):
    m = gx.shape[0]
    c = D_RNN
    n_rows = n_steps * n_batch
    n_tail = (CONV_W - 1) * n_batch
    ncb = c // tc_cols
    gpb = tc_cols // LRU_BLOCK
    z_spec = pl.BlockSpec((n_rows, tc_cols), lambda ci, ti: (ti, ci))
    z_shape = jax.ShapeDtypeStruct((m, c), BF16)
    z, ht = pl.pallas_call(
        functools.partial(_lru_body, n_steps=n_steps, n_batch=n_batch, tc_cols=tc_cols),
        grid=(ncb, m // n_rows),
        in_specs=[
            pl.BlockSpec((n_rows, tc_cols), lambda ci, ti: (ti, ci)),
            pl.BlockSpec((n_rows, tc_cols), lambda ci, ti: (ti, ncb + ci)),
            pl.BlockSpec((n_tail, tc_cols), lambda ci, ti: (0, ci)),
            pl.BlockSpec((n_batch, tc_cols), lambda ci, ti: (0, ci)),
            pl.BlockSpec((CONV_W, tc_cols), lambda ci, ti: (0, ci)),
            pl.BlockSpec((1, tc_cols), lambda ci, ti: (0, ci)),
            pl.BlockSpec((gpb, LRU_BLOCK, 2 * LRU_BLOCK), lambda ci, ti: (ci, 0, 0)),
            pl.BlockSpec((gpb, 1, 2 * LRU_BLOCK), lambda ci, ti: (ci, 0, 0)),
            pl.BlockSpec((1, tc_cols), lambda ci, ti: (0, ci)),
        ],
        out_specs=[z_spec, pl.BlockSpec((n_batch, tc_cols), lambda ci, ti: (0, ci))],
        out_shape=[z_shape, jax.ShapeDtypeStruct((n_batch, c), F32)],
        scratch_shapes=[pltpu.VMEM((n_batch, tc_cols), F32), pltpu.VMEM((n_tail, tc_cols), F32)],
        compiler_params=_params(("arbitrary", "arbitrary")),
        name="lru",
    )(gx, gx, conv_state, h0, conv_w, conv_b, w_gates, b_gates, lam)
    return z, ht


def _rec_front_body(*refs, n_batch, side_gains):
    x_ref, norm_gain_ref, win_ref, cw_ref, cb_ref, wg_ref, bg_ref, lam_ref = refs[:8]
    pos = 8
    side_in = []
    for has_gain in side_gains:
        side_in.append((refs[pos], refs[pos + 1] if has_gain else None))
        pos += 2 if has_gain else 1
    z_ref, ht_ref, tail_ref = refs[pos:pos + 3]
    pos += 3
    side_out = refs[pos:pos + len(side_gains)]
    gx_sc, h_sc, tail_sc = refs[pos + len(side_gains):]
    s = pl.program_id(0)
    c = D_RNN

    @pl.when(s <= 1)
    def _():
        h_sc[...] = jnp.zeros_like(h_sc)
        tail_sc[...] = jnp.zeros_like(tail_sc)

    @pl.when(s == 0)
    def _():
        gx_sc[1] = jnp.zeros(gx_sc.shape[1:], gx_sc.dtype)

    for (src_ref, g_ref), dst_ref in zip(side_in, side_out):
        chunk = src_ref[...]
        if g_ref is not None:
            chunk = chunk * g_ref[...]
        dst_ref[...] = chunk.astype(dst_ref.dtype)

    x = x_ref[...].reshape(n_batch * PERM_STEPS, x_ref.shape[-1])
    y = x * lax.rsqrt(jnp.mean(x * x, axis=-1, keepdims=True) + EPS)
    xn = jnp.dot(_perm_matrix(n_batch, True), (y * norm_gain_ref[...]).astype(BF16),
                 preferred_element_type=F32).astype(BF16)
    slot = s % 2
    gx_sc[slot] = jnp.dot(xn, win_ref[...], preferred_element_type=F32)

    prev = 1 - slot
    perm = _perm_matrix(n_batch, False)
    decay = _lru_decay(lam_ref[...])
    for n in range(N_LRU_BLOCKS):
        cols = slice(n * LRU_BLOCK, (n + 1) * LRU_BLOCK)
        z, tail_sc[:, cols], h_sc[:, cols] = _lru_columns(
            gx_sc[prev, :, n * LRU_BLOCK:(n + 1) * LRU_BLOCK], gx_sc[prev, :, c + n * LRU_BLOCK:c + (n + 1) * LRU_BLOCK],
            tail_sc[:, cols], h_sc[:, cols], cw_ref[:, cols], cb_ref[:, cols], wg_ref[n], bg_ref[n], decay[:, cols],
            n_steps=PERM_STEPS, n_batch=n_batch)
        zp = jnp.dot(perm, z.astype(BF16), preferred_element_type=F32)
        z_ref[:, :, cols] = zp.astype(z_ref.dtype).reshape(n_batch, PERM_STEPS, LRU_BLOCK)

    @pl.when(s == pl.num_programs(0) - 1)
    def _():
        ht_ref[...] = h_sc[...]
        tail_ref[...] = tail_sc[...]


def _rec_front(x3, gain, w_in, conv_w, conv_b, w_gates, b_gates, lam, *, side_casts=()):
    n_batch, t, d = x3.shape
    c = D_RNN
    n_rows = n_batch * PERM_STEPS
    n_blocks = t // PERM_STEPS
    n_tail = (CONV_W - 1) * n_batch
    blk = lambda s: jnp.minimum(s, n_blocks - 1)
    const = lambda *shape: pl.BlockSpec(shape, lambda s: (0,) * len(shape))

    in_specs = [pl.BlockSpec((n_batch, PERM_STEPS, d), lambda s: (0, blk(s), 0)), const(1, d),
                pl.BlockSpec((d, 2 * c), lambda s: (0, 0), pipeline_mode=pl.Buffered(1)),
                const(CONV_W, c), const(1, c), const(N_LRU_BLOCKS, LRU_BLOCK, 2 * LRU_BLOCK),
                const(N_LRU_BLOCKS, 1, 2 * LRU_BLOCK), const(1, c)]
    args = [x3, gain.reshape(1, d).astype(F32), w_in, conv_w, conv_b, w_gates, b_gates, lam]
    out_specs = [pl.BlockSpec((n_batch, PERM_STEPS, c), lambda s: (0, jnp.maximum(s - 1, 0), 0)),
                 const(n_batch, c), const(n_tail, c)]
    out_shape = [jax.ShapeDtypeStruct((n_batch, t, c), BF16),
                 jax.ShapeDtypeStruct((n_batch, c), F32), jax.ShapeDtypeStruct((n_tail, c), F32)]
    for src, gain in side_casts:
        rows, cols = src.shape
        chunk = rows // n_blocks
        assert chunk * n_blocks == rows and chunk % (2 * SUBLANES) == 0, (rows, n_blocks)
        in_specs.append(pl.BlockSpec((chunk, cols), lambda s: (blk(s), 0)))
        args.append(src)
        if gain is not None:
            in_specs.append(pl.BlockSpec((chunk, 1), lambda s: (blk(s), 0)))
            args.append(gain.reshape(rows, 1).astype(F32))
        out_specs.append(pl.BlockSpec((chunk, cols), lambda s: (blk(s), 0)))
        out_shape.append(jax.ShapeDtypeStruct((rows, cols), BF16))

    return pl.pallas_call(
        functools.partial(_rec_front_body, n_batch=n_batch, side_gains=tuple(g is not None for _, g in side_casts)),
        grid=(n_blocks + 1,),
        in_specs=in_specs,
        out_specs=out_specs,
        out_shape=out_shape,
        scratch_shapes=[pltpu.VMEM((2, n_rows, 2 * c), F32), pltpu.VMEM((n_batch, c), F32),
                        pltpu.VMEM((n_tail, c), F32)],
        compiler_params=_params(("arbitrary",)),
        name="rec_front",
    )(*args)


def _attn_prompt_body(sink_ref, q_ref, kvc_ref, kvp_ref, o_ref):
    i = pl.program_id(1)
    kv = jnp.concatenate([kvp_ref[...], kvc_ref[...]], axis=0).astype(BF16)
    n_keys = 2 * WINDOW
    low_k = lax.broadcasted_iota(jnp.int32, (n_keys, LANES), 1) < HEAD_DIM
    low_q = lax.broadcasted_iota(jnp.int32, (WINDOW, LANES), 1) < HEAD_DIM
    zero = jnp.zeros((n_keys, LANES), BF16)
    one = jnp.ones((n_keys, LANES), BF16)
    from_prev = (lax.broadcasted_iota(jnp.int32, (WINDOW, WINDOW), 1)
                 > lax.broadcasted_iota(jnp.int32, (WINDOW, WINDOW), 0))
    prev_bias = jnp.where(i == 0, -jnp.inf, 0.0)
    dims = (((1,), (1,)), ((), ()))

    for h in range(N_KV_HEADS):
        c0 = (h // 2) * LANES
        k_t = kv[:, c0:c0 + LANES]
        v_t = kv[:, KV_DIM + c0:KV_DIM + c0 + LANES]
        k_sw = pltpu.roll(k_t, HEAD_DIM, 1)
        v_sw = pltpu.roll(v_t, HEAD_DIM, 1)
        k_lo, k_hi = (k_t, k_sw) if h % 2 == 0 else (k_sw, k_t)
        v_lo, v_hi = (v_t, v_sw) if h % 2 == 0 else (v_sw, v_t)
        k_a = jnp.where(low_k, k_lo, zero)
        k_b = jnp.where(low_k, zero, k_hi)
        v_a = jnp.where(low_k, v_lo, one)
        v_b = jnp.where(low_k, one, v_hi)
        n_pairs = GROUP // 2
        qc0 = h * GROUP * HEAD_DIM
        q_h = jnp.concatenate([q_ref[:, qc0 + p * LANES:qc0 + (p + 1) * LANES] for p in range(n_pairs)], axis=0)
        acc, esink = [], []
        for side, (k_x, v_x) in enumerate(((k_a, v_a), (k_b, v_b))):
            s = lax.dot_general(q_h, k_x, dims, preferred_element_type=F32)
            probs, es = [], []
            for p in range(n_pairs):
                s_p = s[p * WINDOW:(p + 1) * WINDOW]
                s_v = jnp.where(from_prev, s_p[:, :WINDOW] + prev_bias, s_p[:, WINDOW:])
                sink = sink_ref[h * GROUP + 2 * p + side] * LOG2E
                m = jnp.maximum(jnp.max(s_v, axis=-1, keepdims=True), sink)
                e = jnp.exp2(s_v - m)
                probs.append(jnp.concatenate([jnp.where(from_prev, e, 0.0), jnp.where(from_prev, 0.0, e)],
                                             axis=1).astype(BF16))
                es.append(jnp.exp2(sink - m))
            acc.append(jnp.dot(jnp.concatenate(probs, axis=0), v_x, preferred_element_type=F32))
            esink.append(es)
        for p in range(n_pairs):
            rows = slice(p * WINDOW, (p + 1) * WINDOW)
            acc_a, acc_b = acc[0][rows], acc[1][rows]
            num = jnp.where(low_q, acc_a, acc_b)
            den = pltpu.roll(jnp.where(low_q, acc_b, acc_a), HEAD_DIM, 1) + jnp.where(low_q, esink[0][p], esink[1][p])
            o_ref[:, qc0 + p * LANES:qc0 + (p + 1) * LANES] = (num / den).astype(o_ref.dtype)


def _attn_prompt(q3, kv3, sinks):
    b, t, _ = q3.shape
    nb = t // WINDOW
    return pl.pallas_call(
        _attn_prompt_body,
        grid=(b, nb),
        in_specs=[
            pl.BlockSpec(memory_space=pltpu.SMEM),
            pl.BlockSpec((None, WINDOW, Q_DIM), lambda bi, i: (bi, i, 0)),
            pl.BlockSpec((None, WINDOW, 2 * KV_DIM), lambda bi, i: (bi, i, 0)),
            pl.BlockSpec((None, WINDOW, 2 * KV_DIM), lambda bi, i: (bi, jnp.maximum(i - 1, 0), 0)),
        ],
        out_specs=pl.BlockSpec((None, WINDOW, Q_DIM), lambda bi, i: (bi, i, 0)),
        out_shape=jax.ShapeDtypeStruct((b, t, Q_DIM), BF16),
        compiler_params=_params(("arbitrary", "arbitrary")),
        name="attn_prompt",
    )(sinks, q3, kv3, kv3)


SAMPLE_BATCH_TILE = 8


def _attn_sample_body(sink_ref, q_ref, kvn_ref, ck_ref, cv_ref, o_ref, *, n_steps):
    bb = SAMPLE_BATCH_TILE
    n_new = n_steps * bb
    n_cache = bb * WINDOW
    n_keys = n_cache + n_new
    n_keys_pad = -(-n_keys // LANES) * LANES
    q = q_ref[...].reshape(n_new, Q_DIM)
    kvn = kvn_ref[...].reshape(n_new, 2 * KV_DIM)
    pad = jnp.zeros((n_keys_pad - n_keys, KV_DIM), F32)
    k_all = jnp.concatenate([ck_ref[...].reshape(n_cache, KV_DIM), kvn[:, :KV_DIM], pad], axis=0).astype(BF16)
    v_all = jnp.concatenate([cv_ref[...].reshape(n_cache, KV_DIM), kvn[:, KV_DIM:], pad], axis=0).astype(BF16)
    n_rows = GROUP * n_new
    r = lax.broadcasted_iota(jnp.int32, (n_rows, n_keys_pad), 0)
    c = lax.broadcasted_iota(jnp.int32, (n_rows, n_keys_pad), 1)
    b_r = r % bb
    t_r = (r // bb) % n_steps
    is_cache = c < n_cache
    c2 = jnp.maximum(c - n_cache, 0)
    b_c = jnp.where(is_cache, c // WINDOW, c2 % bb)
    k_step = jnp.where(is_cache, c % WINDOW - WINDOW, c2 // bb)
    dq = t_r - k_step
    allowed = jnp.logical_and(jnp.logical_and(b_c == b_r, c < n_keys),
                              jnp.logical_and(dq >= 0, dq < WINDOW))
    grp = lax.broadcasted_iota(jnp.int32, (n_rows, 1), 0) // n_new
    dims = (((1,), (1,)), ((), ()))
    for h in range(N_KV_HEADS):
        k_h = k_all[:, h * HEAD_DIM:(h + 1) * HEAD_DIM]
        v_h = v_all[:, h * HEAD_DIM:(h + 1) * HEAD_DIM]
        q_h = jnp.concatenate(
            [q[:, (h * GROUP + g) * HEAD_DIM:(h * GROUP + g + 1) * HEAD_DIM] for g in range(GROUP)], axis=0)
        sink = jnp.zeros((n_rows, 1), F32)
        for g in range(GROUP):
            sink = jnp.where(grp == g, sink_ref[h * GROUP + g] * LOG2E, sink)
        s = lax.dot_general(q_h, k_h, dims, preferred_element_type=F32)
        s = jnp.where(allowed, s, -jnp.inf)
        m = jnp.maximum(jnp.max(s, axis=-1, keepdims=True), sink)
        p = jnp.exp2(s - m)
        denom = jnp.sum(p, axis=-1, keepdims=True) + jnp.exp2(sink - m)
        o_h = jnp.dot(p.astype(BF16), v_h, preferred_element_type=F32) / denom
        for g in range(GROUP):
            c0 = (h * GROUP + g) * HEAD_DIM
            o_ref[:, :, c0:c0 + HEAD_DIM] = (
                o_h[g * n_new:(g + 1) * n_new].reshape(n_steps, bb, HEAD_DIM).astype(o_ref.dtype))


def _attn_sample(q3, kvn3, cache_k, cache_v, sinks):
    n_steps, b, _ = q3.shape
    bb = SAMPLE_BATCH_TILE
    return pl.pallas_call(
        functools.partial(_attn_sample_body, n_steps=n_steps),
        grid=(b // bb,),
        in_specs=[
            pl.BlockSpec(memory_space=pltpu.SMEM),
            pl.BlockSpec((n_steps, bb, Q_DIM), lambda i: (0, i, 0)),
            pl.BlockSpec((n_steps, bb, 2 * KV_DIM), lambda i: (0, i, 0)),
            pl.BlockSpec((bb, WINDOW, KV_DIM), lambda i: (i, 0, 0)),
            pl.BlockSpec((bb, WINDOW, KV_DIM), lambda i: (i, 0, 0)),
        ],
        out_specs=pl.BlockSpec((n_steps, bb, Q_DIM), lambda i: (0, i, 0)),
        out_shape=jax.ShapeDtypeStruct((n_steps, b, Q_DIM), BF16),
        compiler_params=_params(("arbitrary",)),
        name="attn_sample",
    )(sinks, q3, kvn3, cache_k, cache_v)


TILES = {
    "rec_in": dict(bm=512, bn=2048),
    "rec_out": dict(bm=512, bn=2048),
    "attn_out": dict(bm=512, bn=2048),
    "mlp_up": dict(bm=1024, bn=2048),
    "mlp_down": dict(bm=1024, bn=1024, bk=2048),
}


def _mm(name, x, w, **kw):
    return _matmul(x, w, name=name, **TILES[name], **kw)


def _flat(w):
    return w.reshape(-1, w.shape[-1])


def _trunk(z, x, attn_fn, raw, w):
    h, hb, ssq = _mm("rec_out", z, w["rec_w_out"], out_dtype=F32, resid=x, bf16_copy=True, ssq_out=True)
    if "mlp_w_down" not in w:
        hid, w_down, w_kv, w_q, w_o = _mm("mlp_up", hb, w["mlp_w_up"], layer=0, out_dtype=BF16, act="relu2",
                                          ssq_in=ssq, side_casts=[
            (_flat(raw["mlp_w_down"]), None), (raw["w_kv"], raw["kv_norm"]),
            (_flat(raw["attn_w_q"]), raw["norm_mix"][1]), (_flat(raw["attn_w_o"]), None)])
        w["mlp_w_down"] = w_down.reshape(raw["mlp_w_down"].shape)
        w["w_kv"] = w_kv
        w["attn_w_q"] = w_q
        w["attn_w_o"] = w_o.reshape(raw["attn_w_o"].shape)
    else:
        hid = _mm("mlp_up", hb, w["mlp_w_up"], layer=0, out_dtype=BF16, act="relu2", ssq_in=ssq)
    h, hb, ssq = _mm("mlp_down", hid, w["mlp_w_down"], layer=0, out_dtype=F32, resid=h, bf16_copy=True, ssq_out=True)
    q, kv = _qkv_proj(hb, ssq, w["attn_w_q"], w["w_kv"])
    o = attn_fn(q, kv)
    h, hb, ssq = _mm("attn_out", o, w["attn_w_o"], out_dtype=F32, resid=h, bf16_copy=True, ssq_out=True)
    hid = _mm("mlp_up", hb, w["mlp_w_up"], layer=1, out_dtype=BF16, act="relu2", ssq_in=ssq)
    h = _mm("mlp_down", hid, w["mlp_w_down"], layer=1, out_dtype=F32, resid=h)
    return _rmsnorm(h, raw["final_norm"], F32), kv


def kernel(x_prompt, x_sample, state_conv, state_h, cache_k, cache_v, norm_mix, norm_mlp, rec_w_in,
           rec_conv_w, rec_conv_b, rec_gate_a_w, rec_gate_a_b, rec_gate_x_w, rec_gate_x_b, rec_lambda,
           rec_w_out, kv_norm, w_kv, attn_w_q, attn_sinks, attn_w_o, mlp_w_up, mlp_w_down, final_norm):
    b, t, d = x_prompt.shape
    sb, st, _ = x_sample.shape
    c = D_RNN
    n_tail = CONV_W - 1

    raw = {
        "norm_mix": norm_mix, "kv_norm": kv_norm, "final_norm": final_norm,
        "w_kv": w_kv, "attn_w_q": attn_w_q, "attn_w_o": attn_w_o, "mlp_w_down": mlp_w_down,
    }
    w = {"rec_w_in": rec_w_in.astype(BF16)}
    conv_w = rec_conv_w[0]
    conv_b = rec_conv_b[0].reshape(1, c)
    lam = rec_lambda[0].reshape(1, c)
    w_gates = jnp.concatenate([rec_gate_a_w[0], rec_gate_x_w[0]], axis=-1).astype(BF16)
    b_gates = jnp.concatenate([rec_gate_a_b[0], rec_gate_x_b[0]], axis=-1).reshape(N_LRU_BLOCKS, 1, 2 * LRU_BLOCK)
    sinks = attn_sinks[0].astype(F32)

    def attn_prompt(q, kv):
        o = _attn_prompt(q.reshape(b, t, Q_DIM), kv.reshape(b, t, 2 * KV_DIM), sinks)
        return o.reshape(b * t, Q_DIM)

    z_p, h_p, tail_p, w_out, w_up = _rec_front(
        x_prompt, norm_mix[0], w["rec_w_in"][0], conv_w, conv_b, w_gates, b_gates, lam,
        side_casts=[(_flat(rec_w_out), None), (_flat(mlp_w_up), norm_mlp.reshape(-1))])
    w["rec_w_out"] = w_out.reshape(rec_w_out.shape)
    w["mlp_w_up"] = w_up.reshape(mlp_w_up.shape)
    y_p, kv_p = _trunk(z_p.reshape(b * t, c), x_prompt.reshape(b * t, d), attn_prompt, raw, w)
    y_prompt = y_p.reshape(b, t, d)
    conv_p = jnp.swapaxes(tail_p.reshape(n_tail, b, c), 0, 1).reshape(b, 1, n_tail, c)
    keep = min(WINDOW, t)
    kv_tail = kv_p.reshape(b, t, 2 * KV_DIM)[:, t - keep:]
    new_k_prompt = kv_tail[..., :KV_DIM].reshape(b, keep, N_KV_HEADS, HEAD_DIM)
    new_v_prompt = kv_tail[..., KV_DIM:].reshape(b, keep, N_KV_HEADS, HEAD_DIM)

    conv_state_tb = jnp.swapaxes(state_conv[:, 0], 0, 1).reshape(n_tail * sb, c)
    h0 = state_h[:, 0]
    ck = cache_k.reshape(sb, WINDOW, KV_DIM)
    cv = cache_v.reshape(sb, WINDOW, KV_DIM)

    def attn_sample(q, kv):
        o = _attn_sample(q.reshape(st, sb, Q_DIM), kv.reshape(st, sb, 2 * KV_DIM), ck, cv, sinks)
        return o.reshape(st * sb, Q_DIM)

    x_s = jnp.swapaxes(x_sample, 0, 1).reshape(st * sb, d)
    gx_s = _mm("rec_in", _rmsnorm(x_s, norm_mix[0], BF16), w["rec_w_in"], out_dtype=F32)
    z_s, h_s = _lru(gx_s, conv_state_tb, h0, conv_w, conv_b, w_gates, b_gates, lam, n_batch=sb, n_steps=st)
    y_s, kv_s = _trunk(z_s, x_s, attn_sample, raw, w)
    y_sample = jnp.swapaxes(y_s.reshape(st, sb, d), 0, 1)
    conv_s = jnp.swapaxes(gx_s.reshape(st, sb, 2 * c)[st - n_tail:, :, c:], 0, 1).reshape(sb, 1, n_tail, c)
    kv_s = jnp.swapaxes(kv_s.reshape(st, sb, 2 * KV_DIM), 0, 1)
    new_k_sample = kv_s[..., :KV_DIM].reshape(sb, st, N_KV_HEADS, HEAD_DIM)
    new_v_sample = kv_s[..., KV_DIM:].reshape(sb, st, N_KV_HEADS, HEAD_DIM)

    return (y_prompt, y_sample, conv_p, h_p.reshape(b, 1, c), new_k_prompt, new_v_prompt,
            conv_s, h_s.reshape(sb, 1, c), new_k_sample, new_v_sample)
```

```python
import functools
import math

import jax
import jax.numpy as jnp
from jax import lax
from jax.experimental import pallas as pl
from jax.experimental.pallas import tpu as pltpu

F32 = jnp.float32
BF16 = jnp.bfloat16

D_MODEL = 2048
D_RNN = 2048
N_LRU_BLOCKS = 16
LRU_BLOCK = D_RNN // N_LRU_BLOCKS
CONV_W = 4
LRU_C = 8.0
N_HEADS = 32
N_KV_HEADS = 4
HEAD_DIM = 64
GROUP = N_HEADS // N_KV_HEADS
Q_DIM = N_HEADS * HEAD_DIM
KV_DIM = N_KV_HEADS * HEAD_DIM
WINDOW = 128
EPS = 1e-6
LOG2E = math.log2(math.e)

SUBLANES = 8
LANES = 128
VMEM_LIMIT_BYTES = 52 * 1024 * 1024


def _params(semantics):
    return pltpu.CompilerParams(dimension_semantics=semantics,
                                vmem_limit_bytes=VMEM_LIMIT_BYTES)


def _rmsnorm_body(x_ref, g_ref, o_ref):
    x = x_ref[...]
    y = x * lax.rsqrt(jnp.mean(x * x, axis=-1, keepdims=True) + EPS)
    o_ref[...] = (y * g_ref[...]).astype(o_ref.dtype)


def _rmsnorm(x, gain, out_dtype, tm=512):
    m, d = x.shape
    tm = min(tm, m)
    row_spec = pl.BlockSpec((tm, d), lambda i: (i, 0))
    return pl.pallas_call(
        _rmsnorm_body,
        grid=(m // tm,),
        in_specs=[row_spec, pl.BlockSpec((1, d), lambda i: (0, 0))],
        out_specs=row_spec,
        out_shape=jax.ShapeDtypeStruct((m, d), out_dtype),
        compiler_params=_params(("arbitrary",)),
        name="rmsnorm",
    )(x, gain.reshape(1, d).astype(F32))


PERM_STEPS = 32


def _perm_matrix(n_batch, to_time_major):
    n = n_batch * PERM_STEPS
    r = lax.broadcasted_iota(jnp.int32, (n, n), 0)
    c = lax.broadcasted_iota(jnp.int32, (n, n), 1)
    if to_time_major:
        src = (r % n_batch) * PERM_STEPS + r // n_batch
    else:
        src = (r % PERM_STEPS) * n_batch + r // PERM_STEPS
    return jnp.where(c == src, 1.0, 0.0).astype(BF16)


def _matmul_body(*refs, nk, act, has_scale, has_resid, has_copy, has_ssq, side_gains, norm_dim):
    refs = list(refs)
    x_ref, w_ref = refs[:2]
    pos = 2
    s_ref = r_ref = b_ref = q_ref = None
    if has_scale:
        s_ref = refs[pos]
        pos += 1
    if has_resid:
        r_ref = refs[pos]
        pos += 1
    side_in = []
    for has_gain in side_gains:
        side_in.append((refs[pos], refs[pos + 1] if has_gain else None))
        pos += 2 if has_gain else 1
    o_ref = refs[pos]
    pos += 1
    if has_copy:
        b_ref = refs[pos]
        pos += 1
    if has_ssq:
        q_ref = refs[pos]
        pos += 1
    side_out = refs[pos:pos + len(side_gains)]
    pos += len(side_gains)
    acc_ref = refs[pos] if nk > 1 else None

    def partial_product():
        for (src_ref, g_ref), dst_ref in zip(side_in, side_out):
            chunk = src_ref[...]
            if g_ref is not None:
                chunk = chunk * g_ref[...]
            dst_ref[...] = chunk.astype(dst_ref.dtype)
        return jnp.dot(x_ref[...], w_ref[...], preferred_element_type=F32)

    def finish(acc):
        if s_ref is not None:
            acc = acc * lax.rsqrt(jnp.sum(s_ref[...], axis=-1, keepdims=True) * (1.0 / norm_dim) + EPS)
        if act == "relu2":
            r = jnp.maximum(acc, 0.0)
            acc = r * r
        if r_ref is not None:
            acc = r_ref[...] + acc
        o_ref[...] = acc.astype(o_ref.dtype)
        if b_ref is not None:
            b_ref[...] = acc.astype(b_ref.dtype)
        if q_ref is not None:
            sq = acc * acc
            tot = sq[:, 0:LANES]
            for c in range(1, sq.shape[1] // LANES):
                tot = tot + sq[:, c * LANES:(c + 1) * LANES]
            q_ref[...] = tot

    if nk == 1:
        finish(partial_product())
    else:
        k = pl.program_id(2)

        @pl.when(k == 0)
        def _():
            acc_ref[...] = partial_product()

        @pl.when(jnp.logical_and(k > 0, k < nk - 1))
        def _():
            acc_ref[...] += partial_product()

        @pl.when(k == nk - 1)
        def _():
            finish(acc_ref[...] + partial_product())


def _matmul(x, w, *, out_dtype, layer=0, act=None, ssq_in=None, resid=None, bf16_copy=False, ssq_out=False,
            side_casts=(), cols_outer=False, bm=1024, bn=1024, bk=None, name="matmul"):
    m = x.shape[0]
    _, kdim, n = w.shape
    bm, bn = min(bm, m), min(bn, n)
    bk = kdim if bk is None else min(bk, kdim)
    nk, nb = kdim // bk, n // bn
    mb = m // bm
    n_steps = mb * nb * nk

    def spec(shape, index_map):
        if cols_outer:
            return pl.BlockSpec(shape, lambda j, i, k: index_map(i, j, k))
        return pl.BlockSpec(shape, index_map)

    in_specs = [spec((bm, bk), lambda i, j, k: (i, k)),
                spec((None, bk, bn), lambda i, j, k: (layer, k, j))]
    args = [x, w]
    if ssq_in is not None:
        in_specs.append(spec((bm, ssq_in.shape[1]), lambda i, j, k: (i, 0)))
        args.append(ssq_in)
    tile_spec = spec((bm, bn), lambda i, j, k: (i, j))
    if resid is not None:
        in_specs.append(tile_spec)
        args.append(resid)
    step = lambda i, j, k: (i * nb + j) * nk + k
    for src, gain in side_casts:
        rows, cols = src.shape
        chunk = rows // n_steps
        assert chunk * n_steps == rows and chunk % (2 * SUBLANES) == 0, (name, rows, n_steps)
        in_specs.append(spec((chunk, cols), lambda i, j, k: (step(i, j, k), 0)))
        args.append(src)
        if gain is not None:
            in_specs.append(spec((chunk, 1), lambda i, j, k: (step(i, j, k), 0)))
            args.append(gain.reshape(rows, 1).astype(F32))

    out_specs = [tile_spec]
    out_shape = [jax.ShapeDtypeStruct((m, n), out_dtype)]
    if bf16_copy:
        out_specs.append(tile_spec)
        out_shape.append(jax.ShapeDtypeStruct((m, n), BF16))
    if ssq_out:
        out_specs.append(spec((bm, LANES), lambda i, j, k: (i, j)))
        out_shape.append(jax.ShapeDtypeStruct((m, nb * LANES), F32))
    for src, _ in side_casts:
        rows, cols = src.shape
        out_specs.append(spec((rows // n_steps, cols), lambda i, j, k: (step(i, j, k), 0)))
        out_shape.append(jax.ShapeDtypeStruct((rows, cols), BF16))

    outs = pl.pallas_call(
        functools.partial(_matmul_body, nk=nk, act=act, has_scale=ssq_in is not None,
                          has_resid=resid is not None, has_copy=bf16_copy, has_ssq=ssq_out,
                          side_gains=tuple(g is not None for _, g in side_casts), norm_dim=kdim),
        grid=(nb, mb, nk) if cols_outer else (mb, nb, nk),
        in_specs=in_specs,
        out_specs=out_specs,
        out_shape=out_shape,
        scratch_shapes=[pltpu.VMEM((bm, bn), F32)] if nk > 1 else [],
        compiler_params=_params(("arbitrary", "arbitrary", "arbitrary")),
        name=name,
    )(*args)
    return outs[0] if len(outs) == 1 else outs


def _qkv_body(x_ref, wq_ref, wkv_ref, s_ref, q_ref, kv_ref):
    x = x_ref[...]
    scale = lax.rsqrt(jnp.sum(s_ref[...], axis=-1, keepdims=True) * (1.0 / x.shape[1]) + EPS)
    q = jnp.dot(x, wq_ref[...], preferred_element_type=F32)
    q_ref[...] = (q * (scale * (HEAD_DIM ** -0.5 * LOG2E))).astype(q_ref.dtype)
    kv_ref[...] = jnp.dot(x, wkv_ref[...], preferred_element_type=F32) * scale


def _qkv_proj(x, ssq, w_q, w_kv, bm=1024):
    m, kdim = x.shape
    bm = min(bm, m)
    return pl.pallas_call(
        _qkv_body,
        grid=(m // bm,),
        in_specs=[pl.BlockSpec((bm, kdim), lambda i: (i, 0)),
                  pl.BlockSpec((kdim, Q_DIM), lambda i: (0, 0)),
                  pl.BlockSpec((kdim, 2 * KV_DIM), lambda i: (0, 0)),
                  pl.BlockSpec((bm, ssq.shape[1]), lambda i: (i, 0))],
        out_specs=[pl.BlockSpec((bm, Q_DIM), lambda i: (i, 0)),
                   pl.BlockSpec((bm, 2 * KV_DIM), lambda i: (i, 0))],
        out_shape=[jax.ShapeDtypeStruct((m, Q_DIM), BF16), jax.ShapeDtypeStruct((m, 2 * KV_DIM), F32)],
        compiler_params=_params(("arbitrary",)),
        name="qkv_proj",
    )(x, w_q, w_kv, ssq)


def _gelu_tanh(x):
    return x * (0.5 + 0.5 * jnp.tanh(0.7978845608028654 * (x + 0.044715 * (x * x * x))))


def _softplus(z):
    return jnp.maximum(z, 0.0) + jnp.log1p(jnp.exp(-jnp.abs(z)))


def _lru_decay(lam):
    return (-0.5 * LRU_C * LOG2E) * _softplus(-lam)


def _lru_columns(gate, x, tail, h, cw, cb, wg, bg, d, *, n_steps, n_batch):
    n_rows = n_steps * n_batch
    x_ext = jnp.concatenate([tail, x], axis=0)
    xc = cb + cw[0:1] * x_ext[0:n_rows]
    for j in range(1, CONV_W):
        xc = xc + cw[j:j + 1] * x_ext[j * n_batch:j * n_batch + n_rows]
    half_pre = 0.5 * (jnp.dot(xc.astype(BF16), wg, preferred_element_type=F32) + bg)
    tr = jnp.tanh(half_pre[:, :LRU_BLOCK])
    ti = jnp.tanh(half_pre[:, LRU_BLOCK:])
    a = jnp.exp2(d * tr + d)
    gain2 = 1.0 - a * a
    u = (gain2 * lax.rsqrt(jnp.maximum(gain2, 1e-30))) * ((0.5 + 0.5 * ti) * xc)
    hs = []
    for s in range(n_steps):
        rows = slice(s * n_batch, (s + 1) * n_batch)
        h = a[rows] * h + u[rows]
        hs.append(h)
    return _gelu_tanh(gate) * jnp.concatenate(hs, axis=0), x_ext[n_rows:], h


def _lru_body(gate_ref, xr_ref, cs_ref, h0_ref, cw_ref, cb_ref, wg_ref, bg_ref, lam_ref,
              z_ref, ht_ref, h_sc, tail_sc, *, n_steps, n_batch, tc_cols):
    t = pl.program_id(1)

    @pl.when(t == 0)
    def _():
        h_sc[...] = h0_ref[...]
        tail_sc[...] = cs_ref[...]

    decay = _lru_decay(lam_ref[...])
    for n in range(tc_cols // LRU_BLOCK):
        cols = slice(n * LRU_BLOCK, (n + 1) * LRU_BLOCK)
        z, tail_sc[:, cols], h_sc[:, cols] = _lru_columns(
            gate_ref[:, cols], xr_ref[:, cols], tail_sc[:, cols], h_sc[:, cols], cw_ref[:, cols], cb_ref[:, cols],
            wg_ref[n], bg_ref[n], decay[:, cols], n_steps=n_steps, n_batch=n_batch)
        z_ref[:, cols] = z.astype(z_ref.dtype)

    @pl.when(t == pl.num_programs(1) - 1)
    def _():
        ht_ref[...] = h_sc[...]


def _lru(gx, conv_state, h0, conv_w, conv_b, w_gates, b_gates, lam, *, n_batch, n_steps, tc_cols=512):
    m = gx.shape[0]
    c = D_RNN
    n_rows = n_steps * n_batch
    n_tail = (CONV_W - 1) * n_batch
    ncb = c // tc_cols
    gpb = tc_cols // LRU_BLOCK
    z_spec = pl.BlockSpec((n_rows, tc_cols), lambda ci, ti: (ti, ci))
    z_shape = jax.ShapeDtypeStruct((m, c), BF16)
    z, ht = pl.pallas_call(
        functools.partial(_lru_body, n_steps=n_steps, n_batch=n_batch, tc_cols=tc_cols),
        grid=(ncb, m // n_rows),
        in_specs=[
            pl.BlockSpec((n_rows, tc_cols), lambda ci, ti: (ti, ci)),
            pl.BlockSpec((n_rows, tc_cols), lambda ci, ti: (ti, ncb + ci)),
            pl.BlockSpec((n_tail, tc_cols), lambda ci, ti: (0, ci)),
            pl.BlockSpec((n_batch, tc_cols), lambda ci, ti: (0, ci)),
            pl.BlockSpec((CONV_W, tc_cols), lambda ci, ti: (0, ci)),
            pl.BlockSpec((1, tc_cols), lambda ci, ti: (0, ci)),
            pl.BlockSpec((gpb, LRU_BLOCK, 2 * LRU_BLOCK), lambda ci, ti: (ci, 0, 0)),
            pl.BlockSpec((gpb, 1, 2 * LRU_BLOCK), lambda ci, ti: (ci, 0, 0)),
            pl.BlockSpec((1, tc_cols), lambda ci, ti: (0, ci)),
        ],
        out_specs=[z_spec, pl.BlockSpec((n_batch, tc_cols), lambda ci, ti: (0, ci))],
        out_shape=[z_shape, jax.ShapeDtypeStruct((n_batch, c), F32)],
        scratch_shapes=[pltpu.VMEM((n_batch, tc_cols), F32), pltpu.VMEM((n_tail, tc_cols), F32)],
        compiler_params=_params(("arbitrary", "arbitrary")),
        name="lru",
    )(gx, gx, conv_state, h0, conv_w, conv_b, w_gates, b_gates, lam)
    return z, ht


def _rec_front_body(*refs, n_batch, side_gains):
    x_ref, norm_gain_ref, win_ref, cw_ref, cb_ref, wg_ref, bg_ref, lam_ref = refs[:8]
    pos = 8
    side_in = []
    for has_gain in side_gains:
        side_in.append((refs[pos], refs[pos + 1] if has_gain else None))
        pos += 2 if has_gain else 1
    z_ref, ht_ref, tail_ref = refs[pos:pos + 3]
    pos += 3
    side_out = refs[pos:pos + len(side_gains)]
    gx_sc, h_sc, tail_sc = refs[pos + len(side_gains):]
    s = pl.program_id(0)
    c = D_RNN

    @pl.when(s <= 1)
    def _():
        h_sc[...] = jnp.zeros_like(h_sc)
        tail_sc[...] = jnp.zeros_like(tail_sc)

    @pl.when(s == 0)
    def _():
        gx_sc[1] = jnp.zeros(gx_sc.shape[1:], gx_sc.dtype)

    for (src_ref, g_ref), dst_ref in zip(side_in, side_out):
        chunk = src_ref[...]
        if g_ref is not None:
            chunk = chunk * g_ref[...]
        dst_ref[...] = chunk.astype(dst_ref.dtype)

    x = x_ref[...].reshape(n_batch * PERM_STEPS, x_ref.shape[-1])
    y = x * lax.rsqrt(jnp.mean(x * x, axis=-1, keepdims=True) + EPS)
    xn = jnp.dot(_perm_matrix(n_batch, True), (y * norm_gain_ref[...]).astype(BF16),
                 preferred_element_type=F32).astype(BF16)
    slot = s % 2
    gx_sc[slot] = jnp.dot(xn, win_ref[...], preferred_element_type=F32)

    prev = 1 - slot
    perm = _perm_matrix(n_batch, False)
    decay = _lru_decay(lam_ref[...])
    for n in range(N_LRU_BLOCKS):
        cols = slice(n * LRU_BLOCK, (n + 1) * LRU_BLOCK)
        z, tail_sc[:, cols], h_sc[:, cols] = _lru_columns(
            gx_sc[prev, :, n * LRU_BLOCK:(n + 1) * LRU_BLOCK], gx_sc[prev, :, c + n * LRU_BLOCK:c + (n + 1) * LRU_BLOCK],
            tail_sc[:, cols], h_sc[:, cols], cw_ref[:, cols], cb_ref[:, cols], wg_ref[n], bg_ref[n], decay[:, cols],
            n_steps=PERM_STEPS, n_batch=n_batch)
        zp = jnp.dot(perm, z.astype(BF16), preferred_element_type=F32)
        z_ref[:, :, cols] = zp.astype(z_ref.dtype).reshape(n_batch, PERM_STEPS, LRU_BLOCK)

    @pl.when(s == pl.num_programs(0) - 1)
    def _():
        ht_ref[...] = h_sc[...]
        tail_ref[...] = tail_sc[...]


def _rec_front(x3, gain, w_in, conv_w, conv_b, w_gates, b_gates, lam, *, side_casts=()):
    n_batch, t, d = x3.shape
    c = D_RNN
    n_rows = n_batch * PERM_STEPS
    n_blocks = t // PERM_STEPS
    n_tail = (CONV_W - 1) * n_batch
    blk = lambda s: jnp.minimum(s, n_blocks - 1)
    const = lambda *shape: pl.BlockSpec(shape, lambda s: (0,) * len(shape))

    in_specs = [pl.BlockSpec((n_batch, PERM_STEPS, d), lambda s: (0, blk(s), 0)), const(1, d),
                pl.BlockSpec((d, 2 * c), lambda s: (0, 0), pipeline_mode=pl.Buffered(1)),
                const(CONV_W, c), const(1, c), const(N_LRU_BLOCKS, LRU_BLOCK, 2 * LRU_BLOCK),
                const(N_LRU_BLOCKS, 1, 2 * LRU_BLOCK), const(1, c)]
    args = [x3, gain.reshape(1, d).astype(F32), w_in, conv_w, conv_b, w_gates, b_gates, lam]
    out_specs = [pl.BlockSpec((n_batch, PERM_STEPS, c), lambda s: (0, jnp.maximum(s - 1, 0), 0)),
                 const(n_batch, c), const(n_tail, c)]
    out_shape = [jax.ShapeDtypeStruct((n_batch, t, c), BF16),
                 jax.ShapeDtypeStruct((n_batch, c), F32), jax.ShapeDtypeStruct((n_tail, c), F32)]
    for src, gain in side_casts:
        rows, cols = src.shape
        chunk = rows // n_blocks
        assert chunk * n_blocks == rows and chunk % (2 * SUBLANES) == 0, (rows, n_blocks)
        in_specs.append(pl.BlockSpec((chunk, cols), lambda s: (blk(s), 0)))
        args.append(src)
        if gain is not None:
            in_specs.append(pl.BlockSpec((chunk, 1), lambda s: (blk(s), 0)))
            args.append(gain.reshape(rows, 1).astype(F32))
        out_specs.append(pl.BlockSpec((chunk, cols), lambda s: (blk(s), 0)))
        out_shape.append(jax.ShapeDtypeStruct((rows, cols), BF16))

    return pl.pallas_call(
        functools.partial(_rec_front_body, n_batch=n_batch, side_gains=tuple(g is not None for _, g in side_casts)),
        grid=(n_blocks + 1,),
        in_specs=in_specs,
        out_specs=out_specs,
        out_shape=out_shape,
        scratch_shapes=[pltpu.VMEM((2, n_rows, 2 * c), F32), pltpu.VMEM((n_batch, c), F32),
                        pltpu.VMEM((n_tail, c), F32)],
        compiler_params=_params(("arbitrary",)),
        name="rec_front",
    )(*args)


def _attn_prompt_body(sink_ref, q_ref, kvc_ref, kvp_ref, o_ref):
    i = pl.program_id(1)
    kv = jnp.concatenate([kvp_ref[...], kvc_ref[...]], axis=0).astype(BF16)
    n_keys = 2 * WINDOW
    low_k = lax.broadcasted_iota(jnp.int32, (n_keys, LANES), 1) < HEAD_DIM
    low_q = lax.broadcasted_iota(jnp.int32, (WINDOW, LANES), 1) < HEAD_DIM
    zero = jnp.zeros((n_keys, LANES), BF16)
    one = jnp.ones((n_keys, LANES), BF16)
    from_prev = (lax.broadcasted_iota(jnp.int32, (WINDOW, WINDOW), 1)
                 > lax.broadcasted_iota(jnp.int32, (WINDOW, WINDOW), 0))
    prev_bias = jnp.where(i == 0, -jnp.inf, 0.0)
    dims = (((1,), (1,)), ((), ()))

    n_pairs = GROUP // 2

    def score_head(h):
        c0 = (h // 2) * LANES
        k_t = kv[:, c0:c0 + LANES]
        v_t = kv[:, KV_DIM + c0:KV_DIM + c0 + LANES]
        k_sw = pltpu.roll(k_t, HEAD_DIM, 1)
        v_sw = pltpu.roll(v_t, HEAD_DIM, 1)
        k_lo, k_hi = (k_t, k_sw) if h % 2 == 0 else (k_sw, k_t)
        v_lo, v_hi = (v_t, v_sw) if h % 2 == 0 else (v_sw, v_t)
        k_a = jnp.where(low_k, k_lo, zero)
        k_b = jnp.where(low_k, zero, k_hi)
        v_a = jnp.where(low_k, v_lo, one)
        v_b = jnp.where(low_k, one, v_hi)
        qc0 = h * GROUP * HEAD_DIM
        q_h = jnp.concatenate([q_ref[:, qc0 + p * LANES:qc0 + (p + 1) * LANES] for p in range(n_pairs)], axis=0)
        scores = [lax.dot_general(q_h, k_x, dims, preferred_element_type=F32) for k_x in (k_a, k_b)]
        return scores, (v_a, v_b)

    for h in range(N_KV_HEADS):
        scores, values = score_head(h)
        qc0 = h * GROUP * HEAD_DIM
        acc, esink = [], []
        for side, v_x in enumerate(values):
            s = scores[side]
            probs, es = [], []
            for p in range(n_pairs):
                s_p = s[p * WINDOW:(p + 1) * WINDOW]
                s_v = jnp.where(from_prev, s_p[:, :WINDOW] + prev_bias, s_p[:, WINDOW:])
                sink = sink_ref[h * GROUP + 2 * p + side] * LOG2E
                m = jnp.maximum(jnp.max(s_v, axis=-1, keepdims=True), sink)
                e = jnp.exp2(s_v - m)
                probs.append(jnp.concatenate([jnp.where(from_prev, e, 0.0), jnp.where(from_prev, 0.0, e)],
                                             axis=1).astype(BF16))
                es.append(jnp.exp2(sink - m))
            acc.append(jnp.dot(jnp.concatenate(probs, axis=0), v_x, preferred_element_type=F32))
            esink.append(es)
        for p in range(n_pairs):
            rows = slice(p * WINDOW, (p + 1) * WINDOW)
            acc_a, acc_b = acc[0][rows], acc[1][rows]
            num = jnp.where(low_q, acc_a, acc_b)
            den = pltpu.roll(jnp.where(low_q, acc_b, acc_a), HEAD_DIM, 1) + jnp.where(low_q, esink[0][p], esink[1][p])
            o_ref[:, qc0 + p * LANES:qc0 + (p + 1) * LANES] = (num / den).astype(o_ref.dtype)


def _attn_prompt(q3, kv3, sinks):
    b, t, _ = q3.shape
    nb = t // WINDOW
    return pl.pallas_call(
        _attn_prompt_body,
        grid=(b, nb),
        in_specs=[
            pl.BlockSpec(memory_space=pltpu.SMEM),
            pl.BlockSpec((None, WINDOW, Q_DIM), lambda bi, i: (bi, i, 0)),
            pl.BlockSpec((None, WINDOW, 2 * KV_DIM), lambda bi, i: (bi, i, 0)),
            pl.BlockSpec((None, WINDOW, 2 * KV_DIM), lambda bi, i: (bi, jnp.maximum(i - 1, 0), 0)),
        ],
        out_specs=pl.BlockSpec((None, WINDOW, Q_DIM), lambda bi, i: (bi, i, 0)),
        out_shape=jax.ShapeDtypeStruct((b, t, Q_DIM), BF16),
        compiler_params=_params(("arbitrary", "arbitrary")),
        name="attn_prompt",
    )(sinks, q3, kv3, kv3)


SAMPLE_BATCH_TILE = 8


def _attn_sample_body(sink_ref, q_ref, kvn_ref, ck_ref, cv_ref, o_ref, *, n_steps):
    bb = SAMPLE_BATCH_TILE
    n_new = n_steps * bb
    n_cache = bb * WINDOW
    n_keys = n_cache + n_new
    n_keys_pad = -(-n_keys // LANES) * LANES
    q = q_ref[...].reshape(n_new, Q_DIM)
    kvn = kvn_ref[...].reshape(n_new, 2 * KV_DIM)
    pad = jnp.zeros((n_keys_pad - n_keys, KV_DIM), F32)
    k_all = jnp.concatenate([ck_ref[...].reshape(n_cache, KV_DIM), kvn[:, :KV_DIM], pad], axis=0).astype(BF16)
    v_all = jnp.concatenate([cv_ref[...].reshape(n_cache, KV_DIM), kvn[:, KV_DIM:], pad], axis=0).astype(BF16)
    n_rows = GROUP * n_new
    r = lax.broadcasted_iota(jnp.int32, (n_rows, n_keys_pad), 0)
    c = lax.broadcasted_iota(jnp.int32, (n_rows, n_keys_pad), 1)
    b_r = r % bb
    t_r = (r // bb) % n_steps
    is_cache = c < n_cache
    c2 = jnp.maximum(c - n_cache, 0)
    b_c = jnp.where(is_cache, c // WINDOW, c2 % bb)
    k_step = jnp.where(is_cache, c % WINDOW - WINDOW, c2 // bb)
    dq = t_r - k_step
    allowed = jnp.logical_and(jnp.logical_and(b_c == b_r, c < n_keys),
                              jnp.logical_and(dq >= 0, dq < WINDOW))
    grp = lax.broadcasted_iota(jnp.int32, (n_rows, 1), 0) // n_new
    dims = (((1,), (1,)), ((), ()))
    for h in range(N_KV_HEADS):
        k_h = k_all[:, h * HEAD_DIM:(h + 1) * HEAD_DIM]
        v_h = v_all[:, h * HEAD_DIM:(h + 1) * HEAD_DIM]
        q_h = jnp.concatenate(
            [q[:, (h * GROUP + g) * HEAD_DIM:(h * GROUP + g + 1) * HEAD_DIM] for g in range(GROUP)], axis=0)
        sink = jnp.zeros((n_rows, 1), F32)
        for g in range(GROUP):
            sink = jnp.where(grp == g, sink_ref[h * GROUP + g] * LOG2E, sink)
        s = lax.dot_general(q_h, k_h, dims, preferred_element_type=F32)
        s = jnp.where(allowed, s, -jnp.inf)
        m = jnp.maximum(jnp.max(s, axis=-1, keepdims=True), sink)
        p = jnp.exp2(s - m)
        denom = jnp.sum(p, axis=-1, keepdims=True) + jnp.exp2(sink - m)
        o_h = jnp.dot(p.astype(BF16), v_h, preferred_element_type=F32) / denom
        for g in range(GROUP):
            c0 = (h * GROUP + g) * HEAD_DIM
            o_ref[:, :, c0:c0 + HEAD_DIM] = (
                o_h[g * n_new:(g + 1) * n_new].reshape(n_steps, bb, HEAD_DIM).astype(o_ref.dtype))


def _attn_sample(q3, kvn3, cache_k, cache_v, sinks):
    n_steps, b, _ = q3.shape
    bb = SAMPLE_BATCH_TILE
    return pl.pallas_call(
        functools.partial(_attn_sample_body, n_steps=n_steps),
        grid=(b // bb,),
        in_specs=[
            pl.BlockSpec(memory_space=pltpu.SMEM),
            pl.BlockSpec((n_steps, bb, Q_DIM), lambda i: (0, i, 0)),
            pl.BlockSpec((n_steps, bb, 2 * KV_DIM), lambda i: (0, i, 0)),
            pl.BlockSpec((bb, WINDOW, KV_DIM), lambda i: (i, 0, 0)),
            pl.BlockSpec((bb, WINDOW, KV_DIM), lambda i: (i, 0, 0)),
        ],
        out_specs=pl.BlockSpec((n_steps, bb, Q_DIM), lambda i: (0, i, 0)),
        out_shape=jax.ShapeDtypeStruct((n_steps, b, Q_DIM), BF16),
        compiler_params=_params(("arbitrary",)),
        name="attn_sample",
    )(sinks, q3, kvn3, cache_k, cache_v)


TILES = {
    "rec_in": dict(bm=512, bn=2048),
    "rec_out": dict(bm=512, bn=2048),
    "attn_out": dict(bm=512, bn=2048),
    "mlp_up": dict(bm=1024, bn=2048),
    "mlp_down": dict(bm=1024, bn=1024, bk=2048),
}


def _mm(name, x, w, **kw):
    return _matmul(x, w, name=name, **TILES[name], **kw)


def _flat(w):
    return w.reshape(-1, w.shape[-1])


def _trunk(z, x, attn_fn, raw, w):
    h, hb, ssq = _mm("rec_out", z, w["rec_w_out"], out_dtype=F32, resid=x, bf16_copy=True, ssq_out=True)
    if "mlp_w_down" not in w:
        hid, w_down, w_kv, w_q, w_o = _mm("mlp_up", hb, w["mlp_w_up"], layer=0, out_dtype=BF16, act="relu2",
                                          ssq_in=ssq, side_casts=[
            (_flat(raw["mlp_w_down"]), None), (raw["w_kv"], raw["kv_norm"]),
            (_flat(raw["attn_w_q"]), raw["norm_mix"][1]), (_flat(raw["attn_w_o"]), None)])
        w["mlp_w_down"] = w_down.reshape(raw["mlp_w_down"].shape)
        w["w_kv"] = w_kv
        w["attn_w_q"] = w_q
        w["attn_w_o"] = w_o.reshape(raw["attn_w_o"].shape)
    else:
        hid = _mm("mlp_up", hb, w["mlp_w_up"], layer=0, out_dtype=BF16, act="relu2", ssq_in=ssq)
    h, hb, ssq = _mm("mlp_down", hid, w["mlp_w_down"], layer=0, out_dtype=F32, resid=h, bf16_copy=True, ssq_out=True)
    q, kv = _qkv_proj(hb, ssq, w["attn_w_q"], w["w_kv"])
    o = attn_fn(q, kv)
    h, hb, ssq = _mm("attn_out", o, w["attn_w_o"], out_dtype=F32, resid=h, bf16_copy=True, ssq_out=True)
    hid = _mm("mlp_up", hb, w["mlp_w_up"], layer=1, out_dtype=BF16, act="relu2", ssq_in=ssq)
    h = _mm("mlp_down", hid, w["mlp_w_down"], layer=1, out_dtype=F32, resid=h)
    return _rmsnorm(h, raw["final_norm"], F32), kv


def kernel(x_prompt, x_sample, state_conv, state_h, cache_k, cache_v, norm_mix, norm_mlp, rec_w_in,
           rec_conv_w, rec_conv_b, rec_gate_a_w, rec_gate_a_b, rec_gate_x_w, rec_gate_x_b, rec_lambda,
           rec_w_out, kv_norm, w_kv, attn_w_q, attn_sinks, attn_w_o, mlp_w_up, mlp_w_down, final_norm):
    b, t, d = x_prompt.shape
    sb, st, _ = x_sample.shape
    c = D_RNN
    n_tail = CONV_W - 1

    raw = {
        "norm_mix": norm_mix, "kv_norm": kv_norm, "final_norm": final_norm,
        "w_kv": w_kv, "attn_w_q": attn_w_q, "attn_w_o": attn_w_o, "mlp_w_down": mlp_w_down,
    }
    w = {"rec_w_in": rec_w_in.astype(BF16)}
    conv_w = rec_conv_w[0]
    conv_b = rec_conv_b[0].reshape(1, c)
    lam = rec_lambda[0].reshape(1, c)
    w_gates = jnp.concatenate([rec_gate_a_w[0], rec_gate_x_w[0]], axis=-1).astype(BF16)
    b_gates = jnp.concatenate([rec_gate_a_b[0], rec_gate_x_b[0]], axis=-1).reshape(N_LRU_BLOCKS, 1, 2 * LRU_BLOCK)
    sinks = attn_sinks[0].astype(F32)

    def attn_prompt(q, kv):
        o = _attn_prompt(q.reshape(b, t, Q_DIM), kv.reshape(b, t, 2 * KV_DIM), sinks)
        return o.reshape(b * t, Q_DIM)

    z_p, h_p, tail_p, w_out, w_up = _rec_front(
        x_prompt, norm_mix[0], w["rec_w_in"][0], conv_w, conv_b, w_gates, b_gates, lam,
        side_casts=[(_flat(rec_w_out), None), (_flat(mlp_w_up), norm_mlp.reshape(-1))])
    w["rec_w_out"] = w_out.reshape(rec_w_out.shape)
    w["mlp_w_up"] = w_up.reshape(mlp_w_up.shape)
    y_p, kv_p = _trunk(z_p.reshape(b * t, c), x_prompt.reshape(b * t, d), attn_prompt, raw, w)
    y_prompt = y_p.reshape(b, t, d)
    conv_p = jnp.swapaxes(tail_p.reshape(n_tail, b, c), 0, 1).reshape(b, 1, n_tail, c)
    keep = min(WINDOW, t)
    kv_tail = kv_p.reshape(b, t, 2 * KV_DIM)[:, t - keep:]
    new_k_prompt = kv_tail[..., :KV_DIM].reshape(b, keep, N_KV_HEADS, HEAD_DIM)
    new_v_prompt = kv_tail[..., KV_DIM:].reshape(b, keep, N_KV_HEADS, HEAD_DIM)

    conv_state_tb = jnp.swapaxes(state_conv[:, 0], 0, 1).reshape(n_tail * sb, c)
    h0 = state_h[:, 0]
    ck = cache_k.reshape(sb, WINDOW, KV_DIM)
    cv = cache_v.reshape(sb, WINDOW, KV_DIM)

    def attn_sample(q, kv):
        o = _attn_sample(q.reshape(st, sb, Q_DIM), kv.reshape(st, sb, 2 * KV_DIM), ck, cv, sinks)
        return o.reshape(st * sb, Q_DIM)

    x_s = jnp.swapaxes(x_sample, 0, 1).reshape(st * sb, d)
    gx_s = _mm("rec_in", _rmsnorm(x_s, norm_mix[0], BF16), w["rec_w_in"], out_dtype=F32)
    z_s, h_s = _lru(gx_s, conv_state_tb, h0, conv_w, conv_b, w_gates, b_gates, lam, n_batch=sb, n_steps=st)
    y_s, kv_s = _trunk(z_s, x_s, attn_sample, raw, w)
    y_sample = jnp.swapaxes(y_s.reshape(st, sb, d), 0, 1)
    conv_s = jnp.swapaxes(gx_s.reshape(st, sb, 2 * c)[st - n_tail:, :, c:], 0, 1).reshape(sb, 1, n_tail, c)
    kv_s = jnp.swapaxes(kv_s.reshape(st, sb, 2 * KV_DIM), 0, 1)
    new_k_sample = kv_s[..., :KV_DIM].reshape(sb, st, N_KV_HEADS, HEAD_DIM)
    new_v_sample = kv_s[..., KV_DIM:].reshape(sb, st, N_KV_HEADS, HEAD_DIM)

    return (y_prompt, y_sample, conv_p, h_p.reshape(b, 1, c), new_k_prompt, new_v_prompt,
            conv_s, h_s.reshape(sb, 1, c), new_k_sample, new_v_sample)
```

```python
import functools
import math

import jax
import jax.numpy as jnp
from jax import lax
from jax.experimental import pallas as pl
from jax.experimental.pallas import tpu as pltpu

F32 = jnp.float32
BF16 = jnp.bfloat16

D_MODEL = 2048
D_RNN = 2048
N_LRU_BLOCKS = 16
LRU_BLOCK = D_RNN // N_LRU_BLOCKS
CONV_W = 4
LRU_C = 8.0
N_HEADS = 32
N_KV_HEADS = 4
HEAD_DIM = 64
GROUP = N_HEADS // N_KV_HEADS
Q_DIM = N_HEADS * HEAD_DIM
KV_DIM = N_KV_HEADS * HEAD_DIM
WINDOW = 128
EPS = 1e-6
LOG2E = math.log2(math.e)

SUBLANES = 8
LANES = 128
VMEM_LIMIT_BYTES = 52 * 1024 * 1024


def _params(semantics):
    return pltpu.CompilerParams(dimension_semantics=semantics,
                                vmem_limit_bytes=VMEM_LIMIT_BYTES)


def _rmsnorm_body(x_ref, g_ref, o_ref):
    x = x_ref[...]
    y = x * lax.rsqrt(jnp.mean(x * x, axis=-1, keepdims=True) + EPS)
    o_ref[...] = (y * g_ref[...]).astype(o_ref.dtype)


def _rmsnorm(x, gain, out_dtype, tm=512):
    m, d = x.shape
    tm = min(tm, m)
    row_spec = pl.BlockSpec((tm, d), lambda i: (i, 0))
    return pl.pallas_call(
        _rmsnorm_body,
        grid=(m // tm,),
        in_specs=[row_spec, pl.BlockSpec((1, d), lambda i: (0, 0))],
        out_specs=row_spec,
        out_shape=jax.ShapeDtypeStruct((m, d), out_dtype),
        compiler_params=_params(("arbitrary",)),
        name="rmsnorm",
    )(x, gain.reshape(1, d).astype(F32))


PERM_STEPS = 32


def _perm_matrix(n_batch, to_time_major):
    n = n_batch * PERM_STEPS
    r = lax.broadcasted_iota(jnp.int32, (n, n), 0)
    c = lax.broadcasted_iota(jnp.int32, (n, n), 1)
    if to_time_major:
        src = (r % n_batch) * PERM_STEPS + r // n_batch
    else:
        src = (r % PERM_STEPS) * n_batch + r // PERM_STEPS
    return jnp.where(c == src, 1.0, 0.0).astype(BF16)


def _matmul_body(*refs, nk, act, has_scale, has_resid, has_copy, has_ssq, side_gains, norm_dim):
    refs = list(refs)
    x_ref, w_ref = refs[:2]
    pos = 2
    s_ref = r_ref = b_ref = q_ref = None
    if has_scale:
        s_ref = refs[pos]
        pos += 1
    if has_resid:
        r_ref = refs[pos]
        pos += 1
    side_in = []
    for has_gain in side_gains:
        side_in.append((refs[pos], refs[pos + 1] if has_gain else None))
        pos += 2 if has_gain else 1
    o_ref = refs[pos]
    pos += 1
    if has_copy:
        b_ref = refs[pos]
        pos += 1
    if has_ssq:
        q_ref = refs[pos]
        pos += 1
    side_out = refs[pos:pos + len(side_gains)]
    pos += len(side_gains)
    acc_ref = refs[pos] if nk > 1 else None

    def partial_product():
        for (src_ref, g_ref), dst_ref in zip(side_in, side_out):
            chunk = src_ref[...]
            if g_ref is not None:
                chunk = chunk * g_ref[...]
            dst_ref[...] = chunk.astype(dst_ref.dtype)
        return jnp.dot(x_ref[...], w_ref[...], preferred_element_type=F32)

    def finish(acc):
        if s_ref is not None:
            acc = acc * lax.rsqrt(jnp.sum(s_ref[...], axis=-1, keepdims=True) * (1.0 / norm_dim) + EPS)
        if act == "relu2":
            r = jnp.maximum(acc, 0.0)
            acc = r * r
        if r_ref is not None:
            acc = r_ref[...] + acc
        o_ref[...] = acc.astype(o_ref.dtype)
        if b_ref is not None:
            b_ref[...] = acc.astype(b_ref.dtype)
        if q_ref is not None:
            sq = acc * acc
            tot = sq[:, 0:LANES]
            for c in range(1, sq.shape[1] // LANES):
                tot = tot + sq[:, c * LANES:(c + 1) * LANES]
            q_ref[...] = tot

    if nk == 1:
        finish(partial_product())
    else:
        k = pl.program_id(2)

        @pl.when(k == 0)
        def _():
            acc_ref[...] = partial_product()

        @pl.when(jnp.logical_and(k > 0, k < nk - 1))
        def _():
            acc_ref[...] += partial_product()

        @pl.when(k == nk - 1)
        def _():
            finish(acc_ref[...] + partial_product())


def _matmul(x, w, *, out_dtype, layer=0, act=None, ssq_in=None, resid=None, bf16_copy=False, ssq_out=False,
            side_casts=(), cols_outer=False, bm=1024, bn=1024, bk=None, name="matmul"):
    m = x.shape[0]
    _, kdim, n = w.shape
    bm, bn = min(bm, m), min(bn, n)
    bk = kdim if bk is None else min(bk, kdim)
    nk, nb = kdim // bk, n // bn
    mb = m // bm
    n_steps = mb * nb * nk

    def spec(shape, index_map):
        if cols_outer:
            return pl.BlockSpec(shape, lambda j, i, k: index_map(i, j, k))
        return pl.BlockSpec(shape, index_map)

    in_specs = [spec((bm, bk), lambda i, j, k: (i, k)),
                spec((None, bk, bn), lambda i, j, k: (layer, k, j))]
    args = [x, w]
    if ssq_in is not None:
        in_specs.append(spec((bm, ssq_in.shape[1]), lambda i, j, k: (i, 0)))
        args.append(ssq_in)
    tile_spec = spec((bm, bn), lambda i, j, k: (i, j))
    if resid is not None:
        in_specs.append(tile_spec)
        args.append(resid)
    step = lambda i, j, k: (i * nb + j) * nk + k
    for src, gain in side_casts:
        rows, cols = src.shape
        chunk = rows // n_steps
        assert chunk * n_steps == rows and chunk % (2 * SUBLANES) == 0, (name, rows, n_steps)
        in_specs.append(spec((chunk, cols), lambda i, j, k: (step(i, j, k), 0)))
        args.append(src)
        if gain is not None:
            in_specs.append(spec((chunk, 1), lambda i, j, k: (step(i, j, k), 0)))
            args.append(gain.reshape(rows, 1).astype(F32))

    out_specs = [tile_spec]
    out_shape = [jax.ShapeDtypeStruct((m, n), out_dtype)]
    if bf16_copy:
        out_specs.append(tile_spec)
        out_shape.append(jax.ShapeDtypeStruct((m, n), BF16))
    if ssq_out:
        out_specs.append(spec((bm, LANES), lambda i, j, k: (i, j)))
        out_shape.append(jax.ShapeDtypeStruct((m, nb * LANES), F32))
    for src, _ in side_casts:
        rows, cols = src.shape
        out_specs.append(spec((rows // n_steps, cols), lambda i, j, k: (step(i, j, k), 0)))
        out_shape.append(jax.ShapeDtypeStruct((rows, cols), BF16))

    outs = pl.pallas_call(
        functools.partial(_matmul_body, nk=nk, act=act, has_scale=ssq_in is not None,
                          has_resid=resid is not None, has_copy=bf16_copy, has_ssq=ssq_out,
                          side_gains=tuple(g is not None for _, g in side_casts), norm_dim=kdim),
        grid=(nb, mb, nk) if cols_outer else (mb, nb, nk),
        in_specs=in_specs,
        out_specs=out_specs,
        out_shape=out_shape,
        scratch_shapes=[pltpu.VMEM((bm, bn), F32)] if nk > 1 else [],
        compiler_params=_params(("arbitrary", "arbitrary", "arbitrary")),
        name=name,
    )(*args)
    return outs[0] if len(outs) == 1 else outs


def _qkv_body(x_ref, wq_ref, wkv_ref, q_ref, kv_ref):
    xf = x_ref[...]
    scale = lax.rsqrt(jnp.mean(xf * xf, axis=-1, keepdims=True) + EPS)
    x = xf.astype(BF16)
    q = jnp.dot(x, wq_ref[...], preferred_element_type=F32)
    q_ref[...] = (q * (scale * (HEAD_DIM ** -0.5 * LOG2E))).astype(q_ref.dtype)
    kv_ref[...] = jnp.dot(x, wkv_ref[...], preferred_element_type=F32) * scale


def _qkv_proj(x, w_q, w_kv, bm=512):
    m, kdim = x.shape
    bm = min(bm, m)
    return pl.pallas_call(
        _qkv_body,
        grid=(m // bm,),
        in_specs=[pl.BlockSpec((bm, kdim), lambda i: (i, 0)),
                  pl.BlockSpec((kdim, Q_DIM), lambda i: (0, 0)),
                  pl.BlockSpec((kdim, 2 * KV_DIM), lambda i: (0, 0))],
        out_specs=[pl.BlockSpec((bm, Q_DIM), lambda i: (i, 0)),
                   pl.BlockSpec((bm, 2 * KV_DIM), lambda i: (i, 0))],
        out_shape=[jax.ShapeDtypeStruct((m, Q_DIM), BF16), jax.ShapeDtypeStruct((m, 2 * KV_DIM), F32)],
        compiler_params=_params(("arbitrary",)),
        name="qkv_proj",
    )(x, w_q, w_kv)


def _gelu_tanh(x):
    return x * (0.5 + 0.5 * jnp.tanh(0.7978845608028654 * (x + 0.044715 * (x * x * x))))


def _softplus(z):
    return jnp.maximum(z, 0.0) + jnp.log1p(jnp.exp(-jnp.abs(z)))


def _lru_decay(lam):
    return (-0.5 * LRU_C * LOG2E) * _softplus(-lam)


def _lru_columns(gate, x, tail, h, cw, cb, wg, bg, d, *, n_steps, n_batch):
    n_rows = n_steps * n_batch
    x_ext = jnp.concatenate([tail, x], axis=0)
    xc = cb + cw[0:1] * x_ext[0:n_rows]
    for j in range(1, CONV_W):
        xc = xc + cw[j:j + 1] * x_ext[j * n_batch:j * n_batch + n_rows]
    half_pre = 0.5 * (jnp.dot(xc.astype(BF16), wg, preferred_element_type=F32) + bg)
    tr = jnp.tanh(half_pre[:, :LRU_BLOCK])
    ti = jnp.tanh(half_pre[:, LRU_BLOCK:])
    a = jnp.exp2(d * tr + d)
    gain2 = 1.0 - a * a
    u = (gain2 * lax.rsqrt(jnp.maximum(gain2, 1e-30))) * ((0.5 + 0.5 * ti) * xc)
    hs = []
    for s in range(n_steps):
        rows = slice(s * n_batch, (s + 1) * n_batch)
        h = a[rows] * h + u[rows]
        hs.append(h)
    return _gelu_tanh(gate) * jnp.concatenate(hs, axis=0), x_ext[n_rows:], h


def _lru_body(gate_ref, xr_ref, cs_ref, h0_ref, cw_ref, cb_ref, wg_ref, bg_ref, lam_ref,
              z_ref, ht_ref, h_sc, tail_sc, *, n_steps, n_batch, tc_cols):
    t = pl.program_id(1)

    @pl.when(t == 0)
    def _():
        h_sc[...] = h0_ref[...]
        tail_sc[...] = cs_ref[...]

    decay = _lru_decay(lam_ref[...])
    for n in range(tc_cols // LRU_BLOCK):
        cols = slice(n * LRU_BLOCK, (n + 1) * LRU_BLOCK)
        z, tail_sc[:, cols], h_sc[:, cols] = _lru_columns(
            gate_ref[:, cols], xr_ref[:, cols], tail_sc[:, cols], h_sc[:, cols], cw_ref[:, cols], cb_ref[:, cols],
            wg_ref[n], bg_ref[n], decay[:, cols], n_steps=n_steps, n_batch=n_batch)
        z_ref[:, cols] = z.astype(z_ref.dtype)

    @pl.when(t == pl.num_programs(1) - 1)
    def _():
        ht_ref[...] = h_sc[...]


def _lru(gx, conv_state, h0, conv_w, conv_b, w_gates, b_gates, lam, *, n_batch, n_steps, tc_cols=512):
    m = gx.shape[0]
    c = D_RNN
    n_rows = n_steps * n_batch
    n_tail = (CONV_W - 1) * n_batch
    ncb = c // tc_cols
    gpb = tc_cols // LRU_BLOCK
    z_spec = pl.BlockSpec((n_rows, tc_cols), lambda ci, ti: (ti, ci))
    z_shape = jax.ShapeDtypeStruct((m, c), BF16)
    z, ht = pl.pallas_call(
        functools.partial(_lru_body, n_steps=n_steps, n_batch=n_batch, tc_cols=tc_cols),
        grid=(ncb, m // n_rows),
        in_specs=[
            pl.BlockSpec((n_rows, tc_cols), lambda ci, ti: (ti, ci)),
            pl.BlockSpec((n_rows, tc_cols), lambda ci, ti: (ti, ncb + ci)),
            pl.BlockSpec((n_tail, tc_cols), lambda ci, ti: (0, ci)),
            pl.BlockSpec((n_batch, tc_cols), lambda ci, ti: (0, ci)),
            pl.BlockSpec((CONV_W, tc_cols), lambda ci, ti: (0, ci)),
            pl.BlockSpec((1, tc_cols), lambda ci, ti: (0, ci)),
            pl.BlockSpec((gpb, LRU_BLOCK, 2 * LRU_BLOCK), lambda ci, ti: (ci, 0, 0)),
            pl.BlockSpec((gpb, 1, 2 * LRU_BLOCK), lambda ci, ti: (ci, 0, 0)),
            pl.BlockSpec((1, tc_cols), lambda ci, ti: (0, ci)),
        ],
        out_specs=[z_spec, pl.BlockSpec((n_batch, tc_cols), lambda ci, ti: (0, ci))],
        out_shape=[z_shape, jax.ShapeDtypeStruct((n_batch, c), F32)],
        scratch_shapes=[pltpu.VMEM((n_batch, tc_cols), F32), pltpu.VMEM((n_tail, tc_cols), F32)],
        compiler_params=_params(("arbitrary", "arbitrary")),
        name="lru",
    )(gx, gx, conv_state, h0, conv_w, conv_b, w_gates, b_gates, lam)
    return z, ht


def _rec_front_body(*refs, n_batch, side_gains):
    x_ref, norm_gain_ref, win_ref, cw_ref, cb_ref, wg_ref, bg_ref, lam_ref = refs[:8]
    pos = 8
    side_in = []
    for has_gain in side_gains:
        side_in.append((refs[pos], refs[pos + 1] if has_gain else None))
        pos += 2 if has_gain else 1
    z_ref, ht_ref, tail_ref = refs[pos:pos + 3]
    pos += 3
    side_out = refs[pos:pos + len(side_gains)]
    gx_sc, h_sc, tail_sc = refs[pos + len(side_gains):]
    s = pl.program_id(0)
    c = D_RNN

    @pl.when(s <= 1)
    def _():
        h_sc[...] = jnp.zeros_like(h_sc)
        tail_sc[...] = jnp.zeros_like(tail_sc)

    @pl.when(s == 0)
    def _():
        gx_sc[1] = jnp.zeros(gx_sc.shape[1:], gx_sc.dtype)

    for (src_ref, g_ref), dst_ref in zip(side_in, side_out):
        chunk = src_ref[...]
        if g_ref is not None:
            chunk = chunk * g_ref[...]
        dst_ref[...] = chunk.astype(dst_ref.dtype)

    x = x_ref[...].reshape(n_batch * PERM_STEPS, x_ref.shape[-1])
    y = x * lax.rsqrt(jnp.mean(x * x, axis=-1, keepdims=True) + EPS)
    xn = jnp.dot(_perm_matrix(n_batch, True), (y * norm_gain_ref[...]).astype(BF16),
                 preferred_element_type=F32).astype(BF16)
    slot = s % 2
    gx_sc[slot] = jnp.dot(xn, win_ref[...], preferred_element_type=F32)

    prev = 1 - slot
    perm = _perm_matrix(n_batch, False)
    decay = _lru_decay(lam_ref[...])
    for n in range(N_LRU_BLOCKS):
        cols = slice(n * LRU_BLOCK, (n + 1) * LRU_BLOCK)
        z, tail_sc[:, cols], h_sc[:, cols] = _lru_columns(
            gx_sc[prev, :, n * LRU_BLOCK:(n + 1) * LRU_BLOCK], gx_sc[prev, :, c + n * LRU_BLOCK:c + (n + 1) * LRU_BLOCK],
            tail_sc[:, cols], h_sc[:, cols], cw_ref[:, cols], cb_ref[:, cols], wg_ref[n], bg_ref[n], decay[:, cols],
            n_steps=PERM_STEPS, n_batch=n_batch)
        zp = jnp.dot(perm, z.astype(BF16), preferred_element_type=F32)
        z_ref[:, :, cols] = zp.astype(z_ref.dtype).reshape(n_batch, PERM_STEPS, LRU_BLOCK)

    @pl.when(s == pl.num_programs(0) - 1)
    def _():
        ht_ref[...] = h_sc[...]
        tail_ref[...] = tail_sc[...]


def _rec_front(x3, gain, w_in, conv_w, conv_b, w_gates, b_gates, lam, *, side_casts=()):
    n_batch, t, d = x3.shape
    c = D_RNN
    n_rows = n_batch * PERM_STEPS
    n_blocks = t // PERM_STEPS
    n_tail = (CONV_W - 1) * n_batch
    blk = lambda s: jnp.minimum(s, n_blocks - 1)
    const = lambda *shape: pl.BlockSpec(shape, lambda s: (0,) * len(shape))

    in_specs = [pl.BlockSpec((n_batch, PERM_STEPS, d), lambda s: (0, blk(s), 0)), const(1, d),
                pl.BlockSpec((d, 2 * c), lambda s: (0, 0), pipeline_mode=pl.Buffered(1)),
                const(CONV_W, c), const(1, c), const(N_LRU_BLOCKS, LRU_BLOCK, 2 * LRU_BLOCK),
                const(N_LRU_BLOCKS, 1, 2 * LRU_BLOCK), const(1, c)]
    args = [x3, gain.reshape(1, d).astype(F32), w_in, conv_w, conv_b, w_gates, b_gates, lam]
    out_specs = [pl.BlockSpec((n_batch, PERM_STEPS, c), lambda s: (0, jnp.maximum(s - 1, 0), 0)),
                 const(n_batch, c), const(n_tail, c)]
    out_shape = [jax.ShapeDtypeStruct((n_batch, t, c), BF16),
                 jax.ShapeDtypeStruct((n_batch, c), F32), jax.ShapeDtypeStruct((n_tail, c), F32)]
    for src, gain in side_casts:
        rows, cols = src.shape
        chunk = rows // n_blocks
        assert chunk * n_blocks == rows and chunk % (2 * SUBLANES) == 0, (rows, n_blocks)
        in_specs.append(pl.BlockSpec((chunk, cols), lambda s: (blk(s), 0)))
        args.append(src)
        if gain is not None:
            in_specs.append(pl.BlockSpec((chunk, 1), lambda s: (blk(s), 0)))
            args.append(gain.reshape(rows, 1).astype(F32))
        out_specs.append(pl.BlockSpec((chunk, cols), lambda s: (blk(s), 0)))
        out_shape.append(jax.ShapeDtypeStruct((rows, cols), BF16))

    return pl.pallas_call(
        functools.partial(_rec_front_body, n_batch=n_batch, side_gains=tuple(g is not None for _, g in side_casts)),
        grid=(n_blocks + 1,),
        in_specs=in_specs,
        out_specs=out_specs,
        out_shape=out_shape,
        scratch_shapes=[pltpu.VMEM((2, n_rows, 2 * c), F32), pltpu.VMEM((n_batch, c), F32),
                        pltpu.VMEM((n_tail, c), F32)],
        compiler_params=_params(("arbitrary",)),
        name="rec_front",
    )(*args)


ATTN_BLOCKS = 2


def _attn_prompt_body(sink_ref, q_ref, kvc_ref, kvp_ref, o_ref):
    i = pl.program_id(1)
    for blk in range(ATTN_BLOCKS):
        rows = slice(blk * WINDOW, (blk + 1) * WINDOW)
        prev = kvp_ref[...] if blk == 0 else kvc_ref[(blk - 1) * WINDOW:blk * WINDOW]
        kv = jnp.concatenate([prev, kvc_ref[rows]], axis=0).astype(BF16)
        _attend_block(sink_ref, q_ref, o_ref, rows, kv, jnp.where(i == 0, -jnp.inf, 0.0) if blk == 0 else None)


def _attend_block(sink_ref, q_ref, o_ref, rows, kv, prev_bias):
    n_keys = 2 * WINDOW
    low_k = lax.broadcasted_iota(jnp.int32, (n_keys, LANES), 1) < HEAD_DIM
    low_q = lax.broadcasted_iota(jnp.int32, (WINDOW, LANES), 1) < HEAD_DIM
    zero = jnp.zeros((n_keys, LANES), BF16)
    one = jnp.ones((n_keys, LANES), BF16)
    from_prev = (lax.broadcasted_iota(jnp.int32, (WINDOW, WINDOW), 1)
                 > lax.broadcasted_iota(jnp.int32, (WINDOW, WINDOW), 0))
    dims = (((1,), (1,)), ((), ()))

    n_pairs = GROUP // 2

    def score_head(h):
        c0 = (h // 2) * LANES
        k_t = kv[:, c0:c0 + LANES]
        v_t = kv[:, KV_DIM + c0:KV_DIM + c0 + LANES]
        k_sw = pltpu.roll(k_t, HEAD_DIM, 1)
        v_sw = pltpu.roll(v_t, HEAD_DIM, 1)
        k_lo, k_hi = (k_t, k_sw) if h % 2 == 0 else (k_sw, k_t)
        v_lo, v_hi = (v_t, v_sw) if h % 2 == 0 else (v_sw, v_t)
        k_a = jnp.where(low_k, k_lo, zero)
        k_b = jnp.where(low_k, zero, k_hi)
        v_a = jnp.where(low_k, v_lo, one)
        v_b = jnp.where(low_k, one, v_hi)
        qc0 = h * GROUP * HEAD_DIM
        q_h = jnp.concatenate([q_ref[rows, qc0 + p * LANES:qc0 + (p + 1) * LANES] for p in range(n_pairs)], axis=0)
        scores = [lax.dot_general(q_h, k_x, dims, preferred_element_type=F32) for k_x in (k_a, k_b)]
        return scores, (v_a, v_b)

    for h in range(N_KV_HEADS):
        scores, values = score_head(h)
        qc0 = h * GROUP * HEAD_DIM
        acc, esink = [], []
        for side, v_x in enumerate(values):
            s = scores[side]
            probs, es = [], []
            for p in range(n_pairs):
                s_p = s[p * WINDOW:(p + 1) * WINDOW]
                s_prev = s_p[:, :WINDOW] if prev_bias is None else s_p[:, :WINDOW] + prev_bias
                s_v = jnp.where(from_prev, s_prev, s_p[:, WINDOW:])
                sink = sink_ref[h * GROUP + 2 * p + side] * LOG2E
                m = jnp.maximum(jnp.max(s_v, axis=-1, keepdims=True), sink)
                e = jnp.exp2(s_v - m)
                probs.append(jnp.concatenate([jnp.where(from_prev, e, 0.0), jnp.where(from_prev, 0.0, e)],
                                             axis=1).astype(BF16))
                es.append(jnp.exp2(sink - m))
            acc.append(jnp.dot(jnp.concatenate(probs, axis=0), v_x, preferred_element_type=F32))
            esink.append(es)
        for p in range(n_pairs):
            part = slice(p * WINDOW, (p + 1) * WINDOW)
            acc_a, acc_b = acc[0][part], acc[1][part]
            num = jnp.where(low_q, acc_a, acc_b)
            den = pltpu.roll(jnp.where(low_q, acc_b, acc_a), HEAD_DIM, 1) + jnp.where(low_q, esink[0][p], esink[1][p])
            o_ref[rows, qc0 + p * LANES:qc0 + (p + 1) * LANES] = (num / den).astype(o_ref.dtype)


def _attn_prompt(q3, kv3, sinks):
    b, t, _ = q3.shape
    step_rows = ATTN_BLOCKS * WINDOW
    nb = t // step_rows
    return pl.pallas_call(
        _attn_prompt_body,
        grid=(b, nb),
        in_specs=[
            pl.BlockSpec(memory_space=pltpu.SMEM),
            pl.BlockSpec((None, step_rows, Q_DIM), lambda bi, i: (bi, i, 0)),
            pl.BlockSpec((None, step_rows, 2 * KV_DIM), lambda bi, i: (bi, i, 0)),
            pl.BlockSpec((None, WINDOW, 2 * KV_DIM), lambda bi, i: (bi, jnp.maximum(ATTN_BLOCKS * i - 1, 0), 0)),
        ],
        out_specs=pl.BlockSpec((None, step_rows, Q_DIM), lambda bi, i: (bi, i, 0)),
        out_shape=jax.ShapeDtypeStruct((b, t, Q_DIM), BF16),
        compiler_params=_params(("arbitrary", "arbitrary")),
        name="attn_prompt",
    )(sinks, q3, kv3, kv3)


SAMPLE_BATCH_TILE = 8


def _attn_sample_body(sink_ref, q_ref, kvn_ref, ck_ref, cv_ref, o_ref, *, n_steps):
    bb = SAMPLE_BATCH_TILE
    n_new = n_steps * bb
    n_cache = bb * WINDOW
    n_keys = n_cache + n_new
    n_keys_pad = -(-n_keys // LANES) * LANES
    q = q_ref[...].reshape(n_new, Q_DIM)
    kvn = kvn_ref[...].reshape(n_new, 2 * KV_DIM)
    pad = jnp.zeros((n_keys_pad - n_keys, KV_DIM), F32)
    k_all = jnp.concatenate([ck_ref[...].reshape(n_cache, KV_DIM), kvn[:, :KV_DIM], pad], axis=0).astype(BF16)
    v_all = jnp.concatenate([cv_ref[...].reshape(n_cache, KV_DIM), kvn[:, KV_DIM:], pad], axis=0).astype(BF16)
    n_rows = GROUP * n_new
    r = lax.broadcasted_iota(jnp.int32, (n_rows, n_keys_pad), 0)
    c = lax.broadcasted_iota(jnp.int32, (n_rows, n_keys_pad), 1)
    b_r = r % bb
    t_r = (r // bb) % n_steps
    is_cache = c < n_cache
    c2 = jnp.maximum(c - n_cache, 0)
    b_c = jnp.where(is_cache, c // WINDOW, c2 % bb)
    k_step = jnp.where(is_cache, c % WINDOW - WINDOW, c2 // bb)
    dq = t_r - k_step
    allowed = jnp.logical_and(jnp.logical_and(b_c == b_r, c < n_keys),
                              jnp.logical_and(dq >= 0, dq < WINDOW))
    grp = lax.broadcasted_iota(jnp.int32, (n_rows, 1), 0) // n_new
    dims = (((1,), (1,)), ((), ()))
    for h in range(N_KV_HEADS):
        k_h = k_all[:, h * HEAD_DIM:(h + 1) * HEAD_DIM]
        v_h = v_all[:, h * HEAD_DIM:(h + 1) * HEAD_DIM]
        q_h = jnp.concatenate(
            [q[:, (h * GROUP + g) * HEAD_DIM:(h * GROUP + g + 1) * HEAD_DIM] for g in range(GROUP)], axis=0)
        sink = jnp.zeros((n_rows, 1), F32)
        for g in range(GROUP):
            sink = jnp.where(grp == g, sink_ref[h * GROUP + g] * LOG2E, sink)
        s = lax.dot_general(q_h, k_h, dims, preferred_element_type=F32)
        s = jnp.where(allowed, s, -jnp.inf)
        m = jnp.maximum(jnp.max(s, axis=-1, keepdims=True), sink)
        p = jnp.exp2(s - m)
        denom = jnp.sum(p, axis=-1, keepdims=True) + jnp.exp2(sink - m)
        o_h = jnp.dot(p.astype(BF16), v_h, preferred_element_type=F32) / denom
        for g in range(GROUP):
            c0 = (h * GROUP + g) * HEAD_DIM
            o_ref[:, :, c0:c0 + HEAD_DIM] = (
                o_h[g * n_new:(g + 1) * n_new].reshape(n_steps, bb, HEAD_DIM).astype(o_ref.dtype))


def _attn_sample(q3, kvn3, cache_k, cache_v, sinks):
    n_steps, b, _ = q3.shape
    bb = SAMPLE_BATCH_TILE
    return pl.pallas_call(
        functools.partial(_attn_sample_body, n_steps=n_steps),
        grid=(b // bb,),
        in_specs=[
            pl.BlockSpec(memory_space=pltpu.SMEM),
            pl.BlockSpec((n_steps, bb, Q_DIM), lambda i: (0, i, 0)),
            pl.BlockSpec((n_steps, bb, 2 * KV_DIM), lambda i: (0, i, 0)),
            pl.BlockSpec((bb, WINDOW, KV_DIM), lambda i: (i, 0, 0)),
            pl.BlockSpec((bb, WINDOW, KV_DIM), lambda i: (i, 0, 0)),
        ],
        out_specs=pl.BlockSpec((n_steps, bb, Q_DIM), lambda i: (0, i, 0)),
        out_shape=jax.ShapeDtypeStruct((n_steps, b, Q_DIM), BF16),
        compiler_params=_params(("arbitrary",)),
        name="attn_sample",
    )(sinks, q3, kvn3, cache_k, cache_v)


TILES = {
    "rec_in": dict(bm=512, bn=2048),
    "rec_out": dict(bm=512, bn=2048),
    "attn_out": dict(bm=512, bn=2048),
    "mlp_up": dict(bm=1024, bn=2048),
    "mlp_down": dict(bm=1024, bn=1024, bk=2048),
}


def _mm(name, x, w, **kw):
    return _matmul(x, w, name=name, **TILES[name], **kw)


def _flat(w):
    return w.reshape(-1, w.shape[-1])


def _trunk(z, x, attn_fn, raw, w):
    h, hb, ssq = _mm("rec_out", z, w["rec_w_out"], out_dtype=F32, resid=x, bf16_copy=True, ssq_out=True)
    if "mlp_w_down" not in w:
        hid, w_down, w_kv, w_q, w_o = _mm("mlp_up", hb, w["mlp_w_up"], layer=0, out_dtype=BF16, act="relu2",
                                          ssq_in=ssq, side_casts=[
            (_flat(raw["mlp_w_down"]), None), (raw["w_kv"], raw["kv_norm"]),
            (_flat(raw["attn_w_q"]), raw["norm_mix"][1]), (_flat(raw["attn_w_o"]), None)])
        w["mlp_w_down"] = w_down.reshape(raw["mlp_w_down"].shape)
        w["w_kv"] = w_kv
        w["attn_w_q"] = w_q
        w["attn_w_o"] = w_o.reshape(raw["attn_w_o"].shape)
    else:
        hid = _mm("mlp_up", hb, w["mlp_w_up"], layer=0, out_dtype=BF16, act="relu2", ssq_in=ssq)
    h = _mm("mlp_down", hid, w["mlp_w_down"], layer=0, out_dtype=F32, resid=h)
    q, kv = _qkv_proj(h, w["attn_w_q"], w["w_kv"])
    o = attn_fn(q, kv)
    h, hb, ssq = _mm("attn_out", o, w["attn_w_o"], out_dtype=F32, resid=h, bf16_copy=True, ssq_out=True)
    hid = _mm("mlp_up", hb, w["mlp_w_up"], layer=1, out_dtype=BF16, act="relu2", ssq_in=ssq)
    h = _mm("mlp_down", hid, w["mlp_w_down"], layer=1, out_dtype=F32, resid=h)
    return _rmsnorm(h, raw["final_norm"], F32), kv


def kernel(x_prompt, x_sample, state_conv, state_h, cache_k, cache_v, norm_mix, norm_mlp, rec_w_in,
           rec_conv_w, rec_conv_b, rec_gate_a_w, rec_gate_a_b, rec_gate_x_w, rec_gate_x_b, rec_lambda,
           rec_w_out, kv_norm, w_kv, attn_w_q, attn_sinks, attn_w_o, mlp_w_up, mlp_w_down, final_norm):
    b, t, d = x_prompt.shape
    sb, st, _ = x_sample.shape
    c = D_RNN
    n_tail = CONV_W - 1

    raw = {
        "norm_mix": norm_mix, "kv_norm": kv_norm, "final_norm": final_norm,
        "w_kv": w_kv, "attn_w_q": attn_w_q, "attn_w_o": attn_w_o, "mlp_w_down": mlp_w_down,
    }
    w_in = rec_w_in[0].astype(BF16)
    w = {}
    conv_w = rec_conv_w[0]
    conv_b = rec_conv_b[0].reshape(1, c)
    lam = rec_lambda[0].reshape(1, c)
    w_gates = jnp.concatenate([rec_gate_a_w[0], rec_gate_x_w[0]], axis=-1).astype(BF16)
    b_gates = jnp.concatenate([rec_gate_a_b[0], rec_gate_x_b[0]], axis=-1).reshape(N_LRU_BLOCKS, 1, 2 * LRU_BLOCK)
    sinks = attn_sinks[0].astype(F32)

    def attn_prompt(q, kv):
        o = _attn_prompt(q.reshape(b, t, Q_DIM), kv.reshape(b, t, 2 * KV_DIM), sinks)
        return o.reshape(b * t, Q_DIM)

    z_p, h_p, tail_p, w_out, w_up = _rec_front(
        x_prompt, norm_mix[0], w_in, conv_w, conv_b, w_gates, b_gates, lam,
        side_casts=[(_flat(rec_w_out), None), (_flat(mlp_w_up), norm_mlp.reshape(-1))])
    w["rec_w_out"] = w_out.reshape(rec_w_out.shape)
    w["mlp_w_up"] = w_up.reshape(mlp_w_up.shape)
    y_p, kv_p = _trunk(z_p.reshape(b * t, c), x_prompt.reshape(b * t, d), attn_prompt, raw, w)
    y_prompt = y_p.reshape(b, t, d)
    conv_p = jnp.swapaxes(tail_p.reshape(n_tail, b, c), 0, 1).reshape(b, 1, n_tail, c)
    keep = min(WINDOW, t)
    kv_tail = kv_p.reshape(b, t, 2 * KV_DIM)[:, t - keep:]
    new_k_prompt = kv_tail[..., :KV_DIM].reshape(b, keep, N_KV_HEADS, HEAD_DIM)
    new_v_prompt = kv_tail[..., KV_DIM:].reshape(b, keep, N_KV_HEADS, HEAD_DIM)

    conv_state_tb = jnp.swapaxes(state_conv[:, 0], 0, 1).reshape(n_tail * sb, c)
    h0 = state_h[:, 0]
    ck = cache_k.reshape(sb, WINDOW, KV_DIM)
    cv = cache_v.reshape(sb, WINDOW, KV_DIM)

    def attn_sample(q, kv):
        o = _attn_sample(q.reshape(st, sb, Q_DIM), kv.reshape(st, sb, 2 * KV_DIM), ck, cv, sinks)
        return o.reshape(st * sb, Q_DIM)

    x_s = jnp.swapaxes(x_sample, 0, 1).reshape(st * sb, d)
    gx_s = _mm("rec_in", _rmsnorm(x_s, norm_mix[0], BF16), w_in[None], out_dtype=F32)
    z_s, h_s = _lru(gx_s, conv_state_tb, h0, conv_w, conv_b, w_gates, b_gates, lam, n_batch=sb, n_steps=st)
    y_s, kv_s = _trunk(z_s, x_s, attn_sample, raw, w)
    y_sample = jnp.swapaxes(y_s.reshape(st, sb, d), 0, 1)
    conv_s = jnp.swapaxes(gx_s.reshape(st, sb, 2 * c)[st - n_tail:, :, c:], 0, 1).reshape(sb, 1, n_tail, c)
    kv_s = jnp.swapaxes(kv_s.reshape(st, sb, 2 * KV_DIM), 0, 1)
    new_k_sample = kv_s[..., :KV_DIM].reshape(sb, st, N_KV_HEADS, HEAD_DIM)
    new_v_sample = kv_s[..., KV_DIM:].reshape(sb, st, N_KV_HEADS, HEAD_DIM)

    return (y_prompt, y_sample, conv_p, h_p.reshape(b, 1, c), new_k_prompt, new_v_prompt,
            conv_s, h_s.reshape(sb, 1, c), new_k_sample, new_v_sample)
```

```python
import functools
import math

import jax
import jax.numpy as jnp
from jax import lax
from jax.experimental import pallas as pl
from jax.experimental.pallas import tpu as pltpu

F32 = jnp.float32
BF16 = jnp.bfloat16

D_MODEL = 2048
D_RNN = 2048
N_LRU_BLOCKS = 16
LRU_BLOCK = D_RNN // N_LRU_BLOCKS
CONV_W = 4
LRU_C = 8.0
N_HEADS = 32
N_KV_HEADS = 4
HEAD_DIM = 64
GROUP = N_HEADS // N_KV_HEADS
Q_DIM = N_HEADS * HEAD_DIM
KV_DIM = N_KV_HEADS * HEAD_DIM
WINDOW = 128
EPS = 1e-6
LOG2E = math.log2(math.e)

SUBLANES = 8
LANES = 128
VMEM_LIMIT_BYTES = 52 * 1024 * 1024


def _params(semantics):
    return pltpu.CompilerParams(dimension_semantics=semantics,
                                vmem_limit_bytes=VMEM_LIMIT_BYTES)


def _rmsnorm_body(x_ref, g_ref, o_ref):
    x = x_ref[...]
    y = x * lax.rsqrt(jnp.mean(x * x, axis=-1, keepdims=True) + EPS)
    o_ref[...] = (y * g_ref[...]).astype(o_ref.dtype)


def _rmsnorm(x, gain, out_dtype, tm=512):
    m, d = x.shape
    tm = min(tm, m)
    row_spec = pl.BlockSpec((tm, d), lambda i: (i, 0))
    return pl.pallas_call(
        _rmsnorm_body,
        grid=(m // tm,),
        in_specs=[row_spec, pl.BlockSpec((1, d), lambda i: (0, 0))],
        out_specs=row_spec,
        out_shape=jax.ShapeDtypeStruct((m, d), out_dtype),
        compiler_params=_params(("arbitrary",)),
        name="rmsnorm",
    )(x, gain.reshape(1, d).astype(F32))


PERM_STEPS = 32


def _perm_matrix(n_batch, to_time_major):
    n = n_batch * PERM_STEPS
    r = lax.broadcasted_iota(jnp.int32, (n, n), 0)
    c = lax.broadcasted_iota(jnp.int32, (n, n), 1)
    if to_time_major:
        src = (r % n_batch) * PERM_STEPS + r // n_batch
    else:
        src = (r % PERM_STEPS) * n_batch + r // PERM_STEPS
    return jnp.where(c == src, 1.0, 0.0).astype(BF16)


def _matmul_body(*refs, nk, act, has_scale, has_resid, has_copy, has_ssq, side_gains, norm_dim):
    refs = list(refs)
    x_ref, w_ref = refs[:2]
    pos = 2
    s_ref = r_ref = b_ref = q_ref = None
    if has_scale:
        s_ref = refs[pos]
        pos += 1
    if has_resid:
        r_ref = refs[pos]
        pos += 1
    side_in = []
    for has_gain in side_gains:
        side_in.append((refs[pos], refs[pos + 1] if has_gain else None))
        pos += 2 if has_gain else 1
    o_ref = refs[pos]
    pos += 1
    if has_copy:
        b_ref = refs[pos]
        pos += 1
    if has_ssq:
        q_ref = refs[pos]
        pos += 1
    side_out = refs[pos:pos + len(side_gains)]
    pos += len(side_gains)
    acc_ref = refs[pos] if nk > 1 else None

    def partial_product():
        for (src_ref, g_ref), dst_ref in zip(side_in, side_out):
            chunk = src_ref[...]
            if g_ref is not None:
                chunk = chunk * g_ref[...]
            dst_ref[...] = chunk.astype(dst_ref.dtype)
        return jnp.dot(x_ref[...], w_ref[...], preferred_element_type=F32)

    def finish(acc):
        if s_ref is not None:
            acc = acc * lax.rsqrt(jnp.sum(s_ref[...], axis=-1, keepdims=True) * (1.0 / norm_dim) + EPS)
        if act == "relu2":
            r = jnp.maximum(acc, 0.0)
            acc = r * r
        if r_ref is not None:
            acc = r_ref[...] + acc
        o_ref[...] = acc.astype(o_ref.dtype)
        if b_ref is not None:
            b_ref[...] = acc.astype(b_ref.dtype)
        if q_ref is not None:
            sq = acc * acc
            tot = sq[:, 0:LANES]
            for c in range(1, sq.shape[1] // LANES):
                tot = tot + sq[:, c * LANES:(c + 1) * LANES]
            q_ref[...] = tot

    if nk == 1:
        finish(partial_product())
    else:
        k = pl.program_id(2)

        @pl.when(k == 0)
        def _():
            acc_ref[...] = partial_product()

        @pl.when(jnp.logical_and(k > 0, k < nk - 1))
        def _():
            acc_ref[...] += partial_product()

        @pl.when(k == nk - 1)
        def _():
            finish(acc_ref[...] + partial_product())


def _matmul(x, w, *, out_dtype, layer=0, act=None, ssq_in=None, resid=None, bf16_copy=False, ssq_out=False,
            side_casts=(), cols_outer=False, bm=1024, bn=1024, bk=None, name="matmul"):
    m = x.shape[0]
    _, kdim, n = w.shape
    bm, bn = min(bm, m), min(bn, n)
    bk = kdim if bk is None else min(bk, kdim)
    nk, nb = kdim // bk, n // bn
    mb = m // bm
    n_steps = mb * nb * nk

    def spec(shape, index_map):
        if cols_outer:
            return pl.BlockSpec(shape, lambda j, i, k: index_map(i, j, k))
        return pl.BlockSpec(shape, index_map)

    in_specs = [spec((bm, bk), lambda i, j, k: (i, k)),
                spec((None, bk, bn), lambda i, j, k: (layer, k, j))]
    args = [x, w]
    if ssq_in is not None:
        in_specs.append(spec((bm, ssq_in.shape[1]), lambda i, j, k: (i, 0)))
        args.append(ssq_in)
    tile_spec = spec((bm, bn), lambda i, j, k: (i, j))
    if resid is not None:
        in_specs.append(tile_spec)
        args.append(resid)
    step = lambda i, j, k: (i * nb + j) * nk + k
    for src, gain in side_casts:
        rows, cols = src.shape
        chunk = rows // n_steps
        assert chunk * n_steps == rows and chunk % (2 * SUBLANES) == 0, (name, rows, n_steps)
        in_specs.append(spec((chunk, cols), lambda i, j, k: (step(i, j, k), 0)))
        args.append(src)
        if gain is not None:
            in_specs.append(spec((chunk, 1), lambda i, j, k: (step(i, j, k), 0)))
            args.append(gain.reshape(rows, 1).astype(F32))

    out_specs = [tile_spec]
    out_shape = [jax.ShapeDtypeStruct((m, n), out_dtype)]
    if bf16_copy:
        out_specs.append(tile_spec)
        out_shape.append(jax.ShapeDtypeStruct((m, n), BF16))
    if ssq_out:
        out_specs.append(spec((bm, LANES), lambda i, j, k: (i, j)))
        out_shape.append(jax.ShapeDtypeStruct((m, nb * LANES), F32))
    for src, _ in side_casts:
        rows, cols = src.shape
        out_specs.append(spec((rows // n_steps, cols), lambda i, j, k: (step(i, j, k), 0)))
        out_shape.append(jax.ShapeDtypeStruct((rows, cols), BF16))

    outs = pl.pallas_call(
        functools.partial(_matmul_body, nk=nk, act=act, has_scale=ssq_in is not None,
                          has_resid=resid is not None, has_copy=bf16_copy, has_ssq=ssq_out,
                          side_gains=tuple(g is not None for _, g in side_casts), norm_dim=kdim),
        grid=(nb, mb, nk) if cols_outer else (mb, nb, nk),
        in_specs=in_specs,
        out_specs=out_specs,
        out_shape=out_shape,
        scratch_shapes=[pltpu.VMEM((bm, bn), F32)] if nk > 1 else [],
        compiler_params=_params(("arbitrary", "arbitrary", "arbitrary")),
        name=name,
    )(*args)
    return outs[0] if len(outs) == 1 else outs


def _qkv_body(x_ref, wq_ref, wkv_ref, q_ref, kv_ref):
    xf = x_ref[...]
    scale = lax.rsqrt(jnp.mean(xf * xf, axis=-1, keepdims=True) + EPS)
    x = xf.astype(BF16)
    q = jnp.dot(x, wq_ref[...], preferred_element_type=F32)
    q_ref[...] = (q * (scale * (HEAD_DIM ** -0.5 * LOG2E))).astype(q_ref.dtype)
    kv_ref[...] = jnp.dot(x, wkv_ref[...], preferred_element_type=F32) * scale


def _qkv_proj(x, w_q, w_kv, bm=512):
    m, kdim = x.shape
    bm = min(bm, m)
    return pl.pallas_call(
        _qkv_body,
        grid=(m // bm,),
        in_specs=[pl.BlockSpec((bm, kdim), lambda i: (i, 0)),
                  pl.BlockSpec((kdim, Q_DIM), lambda i: (0, 0)),
                  pl.BlockSpec((kdim, 2 * KV_DIM), lambda i: (0, 0))],
        out_specs=[pl.BlockSpec((bm, Q_DIM), lambda i: (i, 0)),
                   pl.BlockSpec((bm, 2 * KV_DIM), lambda i: (i, 0))],
        out_shape=[jax.ShapeDtypeStruct((m, Q_DIM), BF16), jax.ShapeDtypeStruct((m, 2 * KV_DIM), F32)],
        compiler_params=_params(("arbitrary",)),
        name="qkv_proj",
    )(x, w_q, w_kv)


def _gelu_tanh(x):
    return x * (0.5 + 0.5 * jnp.tanh(0.7978845608028654 * (x + 0.044715 * (x * x * x))))


def _softplus(z):
    return jnp.maximum(z, 0.0) + jnp.log1p(jnp.exp(-jnp.abs(z)))


def _lru_decay(lam):
    return (-0.5 * LRU_C * LOG2E) * _softplus(-lam)


def _lru_columns(gate, x, tail, h, cw, cb, wg, bg, d, *, n_steps, n_batch):
    n_rows = n_steps * n_batch
    x_ext = jnp.concatenate([tail, x], axis=0)
    xc = cb + cw[0:1] * x_ext[0:n_rows]
    for j in range(1, CONV_W):
        xc = xc + cw[j:j + 1] * x_ext[j * n_batch:j * n_batch + n_rows]
    half_pre = 0.5 * (jnp.dot(xc.astype(BF16), wg, preferred_element_type=F32) + bg)
    tr = jnp.tanh(half_pre[:, :LRU_BLOCK])
    ti = jnp.tanh(half_pre[:, LRU_BLOCK:])
    a = jnp.exp2(d * tr + d)
    gain2 = 1.0 - a * a
    u = (gain2 * lax.rsqrt(jnp.maximum(gain2, 1e-30))) * ((0.5 + 0.5 * ti) * xc)
    hs = []
    for s in range(n_steps):
        rows = slice(s * n_batch, (s + 1) * n_batch)
        h = a[rows] * h + u[rows]
        hs.append(h)
    return _gelu_tanh(gate) * jnp.concatenate(hs, axis=0), x_ext[n_rows:], h


def _lru_body(gate_ref, xr_ref, cs_ref, h0_ref, cw_ref, cb_ref, wg_ref, bg_ref, lam_ref,
              z_ref, ht_ref, h_sc, tail_sc, *, n_steps, n_batch, tc_cols):
    t = pl.program_id(1)

    @pl.when(t == 0)
    def _():
        h_sc[...] = h0_ref[...]
        tail_sc[...] = cs_ref[...]

    decay = _lru_decay(lam_ref[...])
    for n in range(tc_cols // LRU_BLOCK):
        cols = slice(n * LRU_BLOCK, (n + 1) * LRU_BLOCK)
        z, tail_sc[:, cols], h_sc[:, cols] = _lru_columns(
            gate_ref[:, cols], xr_ref[:, cols], tail_sc[:, cols], h_sc[:, cols], cw_ref[:, cols], cb_ref[:, cols],
            wg_ref[n], bg_ref[n], decay[:, cols], n_steps=n_steps, n_batch=n_batch)
        z_ref[:, cols] = z.astype(z_ref.dtype)

    @pl.when(t == pl.num_programs(1) - 1)
    def _():
        ht_ref[...] = h_sc[...]


def _lru(gx, conv_state, h0, conv_w, conv_b, w_gates, b_gates, lam, *, n_batch, n_steps, tc_cols=512):
    m = gx.shape[0]
    c = D_RNN
    n_rows = n_steps * n_batch
    n_tail = (CONV_W - 1) * n_batch
    ncb = c // tc_cols
    gpb = tc_cols // LRU_BLOCK
    z_spec = pl.BlockSpec((n_rows, tc_cols), lambda ci, ti: (ti, ci))
    z_shape = jax.ShapeDtypeStruct((m, c), BF16)
    z, ht = pl.pallas_call(
        functools.partial(_lru_body, n_steps=n_steps, n_batch=n_batch, tc_cols=tc_cols),
        grid=(ncb, m // n_rows),
        in_specs=[
            pl.BlockSpec((n_rows, tc_cols), lambda ci, ti: (ti, ci)),
            pl.BlockSpec((n_rows, tc_cols), lambda ci, ti: (ti, ncb + ci)),
            pl.BlockSpec((n_tail, tc_cols), lambda ci, ti: (0, ci)),
            pl.BlockSpec((n_batch, tc_cols), lambda ci, ti: (0, ci)),
            pl.BlockSpec((CONV_W, tc_cols), lambda ci, ti: (0, ci)),
            pl.BlockSpec((1, tc_cols), lambda ci, ti: (0, ci)),
            pl.BlockSpec((gpb, LRU_BLOCK, 2 * LRU_BLOCK), lambda ci, ti: (ci, 0, 0)),
            pl.BlockSpec((gpb, 1, 2 * LRU_BLOCK), lambda ci, ti: (ci, 0, 0)),
            pl.BlockSpec((1, tc_cols), lambda ci, ti: (0, ci)),
        ],
        out_specs=[z_spec, pl.BlockSpec((n_batch, tc_cols), lambda ci, ti: (0, ci))],
        out_shape=[z_shape, jax.ShapeDtypeStruct((n_batch, c), F32)],
        scratch_shapes=[pltpu.VMEM((n_batch, tc_cols), F32), pltpu.VMEM((n_tail, tc_cols), F32)],
        compiler_params=_params(("arbitrary", "arbitrary")),
        name="lru",
    )(gx, gx, conv_state, h0, conv_w, conv_b, w_gates, b_gates, lam)
    return z, ht


FRONT_STEPS = 2 * PERM_STEPS


def _rec_front_body(*refs, n_batch, side_gains):
    x_ref, norm_gain_ref, win_ref, cw_ref, cb_ref, wg_ref, bg_ref, lam_ref = refs[:8]
    pos = 8
    side_in = []
    for has_gain in side_gains:
        side_in.append((refs[pos], refs[pos + 1] if has_gain else None))
        pos += 2 if has_gain else 1
    z_ref, ht_ref, tail_ref = refs[pos:pos + 3]
    pos += 3
    side_out = refs[pos:pos + len(side_gains)]
    h_sc, tail_sc = refs[pos + len(side_gains):]
    s = pl.program_id(0)
    c = D_RNN

    @pl.when(s == 0)
    def _():
        h_sc[...] = jnp.zeros_like(h_sc)
        tail_sc[...] = jnp.zeros_like(tail_sc)

    for (src_ref, g_ref), dst_ref in zip(side_in, side_out):
        chunk = src_ref[...]
        if g_ref is not None:
            chunk = chunk * g_ref[...]
        dst_ref[...] = chunk.astype(dst_ref.dtype)

    p_rows = n_batch * PERM_STEPS
    n_tiles = FRONT_STEPS // PERM_STEPS
    to_tb = _perm_matrix(n_batch, True)
    xn = []
    for tile in range(n_tiles):
        x = x_ref[:, tile * PERM_STEPS:(tile + 1) * PERM_STEPS, :].reshape(p_rows, x_ref.shape[-1])
        y = x * lax.rsqrt(jnp.mean(x * x, axis=-1, keepdims=True) + EPS)
        xn.append(jnp.dot(to_tb, (y * norm_gain_ref[...]).astype(BF16), preferred_element_type=F32).astype(BF16))
    gx = jnp.dot(jnp.concatenate(xn, axis=0), win_ref[...], preferred_element_type=F32)

    to_bt = _perm_matrix(n_batch, False)
    decay = _lru_decay(lam_ref[...])
    for n in range(N_LRU_BLOCKS):
        cols = slice(n * LRU_BLOCK, (n + 1) * LRU_BLOCK)
        z, tail_sc[:, cols], h_sc[:, cols] = _lru_columns(
            gx[:, n * LRU_BLOCK:(n + 1) * LRU_BLOCK], gx[:, c + n * LRU_BLOCK:c + (n + 1) * LRU_BLOCK],
            tail_sc[:, cols], h_sc[:, cols], cw_ref[:, cols], cb_ref[:, cols], wg_ref[n], bg_ref[n], decay[:, cols],
            n_steps=FRONT_STEPS, n_batch=n_batch)
        z = z.astype(BF16)
        for tile in range(n_tiles):
            zp = jnp.dot(to_bt, z[tile * p_rows:(tile + 1) * p_rows], preferred_element_type=F32)
            z_ref[:, tile * PERM_STEPS:(tile + 1) * PERM_STEPS, cols] = (
                zp.astype(z_ref.dtype).reshape(n_batch, PERM_STEPS, LRU_BLOCK))

    @pl.when(s == pl.num_programs(0) - 1)
    def _():
        ht_ref[...] = h_sc[...]
        tail_ref[...] = tail_sc[...]


def _rec_front(x3, gain, w_in, conv_w, conv_b, w_gates, b_gates, lam, *, side_casts=()):
    n_batch, t, d = x3.shape
    c = D_RNN
    n_blocks = t // FRONT_STEPS
    n_tail = (CONV_W - 1) * n_batch
    const = lambda *shape: pl.BlockSpec(shape, lambda s: (0,) * len(shape))

    in_specs = [pl.BlockSpec((n_batch, FRONT_STEPS, d), lambda s: (0, s, 0)), const(1, d),
                pl.BlockSpec((d, 2 * c), lambda s: (0, 0), pipeline_mode=pl.Buffered(1)),
                const(CONV_W, c), const(1, c), const(N_LRU_BLOCKS, LRU_BLOCK, 2 * LRU_BLOCK),
                const(N_LRU_BLOCKS, 1, 2 * LRU_BLOCK), const(1, c)]
    args = [x3, gain.reshape(1, d).astype(F32), w_in, conv_w, conv_b, w_gates, b_gates, lam]
    out_specs = [pl.BlockSpec((n_batch, FRONT_STEPS, c), lambda s: (0, s, 0)), const(n_batch, c), const(n_tail, c)]
    out_shape = [jax.ShapeDtypeStruct((n_batch, t, c), BF16),
                 jax.ShapeDtypeStruct((n_batch, c), F32), jax.ShapeDtypeStruct((n_tail, c), F32)]
    for src, gain in side_casts:
        rows, cols = src.shape
        chunk = rows // n_blocks
        assert chunk * n_blocks == rows and chunk % (2 * SUBLANES) == 0, (rows, n_blocks)
        in_specs.append(pl.BlockSpec((chunk, cols), lambda s: (s, 0)))
        args.append(src)
        if gain is not None:
            in_specs.append(pl.BlockSpec((chunk, 1), lambda s: (s, 0)))
            args.append(gain.reshape(rows, 1).astype(F32))
        out_specs.append(pl.BlockSpec((chunk, cols), lambda s: (s, 0)))
        out_shape.append(jax.ShapeDtypeStruct((rows, cols), BF16))

    return pl.pallas_call(
        functools.partial(_rec_front_body, n_batch=n_batch, side_gains=tuple(g is not None for _, g in side_casts)),
        grid=(n_blocks,),
        in_specs=in_specs,
        out_specs=out_specs,
        out_shape=out_shape,
        scratch_shapes=[pltpu.VMEM((n_batch, c), F32), pltpu.VMEM((n_tail, c), F32)],
        compiler_params=_params(("arbitrary",)),
        name="rec_front",
    )(*args)


ATTN_BLOCKS = 4


def _attn_prompt_body(sink_ref, q_ref, kvc_ref, kvp_ref, o_ref):
    i = pl.program_id(1)
    for blk in range(ATTN_BLOCKS):
        rows = slice(blk * WINDOW, (blk + 1) * WINDOW)
        prev = kvp_ref[...] if blk == 0 else kvc_ref[(blk - 1) * WINDOW:blk * WINDOW]
        kv = jnp.concatenate([prev, kvc_ref[rows]], axis=0).astype(BF16)
        _attend_block(sink_ref, q_ref, o_ref, rows, kv, jnp.where(i == 0, -jnp.inf, 0.0) if blk == 0 else None)


def _attend_block(sink_ref, q_ref, o_ref, rows, kv, prev_bias):
    n_keys = 2 * WINDOW
    low_k = lax.broadcasted_iota(jnp.int32, (n_keys, LANES), 1) < HEAD_DIM
    low_q = lax.broadcasted_iota(jnp.int32, (WINDOW, LANES), 1) < HEAD_DIM
    zero = jnp.zeros((n_keys, LANES), BF16)
    one = jnp.ones((n_keys, LANES), BF16)
    from_prev = (lax.broadcasted_iota(jnp.int32, (WINDOW, WINDOW), 1)
                 > lax.broadcasted_iota(jnp.int32, (WINDOW, WINDOW), 0))
    dims = (((1,), (1,)), ((), ()))

    n_pairs = GROUP // 2

    def score_head(h):
        c0 = (h // 2) * LANES
        k_t = kv[:, c0:c0 + LANES]
        v_t = kv[:, KV_DIM + c0:KV_DIM + c0 + LANES]
        k_sw = pltpu.roll(k_t, HEAD_DIM, 1)
        v_sw = pltpu.roll(v_t, HEAD_DIM, 1)
        k_lo, k_hi = (k_t, k_sw) if h % 2 == 0 else (k_sw, k_t)
        v_lo, v_hi = (v_t, v_sw) if h % 2 == 0 else (v_sw, v_t)
        k_a = jnp.where(low_k, k_lo, zero)
        k_b = jnp.where(low_k, zero, k_hi)
        v_a = jnp.where(low_k, v_lo, one)
        v_b = jnp.where(low_k, one, v_hi)
        qc0 = h * GROUP * HEAD_DIM
        q_h = jnp.concatenate([q_ref[rows, qc0 + p * LANES:qc0 + (p + 1) * LANES] for p in range(n_pairs)], axis=0)
        scores = [lax.dot_general(q_h, k_x, dims, preferred_element_type=F32) for k_x in (k_a, k_b)]
        return scores, (v_a, v_b)

    for h in range(N_KV_HEADS):
        scores, values = score_head(h)
        qc0 = h * GROUP * HEAD_DIM
        acc, esink = [], []
        for side, v_x in enumerate(values):
            s = scores[side]
            probs, es = [], []
            for p in range(n_pairs):
                s_p = s[p * WINDOW:(p + 1) * WINDOW]
                s_prev = s_p[:, :WINDOW] if prev_bias is None else s_p[:, :WINDOW] + prev_bias
                s_v = jnp.where(from_prev, s_prev, s_p[:, WINDOW:])
                sink = sink_ref[h * GROUP + 2 * p + side] * LOG2E
                m = jnp.maximum(jnp.max(s_v, axis=-1, keepdims=True), sink)
                e = jnp.exp2(s_v - m)
                probs.append(jnp.concatenate([jnp.where(from_prev, e, 0.0), jnp.where(from_prev, 0.0, e)],
                                             axis=1).astype(BF16))
                es.append(jnp.exp2(sink - m))
            acc.append(jnp.dot(jnp.concatenate(probs, axis=0), v_x, preferred_element_type=F32))
            esink.append(es)
        for p in range(n_pairs):
            part = slice(p * WINDOW, (p + 1) * WINDOW)
            acc_a, acc_b = acc[0][part], acc[1][part]
            num = jnp.where(low_q, acc_a, acc_b)
            den = pltpu.roll(jnp.where(low_q, acc_b, acc_a), HEAD_DIM, 1) + jnp.where(low_q, esink[0][p], esink[1][p])
            o_ref[rows, qc0 + p * LANES:qc0 + (p + 1) * LANES] = (num / den).astype(o_ref.dtype)


def _attn_prompt(q3, kv3, sinks):
    b, t, _ = q3.shape
    step_rows = ATTN_BLOCKS * WINDOW
    nb = t // step_rows
    return pl.pallas_call(
        _attn_prompt_body,
        grid=(b, nb),
        in_specs=[
            pl.BlockSpec(memory_space=pltpu.SMEM),
            pl.BlockSpec((None, step_rows, Q_DIM), lambda bi, i: (bi, i, 0)),
            pl.BlockSpec((None, step_rows, 2 * KV_DIM), lambda bi, i: (bi, i, 0)),
            pl.BlockSpec((None, WINDOW, 2 * KV_DIM), lambda bi, i: (bi, jnp.maximum(ATTN_BLOCKS * i - 1, 0), 0)),
        ],
        out_specs=pl.BlockSpec((None, step_rows, Q_DIM), lambda bi, i: (bi, i, 0)),
        out_shape=jax.ShapeDtypeStruct((b, t, Q_DIM), BF16),
        compiler_params=_params(("arbitrary", "arbitrary")),
        name="attn_prompt",
    )(sinks, q3, kv3, kv3)


SAMPLE_BATCH_TILE = 8


def _attn_sample_body(sink_ref, q_ref, kvn_ref, ck_ref, cv_ref, o_ref, *, n_steps):
    bb = SAMPLE_BATCH_TILE
    n_new = n_steps * bb
    n_cache = bb * WINDOW
    n_keys = n_cache + n_new
    n_keys_pad = -(-n_keys // LANES) * LANES
    q = q_ref[...].reshape(n_new, Q_DIM)
    kvn = kvn_ref[...].reshape(n_new, 2 * KV_DIM)
    pad = jnp.zeros((n_keys_pad - n_keys, KV_DIM), F32)
    k_all = jnp.concatenate([ck_ref[...].reshape(n_cache, KV_DIM), kvn[:, :KV_DIM], pad], axis=0).astype(BF16)
    v_all = jnp.concatenate([cv_ref[...].reshape(n_cache, KV_DIM), kvn[:, KV_DIM:], pad], axis=0).astype(BF16)
    n_rows = GROUP * n_new
    r = lax.broadcasted_iota(jnp.int32, (n_rows, n_keys_pad), 0)
    c = lax.broadcasted_iota(jnp.int32, (n_rows, n_keys_pad), 1)
    b_r = r % bb
    t_r = (r // bb) % n_steps
    is_cache = c < n_cache
    c2 = jnp.maximum(c - n_cache, 0)
    b_c = jnp.where(is_cache, c // WINDOW, c2 % bb)
    k_step = jnp.where(is_cache, c % WINDOW - WINDOW, c2 // bb)
    dq = t_r - k_step
    allowed = jnp.logical_and(jnp.logical_and(b_c == b_r, c < n_keys),
                              jnp.logical_and(dq >= 0, dq < WINDOW))
    grp = lax.broadcasted_iota(jnp.int32, (n_rows, 1), 0) // n_new
    dims = (((1,), (1,)), ((), ()))
    for h in range(N_KV_HEADS):
        k_h = k_all[:, h * HEAD_DIM:(h + 1) * HEAD_DIM]
        v_h = v_all[:, h * HEAD_DIM:(h + 1) * HEAD_DIM]
        q_h = jnp.concatenate(
            [q[:, (h * GROUP + g) * HEAD_DIM:(h * GROUP + g + 1) * HEAD_DIM] for g in range(GROUP)], axis=0)
        sink = jnp.zeros((n_rows, 1), F32)
        for g in range(GROUP):
            sink = jnp.where(grp == g, sink_ref[h * GROUP + g] * LOG2E, sink)
        s = lax.dot_general(q_h, k_h, dims, preferred_element_type=F32)
        s = jnp.where(allowed, s, -jnp.inf)
        m = jnp.maximum(jnp.max(s, axis=-1, keepdims=True), sink)
        p = jnp.exp2(s - m)
        denom = jnp.sum(p, axis=-1, keepdims=True) + jnp.exp2(sink - m)
        o_h = jnp.dot(p.astype(BF16), v_h, preferred_element_type=F32) / denom
        for g in range(GROUP):
            c0 = (h * GROUP + g) * HEAD_DIM
            o_ref[:, :, c0:c0 + HEAD_DIM] = (
                o_h[g * n_new:(g + 1) * n_new].reshape(n_steps, bb, HEAD_DIM).astype(o_ref.dtype))


def _attn_sample(q3, kvn3, cache_k, cache_v, sinks):
    n_steps, b, _ = q3.shape
    bb = SAMPLE_BATCH_TILE
    return pl.pallas_call(
        functools.partial(_attn_sample_body, n_steps=n_steps),
        grid=(b // bb,),
        in_specs=[
            pl.BlockSpec(memory_space=pltpu.SMEM),
            pl.BlockSpec((n_steps, bb, Q_DIM), lambda i: (0, i, 0)),
            pl.BlockSpec((n_steps, bb, 2 * KV_DIM), lambda i: (0, i, 0)),
            pl.BlockSpec((bb, WINDOW, KV_DIM), lambda i: (i, 0, 0)),
            pl.BlockSpec((bb, WINDOW, KV_DIM), lambda i: (i, 0, 0)),
        ],
        out_specs=pl.BlockSpec((n_steps, bb, Q_DIM), lambda i: (0, i, 0)),
        out_shape=jax.ShapeDtypeStruct((n_steps, b, Q_DIM), BF16),
        compiler_params=_params(("arbitrary",)),
        name="attn_sample",
    )(sinks, q3, kvn3, cache_k, cache_v)


TILES = {
    "rec_in": dict(bm=512, bn=2048),
    "rec_out": dict(bm=512, bn=2048),
    "attn_out": dict(bm=512, bn=2048),
    "mlp_up": dict(bm=1024, bn=2048),
    "mlp_down": dict(bm=1024, bn=1024, bk=2048),
}


def _mm(name, x, w, **kw):
    return _matmul(x, w, name=name, **TILES[name], **kw)


def _flat(w):
    return w.reshape(-1, w.shape[-1])


def _trunk(z, x, attn_fn, raw, w):
    h, hb, ssq = _mm("rec_out", z, w["rec_w_out"], out_dtype=F32, resid=x, bf16_copy=True, ssq_out=True)
    if "mlp_w_down" not in w:
        hid, w_down, w_kv, w_q, w_o = _mm("mlp_up", hb, w["mlp_w_up"], layer=0, out_dtype=BF16, act="relu2",
                                          ssq_in=ssq, side_casts=[
            (_flat(raw["mlp_w_down"]), None), (raw["w_kv"], raw["kv_norm"]),
            (_flat(raw["attn_w_q"]), raw["norm_mix"][1]), (_flat(raw["attn_w_o"]), None)])
        w["mlp_w_down"] = w_down.reshape(raw["mlp_w_down"].shape)
        w["w_kv"] = w_kv
        w["attn_w_q"] = w_q
        w["attn_w_o"] = w_o.reshape(raw["attn_w_o"].shape)
    else:
        hid = _mm("mlp_up", hb, w["mlp_w_up"], layer=0, out_dtype=BF16, act="relu2", ssq_in=ssq)
    h = _mm("mlp_down", hid, w["mlp_w_down"], layer=0, out_dtype=F32, resid=h)
    q, kv = _qkv_proj(h, w["attn_w_q"], w["w_kv"])
    o = attn_fn(q, kv)
    h, hb, ssq = _mm("attn_out", o, w["attn_w_o"], out_dtype=F32, resid=h, bf16_copy=True, ssq_out=True)
    hid = _mm("mlp_up", hb, w["mlp_w_up"], layer=1, out_dtype=BF16, act="relu2", ssq_in=ssq)
    h = _mm("mlp_down", hid, w["mlp_w_down"], layer=1, out_dtype=F32, resid=h)
    return _rmsnorm(h, raw["final_norm"], F32), kv


def kernel(x_prompt, x_sample, state_conv, state_h, cache_k, cache_v, norm_mix, norm_mlp, rec_w_in,
           rec_conv_w, rec_conv_b, rec_gate_a_w, rec_gate_a_b, rec_gate_x_w, rec_gate_x_b, rec_lambda,
           rec_w_out, kv_norm, w_kv, attn_w_q, attn_sinks, attn_w_o, mlp_w_up, mlp_w_down, final_norm):
    b, t, d = x_prompt.shape
    sb, st, _ = x_sample.shape
    c = D_RNN
    n_tail = CONV_W - 1

    raw = {
        "norm_mix": norm_mix, "kv_norm": kv_norm, "final_norm": final_norm,
        "w_kv": w_kv, "attn_w_q": attn_w_q, "attn_w_o": attn_w_o, "mlp_w_down": mlp_w_down,
    }
    w_in = rec_w_in[0].astype(BF16)
    w = {}
    conv_w = rec_conv_w[0]
    conv_b = rec_conv_b[0].reshape(1, c)
    lam = rec_lambda[0].reshape(1, c)
    w_gates = jnp.concatenate([rec_gate_a_w[0], rec_gate_x_w[0]], axis=-1).astype(BF16)
    b_gates = jnp.concatenate([rec_gate_a_b[0], rec_gate_x_b[0]], axis=-1).reshape(N_LRU_BLOCKS, 1, 2 * LRU_BLOCK)
    sinks = attn_sinks[0].astype(F32)

    def attn_prompt(q, kv):
        o = _attn_prompt(q.reshape(b, t, Q_DIM), kv.reshape(b, t, 2 * KV_DIM), sinks)
        return o.reshape(b * t, Q_DIM)

    z_p, h_p, tail_p, w_out, w_up = _rec_front(
        x_prompt, norm_mix[0], w_in, conv_w, conv_b, w_gates, b_gates, lam,
        side_casts=[(_flat(rec_w_out), None), (_flat(mlp_w_up), norm_mlp.reshape(-1))])
    w["rec_w_out"] = w_out.reshape(rec_w_out.shape)
    w["mlp_w_up"] = w_up.reshape(mlp_w_up.shape)
    y_p, kv_p = _trunk(z_p.reshape(b * t, c), x_prompt.reshape(b * t, d), attn_prompt, raw, w)
    y_prompt = y_p.reshape(b, t, d)
    conv_p = jnp.swapaxes(tail_p.reshape(n_tail, b, c), 0, 1).reshape(b, 1, n_tail, c)
    keep = min(WINDOW, t)
    kv_tail = kv_p.reshape(b, t, 2 * KV_DIM)[:, t - keep:]
    new_k_prompt = kv_tail[..., :KV_DIM].reshape(b, keep, N_KV_HEADS, HEAD_DIM)
    new_v_prompt = kv_tail[..., KV_DIM:].reshape(b, keep, N_KV_HEADS, HEAD_DIM)

    conv_state_tb = jnp.swapaxes(state_conv[:, 0], 0, 1).reshape(n_tail * sb, c)
    h0 = state_h[:, 0]
    ck = cache_k.reshape(sb, WINDOW, KV_DIM)
    cv = cache_v.reshape(sb, WINDOW, KV_DIM)

    def attn_sample(q, kv):
        o = _attn_sample(q.reshape(st, sb, Q_DIM), kv.reshape(st, sb, 2 * KV_DIM), ck, cv, sinks)
        return o.reshape(st * sb, Q_DIM)

    x_s = jnp.swapaxes(x_sample, 0, 1).reshape(st * sb, d)
    gx_s = _mm("rec_in", _rmsnorm(x_s, norm_mix[0], BF16), w_in[None], out_dtype=F32)
    z_s, h_s = _lru(gx_s, conv_state_tb, h0, conv_w, conv_b, w_gates, b_gates, lam, n_batch=sb, n_steps=st)
    y_s, kv_s = _trunk(z_s, x_s, attn_sample, raw, w)
    y_sample = jnp.swapaxes(y_s.reshape(st, sb, d), 0, 1)
    conv_s = jnp.swapaxes(gx_s.reshape(st, sb, 2 * c)[st - n_tail:, :, c:], 0, 1).reshape(sb, 1, n_tail, c)
    kv_s = jnp.swapaxes(kv_s.reshape(st, sb, 2 * KV_DIM), 0, 1)
    new_k_sample = kv_s[..., :KV_DIM].reshape(sb, st, N_KV_HEADS, HEAD_DIM)
    new_v_sample = kv_s[..., KV_DIM:].reshape(sb, st, N_KV_HEADS, HEAD_DIM)

    return (y_prompt, y_sample, conv_p, h_p.reshape(b, 1, c), new_k_prompt, new_v_prompt,
            conv_s, h_s.reshape(sb, 1, c), new_k_sample, new_v_sample)
```

```python
import functools
import math

import jax
import jax.numpy as jnp
from jax import lax
from jax.experimental import pallas as pl
from jax.experimental.pallas import tpu as pltpu

F32 = jnp.float32
BF16 = jnp.bfloat16

D_MODEL = 2048
D_RNN = 2048
N_LRU_BLOCKS = 16
LRU_BLOCK = D_RNN // N_LRU_BLOCKS
CONV_W = 4
LRU_C = 8.0
N_HEADS = 32
N_KV_HEADS = 4
HEAD_DIM = 64
GROUP = N_HEADS // N_KV_HEADS
Q_DIM = N_HEADS * HEAD_DIM
KV_DIM = N_KV_HEADS * HEAD_DIM
WINDOW = 128
EPS = 1e-6
LOG2E = math.log2(math.e)

SUBLANES = 8
LANES = 128
VMEM_LIMIT_BYTES = 52 * 1024 * 1024


def _params(semantics):
    return pltpu.CompilerParams(dimension_semantics=semantics,
                                vmem_limit_bytes=VMEM_LIMIT_BYTES)


def _rmsnorm_body(x_ref, g_ref, o_ref):
    x = x_ref[...]
    y = x * lax.rsqrt(jnp.mean(x * x, axis=-1, keepdims=True) + EPS)
    o_ref[...] = (y * g_ref[...]).astype(o_ref.dtype)


def _rmsnorm(x, gain, out_dtype, tm=512):
    m, d = x.shape
    tm = min(tm, m)
    row_spec = pl.BlockSpec((tm, d), lambda i: (i, 0))
    return pl.pallas_call(
        _rmsnorm_body,
        grid=(m // tm,),
        in_specs=[row_spec, pl.BlockSpec((1, d), lambda i: (0, 0))],
        out_specs=row_spec,
        out_shape=jax.ShapeDtypeStruct((m, d), out_dtype),
        compiler_params=_params(("arbitrary",)),
        name="rmsnorm",
    )(x, gain.reshape(1, d).astype(F32))


PERM_STEPS = 32


def _perm_matrix(n_batch, to_time_major):
    n = n_batch * PERM_STEPS
    r = lax.broadcasted_iota(jnp.int32, (n, n), 0)
    c = lax.broadcasted_iota(jnp.int32, (n, n), 1)
    if to_time_major:
        src = (r % n_batch) * PERM_STEPS + r // n_batch
    else:
        src = (r % PERM_STEPS) * n_batch + r // PERM_STEPS
    return jnp.where(c == src, 1.0, 0.0).astype(BF16)


def _matmul_body(*refs, nk, act, has_scale, has_resid, has_copy, has_ssq, side_gains, norm_dim):
    refs = list(refs)
    x_ref, w_ref = refs[:2]
    pos = 2
    s_ref = r_ref = b_ref = q_ref = None
    if has_scale:
        s_ref = refs[pos]
        pos += 1
    if has_resid:
        r_ref = refs[pos]
        pos += 1
    side_in = []
    for has_gain in side_gains:
        side_in.append((refs[pos], refs[pos + 1] if has_gain else None))
        pos += 2 if has_gain else 1
    o_ref = refs[pos]
    pos += 1
    if has_copy:
        b_ref = refs[pos]
        pos += 1
    if has_ssq:
        q_ref = refs[pos]
        pos += 1
    side_out = refs[pos:pos + len(side_gains)]
    pos += len(side_gains)
    acc_ref = refs[pos] if nk > 1 else None

    def partial_product():
        for (src_ref, g_ref), dst_ref in zip(side_in, side_out):
            chunk = src_ref[...]
            if g_ref is not None:
                chunk = chunk * g_ref[...]
            dst_ref[...] = chunk.astype(dst_ref.dtype)
        return jnp.dot(x_ref[...], w_ref[...], preferred_element_type=F32)

    def finish(acc):
        if s_ref is not None:
            acc = acc * lax.rsqrt(jnp.sum(s_ref[...], axis=-1, keepdims=True) * (1.0 / norm_dim) + EPS)
        if act == "relu2":
            r = jnp.maximum(acc, 0.0)
            acc = r * r
        if r_ref is not None:
            acc = r_ref[...] + acc
        o_ref[...] = acc.astype(o_ref.dtype)
        if b_ref is not None:
            b_ref[...] = acc.astype(b_ref.dtype)
        if q_ref is not None:
            sq = acc * acc
            tot = sq[:, 0:LANES]
            for c in range(1, sq.shape[1] // LANES):
                tot = tot + sq[:, c * LANES:(c + 1) * LANES]
            q_ref[...] = tot

    if nk == 1:
        finish(partial_product())
    else:
        k = pl.program_id(2)

        @pl.when(k == 0)
        def _():
            acc_ref[...] = partial_product()

        @pl.when(jnp.logical_and(k > 0, k < nk - 1))
        def _():
            acc_ref[...] += partial_product()

        @pl.when(k == nk - 1)
        def _():
            finish(acc_ref[...] + partial_product())


def _matmul(x, w, *, out_dtype, layer=0, act=None, ssq_in=None, resid=None, bf16_copy=False, ssq_out=False,
            side_casts=(), cols_outer=False, bm=1024, bn=1024, bk=None, name="matmul"):
    m = x.shape[0]
    _, kdim, n = w.shape
    bm, bn = min(bm, m), min(bn, n)
    bk = kdim if bk is None else min(bk, kdim)
    nk, nb = kdim // bk, n // bn
    mb = m // bm
    n_steps = mb * nb * nk

    def spec(shape, index_map):
        if cols_outer:
            return pl.BlockSpec(shape, lambda j, i, k: index_map(i, j, k))
        return pl.BlockSpec(shape, index_map)

    in_specs = [spec((bm, bk), lambda i, j, k: (i, k)),
                spec((None, bk, bn), lambda i, j, k: (layer, k, j))]
    args = [x, w]
    if ssq_in is not None:
        in_specs.append(spec((bm, ssq_in.shape[1]), lambda i, j, k: (i, 0)))
        args.append(ssq_in)
    tile_spec = spec((bm, bn), lambda i, j, k: (i, j))
    if resid is not None:
        in_specs.append(tile_spec)
        args.append(resid)
    step = lambda i, j, k: (i * nb + j) * nk + k
    for src, gain in side_casts:
        rows, cols = src.shape
        chunk = rows // n_steps
        assert chunk * n_steps == rows and chunk % (2 * SUBLANES) == 0, (name, rows, n_steps)
        in_specs.append(spec((chunk, cols), lambda i, j, k: (step(i, j, k), 0)))
        args.append(src)
        if gain is not None:
            in_specs.append(spec((chunk, 1), lambda i, j, k: (step(i, j, k), 0)))
            args.append(gain.reshape(rows, 1).astype(F32))

    out_specs = [tile_spec]
    out_shape = [jax.ShapeDtypeStruct((m, n), out_dtype)]
    if bf16_copy:
        out_specs.append(tile_spec)
        out_shape.append(jax.ShapeDtypeStruct((m, n), BF16))
    if ssq_out:
        out_specs.append(spec((bm, LANES), lambda i, j, k: (i, j)))
        out_shape.append(jax.ShapeDtypeStruct((m, nb * LANES), F32))
    for src, _ in side_casts:
        rows, cols = src.shape
        out_specs.append(spec((rows // n_steps, cols), lambda i, j, k: (step(i, j, k), 0)))
        out_shape.append(jax.ShapeDtypeStruct((rows, cols), BF16))

    outs = pl.pallas_call(
        functools.partial(_matmul_body, nk=nk, act=act, has_scale=ssq_in is not None,
                          has_resid=resid is not None, has_copy=bf16_copy, has_ssq=ssq_out,
                          side_gains=tuple(g is not None for _, g in side_casts), norm_dim=kdim),
        grid=(nb, mb, nk) if cols_outer else (mb, nb, nk),
        in_specs=in_specs,
        out_specs=out_specs,
        out_shape=out_shape,
        scratch_shapes=[pltpu.VMEM((bm, bn), F32)] if nk > 1 else [],
        compiler_params=_params(("arbitrary", "arbitrary", "arbitrary")),
        name=name,
    )(*args)
    return outs[0] if len(outs) == 1 else outs


def _qkv_body(x_ref, wq_ref, wkv_ref, q_ref, kv_ref):
    xf = x_ref[...]
    scale = lax.rsqrt(jnp.mean(xf * xf, axis=-1, keepdims=True) + EPS)
    x = xf.astype(BF16)
    q = jnp.dot(x, wq_ref[...], preferred_element_type=F32)
    q_ref[...] = (q * (scale * (HEAD_DIM ** -0.5 * LOG2E))).astype(q_ref.dtype)
    kv_ref[...] = jnp.dot(x, wkv_ref[...], preferred_element_type=F32) * scale


def _qkv_proj(x, w_q, w_kv, bm=512):
    m, kdim = x.shape
    bm = min(bm, m)
    return pl.pallas_call(
        _qkv_body,
        grid=(m // bm,),
        in_specs=[pl.BlockSpec((bm, kdim), lambda i: (i, 0)),
                  pl.BlockSpec((kdim, Q_DIM), lambda i: (0, 0)),
                  pl.BlockSpec((kdim, 2 * KV_DIM), lambda i: (0, 0))],
        out_specs=[pl.BlockSpec((bm, Q_DIM), lambda i: (i, 0)),
                   pl.BlockSpec((bm, 2 * KV_DIM), lambda i: (i, 0))],
        out_shape=[jax.ShapeDtypeStruct((m, Q_DIM), BF16), jax.ShapeDtypeStruct((m, 2 * KV_DIM), F32)],
        compiler_params=_params(("arbitrary",)),
        name="qkv_proj",
    )(x, w_q, w_kv)


def _gelu_tanh(x):
    return x * (0.5 + 0.5 * jnp.tanh(0.7978845608028654 * (x + 0.044715 * (x * x * x))))


def _softplus(z):
    return jnp.maximum(z, 0.0) + jnp.log1p(jnp.exp(-jnp.abs(z)))


def _lru_decay(lam):
    return (-0.5 * LRU_C * LOG2E) * _softplus(-lam)


def _lru_columns(gate, x, tail, h, cw, cb, wg, bg, d, *, n_steps, n_batch):
    n_rows = n_steps * n_batch
    x_ext = jnp.concatenate([tail, x], axis=0)
    xc = cb + cw[0:1] * x_ext[0:n_rows]
    for j in range(1, CONV_W):
        xc = xc + cw[j:j + 1] * x_ext[j * n_batch:j * n_batch + n_rows]
    half_pre = 0.5 * (jnp.dot(xc.astype(BF16), wg, preferred_element_type=F32) + bg)
    tr = jnp.tanh(half_pre[:, :LRU_BLOCK])
    ti = jnp.tanh(half_pre[:, LRU_BLOCK:])
    a = jnp.exp2(d * tr + d)
    gain2 = 1.0 - a * a
    u = (gain2 * lax.rsqrt(jnp.maximum(gain2, 1e-30))) * ((0.5 + 0.5 * ti) * xc)
    hs = []
    for s in range(n_steps):
        rows = slice(s * n_batch, (s + 1) * n_batch)
        h = a[rows] * h + u[rows]
        hs.append(h)
    return _gelu_tanh(gate) * jnp.concatenate(hs, axis=0), x_ext[n_rows:], h


def _lru_body(gate_ref, xr_ref, cs_ref, h0_ref, cw_ref, cb_ref, wg_ref, bg_ref, lam_ref,
              z_ref, ht_ref, h_sc, tail_sc, *, n_steps, n_batch, tc_cols):
    t = pl.program_id(1)

    @pl.when(t == 0)
    def _():
        h_sc[...] = h0_ref[...]
        tail_sc[...] = cs_ref[...]

    decay = _lru_decay(lam_ref[...])
    for n in range(tc_cols // LRU_BLOCK):
        cols = slice(n * LRU_BLOCK, (n + 1) * LRU_BLOCK)
        z, tail_sc[:, cols], h_sc[:, cols] = _lru_columns(
            gate_ref[:, cols], xr_ref[:, cols], tail_sc[:, cols], h_sc[:, cols], cw_ref[:, cols], cb_ref[:, cols],
            wg_ref[n], bg_ref[n], decay[:, cols], n_steps=n_steps, n_batch=n_batch)
        z_ref[:, cols] = z.astype(z_ref.dtype)

    @pl.when(t == pl.num_programs(1) - 1)
    def _():
        ht_ref[...] = h_sc[...]


def _lru(gx, conv_state, h0, conv_w, conv_b, w_gates, b_gates, lam, *, n_batch, n_steps, tc_cols=512):
    m = gx.shape[0]
    c = D_RNN
    n_rows = n_steps * n_batch
    n_tail = (CONV_W - 1) * n_batch
    ncb = c // tc_cols
    gpb = tc_cols // LRU_BLOCK
    z_spec = pl.BlockSpec((n_rows, tc_cols), lambda ci, ti: (ti, ci))
    z_shape = jax.ShapeDtypeStruct((m, c), BF16)
    z, ht = pl.pallas_call(
        functools.partial(_lru_body, n_steps=n_steps, n_batch=n_batch, tc_cols=tc_cols),
        grid=(ncb, m // n_rows),
        in_specs=[
            pl.BlockSpec((n_rows, tc_cols), lambda ci, ti: (ti, ci)),
            pl.BlockSpec((n_rows, tc_cols), lambda ci, ti: (ti, ncb + ci)),
            pl.BlockSpec((n_tail, tc_cols), lambda ci, ti: (0, ci)),
            pl.BlockSpec((n_batch, tc_cols), lambda ci, ti: (0, ci)),
            pl.BlockSpec((CONV_W, tc_cols), lambda ci, ti: (0, ci)),
            pl.BlockSpec((1, tc_cols), lambda ci, ti: (0, ci)),
            pl.BlockSpec((gpb, LRU_BLOCK, 2 * LRU_BLOCK), lambda ci, ti: (ci, 0, 0)),
            pl.BlockSpec((gpb, 1, 2 * LRU_BLOCK), lambda ci, ti: (ci, 0, 0)),
            pl.BlockSpec((1, tc_cols), lambda ci, ti: (0, ci)),
        ],
        out_specs=[z_spec, pl.BlockSpec((n_batch, tc_cols), lambda ci, ti: (0, ci))],
        out_shape=[z_shape, jax.ShapeDtypeStruct((n_batch, c), F32)],
        scratch_shapes=[pltpu.VMEM((n_batch, tc_cols), F32), pltpu.VMEM((n_tail, tc_cols), F32)],
        compiler_params=_params(("arbitrary", "arbitrary")),
        name="lru",
    )(gx, gx, conv_state, h0, conv_w, conv_b, w_gates, b_gates, lam)
    return z, ht


FRONT_STEPS = 2 * PERM_STEPS


def _rec_front_body(*refs, n_batch, side_gains):
    x_ref, norm_gain_ref, win_ref, cw_ref, cb_ref, wg_ref, bg_ref, lam_ref = refs[:8]
    pos = 8
    side_in = []
    for has_gain in side_gains:
        side_in.append((refs[pos], refs[pos + 1] if has_gain else None))
        pos += 2 if has_gain else 1
    z_ref, ht_ref, tail_ref = refs[pos:pos + 3]
    pos += 3
    side_out = refs[pos:pos + len(side_gains)]
    h_sc, tail_sc = refs[pos + len(side_gains):]
    s = pl.program_id(0)
    c = D_RNN

    @pl.when(s == 0)
    def _():
        h_sc[...] = jnp.zeros_like(h_sc)
        tail_sc[...] = jnp.zeros_like(tail_sc)

    for (src_ref, g_ref), dst_ref in zip(side_in, side_out):
        chunk = src_ref[...]
        if g_ref is not None:
            chunk = chunk * g_ref[...]
        dst_ref[...] = chunk.astype(dst_ref.dtype)

    p_rows = n_batch * PERM_STEPS
    n_tiles = FRONT_STEPS // PERM_STEPS
    to_tb = _perm_matrix(n_batch, True)
    xn = []
    for tile in range(n_tiles):
        x = x_ref[:, tile * PERM_STEPS:(tile + 1) * PERM_STEPS, :].reshape(p_rows, x_ref.shape[-1])
        y = x * lax.rsqrt(jnp.mean(x * x, axis=-1, keepdims=True) + EPS)
        xn.append(jnp.dot(to_tb, (y * norm_gain_ref[...]).astype(BF16), preferred_element_type=F32).astype(BF16))
    gx = jnp.dot(jnp.concatenate(xn, axis=0), win_ref[...], preferred_element_type=F32)

    to_bt = _perm_matrix(n_batch, False)
    decay = _lru_decay(lam_ref[...])
    for n in range(N_LRU_BLOCKS):
        cols = slice(n * LRU_BLOCK, (n + 1) * LRU_BLOCK)
        z, tail_sc[:, cols], h_sc[:, cols] = _lru_columns(
            gx[:, n * LRU_BLOCK:(n + 1) * LRU_BLOCK], gx[:, c + n * LRU_BLOCK:c + (n + 1) * LRU_BLOCK],
            tail_sc[:, cols], h_sc[:, cols], cw_ref[:, cols], cb_ref[:, cols], wg_ref[n], bg_ref[n], decay[:, cols],
            n_steps=FRONT_STEPS, n_batch=n_batch)
        z = z.astype(BF16)
        for tile in range(n_tiles):
            zp = jnp.dot(to_bt, z[tile * p_rows:(tile + 1) * p_rows], preferred_element_type=F32)
            z_ref[:, tile * PERM_STEPS:(tile + 1) * PERM_STEPS, cols] = (
                zp.astype(z_ref.dtype).reshape(n_batch, PERM_STEPS, LRU_BLOCK))

    @pl.when(s == pl.num_programs(0) - 1)
    def _():
        ht_ref[...] = h_sc[...]
        tail_ref[...] = tail_sc[...]


def _rec_front(x3, gain, w_in, conv_w, conv_b, w_gates, b_gates, lam, *, side_casts=()):
    n_batch, t, d = x3.shape
    c = D_RNN
    n_blocks = t // FRONT_STEPS
    n_tail = (CONV_W - 1) * n_batch
    const = lambda *shape: pl.BlockSpec(shape, lambda s: (0,) * len(shape))

    in_specs = [pl.BlockSpec((n_batch, FRONT_STEPS, d), lambda s: (0, s, 0)), const(1, d),
                pl.BlockSpec((d, 2 * c), lambda s: (0, 0), pipeline_mode=pl.Buffered(1)),
                const(CONV_W, c), const(1, c), const(N_LRU_BLOCKS, LRU_BLOCK, 2 * LRU_BLOCK),
                const(N_LRU_BLOCKS, 1, 2 * LRU_BLOCK), const(1, c)]
    args = [x3, gain.reshape(1, d).astype(F32), w_in, conv_w, conv_b, w_gates, b_gates, lam]
    out_specs = [pl.BlockSpec((n_batch, FRONT_STEPS, c), lambda s: (0, s, 0)), const(n_batch, c), const(n_tail, c)]
    out_shape = [jax.ShapeDtypeStruct((n_batch, t, c), BF16),
                 jax.ShapeDtypeStruct((n_batch, c), F32), jax.ShapeDtypeStruct((n_tail, c), F32)]
    for src, gain in side_casts:
        rows, cols = src.shape
        chunk = rows // n_blocks
        assert chunk * n_blocks == rows and chunk % (2 * SUBLANES) == 0, (rows, n_blocks)
        in_specs.append(pl.BlockSpec((chunk, cols), lambda s: (s, 0)))
        args.append(src)
        if gain is not None:
            in_specs.append(pl.BlockSpec((chunk, 1), lambda s: (s, 0)))
            args.append(gain.reshape(rows, 1).astype(F32))
        out_specs.append(pl.BlockSpec((chunk, cols), lambda s: (s, 0)))
        out_shape.append(jax.ShapeDtypeStruct((rows, cols), BF16))

    return pl.pallas_call(
        functools.partial(_rec_front_body, n_batch=n_batch, side_gains=tuple(g is not None for _, g in side_casts)),
        grid=(n_blocks,),
        in_specs=in_specs,
        out_specs=out_specs,
        out_shape=out_shape,
        scratch_shapes=[pltpu.VMEM((n_batch, c), F32), pltpu.VMEM((n_tail, c), F32)],
        compiler_params=_params(("arbitrary",)),
        name="rec_front",
    )(*args)


ATTN_BLOCKS = 4


def _attend_block(sink_ref, q_ref, o_ref, rows, kv, prev_bias):
    n_keys = 2 * WINDOW
    low_k = lax.broadcasted_iota(jnp.int32, (n_keys, LANES), 1) < HEAD_DIM
    low_q = lax.broadcasted_iota(jnp.int32, (WINDOW, LANES), 1) < HEAD_DIM
    zero = jnp.zeros((n_keys, LANES), BF16)
    one = jnp.ones((n_keys, LANES), BF16)
    from_prev = (lax.broadcasted_iota(jnp.int32, (WINDOW, WINDOW), 1)
                 > lax.broadcasted_iota(jnp.int32, (WINDOW, WINDOW), 0))
    dims = (((1,), (1,)), ((), ()))

    n_pairs = GROUP // 2

    def score_head(h):
        c0 = (h // 2) * LANES
        k_t = kv[:, c0:c0 + LANES]
        v_t = kv[:, KV_DIM + c0:KV_DIM + c0 + LANES]
        k_sw = pltpu.roll(k_t, HEAD_DIM, 1)
        v_sw = pltpu.roll(v_t, HEAD_DIM, 1)
        k_lo, k_hi = (k_t, k_sw) if h % 2 == 0 else (k_sw, k_t)
        v_lo, v_hi = (v_t, v_sw) if h % 2 == 0 else (v_sw, v_t)
        k_a = jnp.where(low_k, k_lo, zero)
        k_b = jnp.where(low_k, zero, k_hi)
        v_a = jnp.where(low_k, v_lo, one)
        v_b = jnp.where(low_k, one, v_hi)
        qc0 = h * GROUP * HEAD_DIM
        q_h = jnp.concatenate([q_ref[rows, qc0 + p * LANES:qc0 + (p + 1) * LANES] for p in range(n_pairs)], axis=0)
        scores = [lax.dot_general(q_h, k_x, dims, preferred_element_type=F32) for k_x in (k_a, k_b)]
        return scores, (v_a, v_b)

    for h in range(N_KV_HEADS):
        scores, values = score_head(h)
        qc0 = h * GROUP * HEAD_DIM
        acc, esink = [], []
        for side, v_x in enumerate(values):
            s = scores[side]
            probs, es = [], []
            for p in range(n_pairs):
                s_p = s[p * WINDOW:(p + 1) * WINDOW]
                s_prev = s_p[:, :WINDOW] if prev_bias is None else s_p[:, :WINDOW] + prev_bias
                s_v = jnp.where(from_prev, s_prev, s_p[:, WINDOW:])
                sink = sink_ref[h * GROUP + 2 * p + side] * LOG2E
                m = jnp.maximum(jnp.max(s_v, axis=-1, keepdims=True), sink)
                e = jnp.exp2(s_v - m)
                probs.append(jnp.concatenate([jnp.where(from_prev, e, 0.0), jnp.where(from_prev, 0.0, e)],
                                             axis=1).astype(BF16))
                es.append(jnp.exp2(sink - m))
            acc.append(jnp.dot(jnp.concatenate(probs, axis=0), v_x, preferred_element_type=F32))
            esink.append(es)
        for p in range(n_pairs):
            part = slice(p * WINDOW, (p + 1) * WINDOW)
            acc_a, acc_b = acc[0][part], acc[1][part]
            num = jnp.where(low_q, acc_a, acc_b)
            den = pltpu.roll(jnp.where(low_q, acc_b, acc_a), HEAD_DIM, 1) + jnp.where(low_q, esink[0][p], esink[1][p])
            o_ref[rows, qc0 + p * LANES:qc0 + (p + 1) * LANES] = (num / den).astype(o_ref.dtype)


def _attn_front_body(sink_ref, x_ref, wq_ref, wkv_ref, o_ref, kv_ref, q_sc, kv_last_sc):
    i = pl.program_id(1)

    @pl.when(i == 0)
    def _():
        kv_last_sc[...] = jnp.zeros_like(kv_last_sc)

    xf = x_ref[...]
    scale = lax.rsqrt(jnp.mean(xf * xf, axis=-1, keepdims=True) + EPS)
    x = xf.astype(BF16)
    q_sc[...] = (jnp.dot(x, wq_ref[...], preferred_element_type=F32) * (scale * (HEAD_DIM ** -0.5 * LOG2E))).astype(BF16)
    kv_new = jnp.dot(x, wkv_ref[...], preferred_element_type=F32) * scale
    kv_ref[...] = kv_new
    kv_own = kv_new.astype(BF16)
    for blk in range(ATTN_BLOCKS):
        rows = slice(blk * WINDOW, (blk + 1) * WINDOW)
        prev = kv_last_sc[...] if blk == 0 else kv_own[(blk - 1) * WINDOW:blk * WINDOW]
        kv = jnp.concatenate([prev, kv_own[rows]], axis=0)
        _attend_block(sink_ref, q_sc, o_ref, rows, kv, jnp.where(i == 0, -jnp.inf, 0.0) if blk == 0 else None)
    kv_last_sc[...] = kv_own[(ATTN_BLOCKS - 1) * WINDOW:]


def _attn_front(x3, w_q, w_kv, sinks):
    b, t, d = x3.shape
    step_rows = ATTN_BLOCKS * WINDOW
    return pl.pallas_call(
        _attn_front_body,
        grid=(b, t // step_rows),
        in_specs=[
            pl.BlockSpec(memory_space=pltpu.SMEM),
            pl.BlockSpec((None, step_rows, d), lambda bi, i: (bi, i, 0)),
            pl.BlockSpec((d, Q_DIM), lambda bi, i: (0, 0), pipeline_mode=pl.Buffered(1)),
            pl.BlockSpec((d, 2 * KV_DIM), lambda bi, i: (0, 0), pipeline_mode=pl.Buffered(1)),
        ],
        out_specs=[pl.BlockSpec((None, step_rows, Q_DIM), lambda bi, i: (bi, i, 0)),
                   pl.BlockSpec((None, step_rows, 2 * KV_DIM), lambda bi, i: (bi, i, 0))],
        out_shape=[jax.ShapeDtypeStruct((b, t, Q_DIM), BF16), jax.ShapeDtypeStruct((b, t, 2 * KV_DIM), F32)],
        scratch_shapes=[pltpu.VMEM((step_rows, Q_DIM), BF16), pltpu.VMEM((WINDOW, 2 * KV_DIM), BF16)],
        compiler_params=_params(("arbitrary", "arbitrary")),
        name="attn_front",
    )(sinks, x3, w_q, w_kv)


SAMPLE_BATCH_TILE = 8


def _attn_sample_body(sink_ref, q_ref, kvn_ref, ck_ref, cv_ref, o_ref, *, n_steps):
    bb = SAMPLE_BATCH_TILE
    n_new = n_steps * bb
    n_cache = bb * WINDOW
    n_keys = n_cache + n_new
    n_keys_pad = -(-n_keys // LANES) * LANES
    q = q_ref[...].reshape(n_new, Q_DIM)
    kvn = kvn_ref[...].reshape(n_new, 2 * KV_DIM)
    pad = jnp.zeros((n_keys_pad - n_keys, KV_DIM), F32)
    k_all = jnp.concatenate([ck_ref[...].reshape(n_cache, KV_DIM), kvn[:, :KV_DIM], pad], axis=0).astype(BF16)
    v_all = jnp.concatenate([cv_ref[...].reshape(n_cache, KV_DIM), kvn[:, KV_DIM:], pad], axis=0).astype(BF16)
    n_rows = GROUP * n_new
    r = lax.broadcasted_iota(jnp.int32, (n_rows, n_keys_pad), 0)
    c = lax.broadcasted_iota(jnp.int32, (n_rows, n_keys_pad), 1)
    b_r = r % bb
    t_r = (r // bb) % n_steps
    is_cache = c < n_cache
    c2 = jnp.maximum(c - n_cache, 0)
    b_c = jnp.where(is_cache, c // WINDOW, c2 % bb)
    k_step = jnp.where(is_cache, c % WINDOW - WINDOW, c2 // bb)
    dq = t_r - k_step
    allowed = jnp.logical_and(jnp.logical_and(b_c == b_r, c < n_keys),
                              jnp.logical_and(dq >= 0, dq < WINDOW))
    grp = lax.broadcasted_iota(jnp.int32, (n_rows, 1), 0) // n_new
    dims = (((1,), (1,)), ((), ()))
    for h in range(N_KV_HEADS):
        k_h = k_all[:, h * HEAD_DIM:(h + 1) * HEAD_DIM]
        v_h = v_all[:, h * HEAD_DIM:(h + 1) * HEAD_DIM]
        q_h = jnp.concatenate(
            [q[:, (h * GROUP + g) * HEAD_DIM:(h * GROUP + g + 1) * HEAD_DIM] for g in range(GROUP)], axis=0)
        sink = jnp.zeros((n_rows, 1), F32)
        for g in range(GROUP):
            sink = jnp.where(grp == g, sink_ref[h * GROUP + g] * LOG2E, sink)
        s = lax.dot_general(q_h, k_h, dims, preferred_element_type=F32)
        s = jnp.where(allowed, s, -jnp.inf)
        m = jnp.maximum(jnp.max(s, axis=-1, keepdims=True), sink)
        p = jnp.exp2(s - m)
        denom = jnp.sum(p, axis=-1, keepdims=True) + jnp.exp2(sink - m)
        o_h = jnp.dot(p.astype(BF16), v_h, preferred_element_type=F32) / denom
        for g in range(GROUP):
            c0 = (h * GROUP + g) * HEAD_DIM
            o_ref[:, :, c0:c0 + HEAD_DIM] = (
                o_h[g * n_new:(g + 1) * n_new].reshape(n_steps, bb, HEAD_DIM).astype(o_ref.dtype))


def _attn_sample(q3, kvn3, cache_k, cache_v, sinks):
    n_steps, b, _ = q3.shape
    bb = SAMPLE_BATCH_TILE
    return pl.pallas_call(
        functools.partial(_attn_sample_body, n_steps=n_steps),
        grid=(b // bb,),
        in_specs=[
            pl.BlockSpec(memory_space=pltpu.SMEM),
            pl.BlockSpec((n_steps, bb, Q_DIM), lambda i: (0, i, 0)),
            pl.BlockSpec((n_steps, bb, 2 * KV_DIM), lambda i: (0, i, 0)),
            pl.BlockSpec((bb, WINDOW, KV_DIM), lambda i: (i, 0, 0)),
            pl.BlockSpec((bb, WINDOW, KV_DIM), lambda i: (i, 0, 0)),
        ],
        out_specs=pl.BlockSpec((n_steps, bb, Q_DIM), lambda i: (0, i, 0)),
        out_shape=jax.ShapeDtypeStruct((n_steps, b, Q_DIM), BF16),
        compiler_params=_params(("arbitrary",)),
        name="attn_sample",
    )(sinks, q3, kvn3, cache_k, cache_v)


TILES = {
    "rec_in": dict(bm=512, bn=2048),
    "rec_out": dict(bm=512, bn=2048),
    "attn_out": dict(bm=512, bn=2048),
    "mlp_up": dict(bm=1024, bn=2048),
    "mlp_down": dict(bm=1024, bn=1024, bk=2048),
}


def _mm(name, x, w, **kw):
    return _matmul(x, w, name=name, **TILES[name], **kw)


def _flat(w):
    return w.reshape(-1, w.shape[-1])


def _trunk(z, x, attn_fn, raw, w):
    h, hb, ssq = _mm("rec_out", z, w["rec_w_out"], out_dtype=F32, resid=x, bf16_copy=True, ssq_out=True)
    if "mlp_w_down" not in w:
        hid, w_down, w_kv, w_q, w_o = _mm("mlp_up", hb, w["mlp_w_up"], layer=0, out_dtype=BF16, act="relu2",
                                          ssq_in=ssq, side_casts=[
            (_flat(raw["mlp_w_down"]), None), (raw["w_kv"], raw["kv_norm"]),
            (_flat(raw["attn_w_q"]), raw["norm_mix"][1]), (_flat(raw["attn_w_o"]), None)])
        w["mlp_w_down"] = w_down.reshape(raw["mlp_w_down"].shape)
        w["w_kv"] = w_kv
        w["attn_w_q"] = w_q
        w["attn_w_o"] = w_o.reshape(raw["attn_w_o"].shape)
    else:
        hid = _mm("mlp_up", hb, w["mlp_w_up"], layer=0, out_dtype=BF16, act="relu2", ssq_in=ssq)
    h = _mm("mlp_down", hid, w["mlp_w_down"], layer=0, out_dtype=F32, resid=h)
    o, kv = attn_fn(h, w["attn_w_q"], w["w_kv"])
    h, hb, ssq = _mm("attn_out", o, w["attn_w_o"], out_dtype=F32, resid=h, bf16_copy=True, ssq_out=True)
    hid = _mm("mlp_up", hb, w["mlp_w_up"], layer=1, out_dtype=BF16, act="relu2", ssq_in=ssq)
    h = _mm("mlp_down", hid, w["mlp_w_down"], layer=1, out_dtype=F32, resid=h)
    return _rmsnorm(h, raw["final_norm"], F32), kv


def kernel(x_prompt, x_sample, state_conv, state_h, cache_k, cache_v, norm_mix, norm_mlp, rec_w_in,
           rec_conv_w, rec_conv_b, rec_gate_a_w, rec_gate_a_b, rec_gate_x_w, rec_gate_x_b, rec_lambda,
           rec_w_out, kv_norm, w_kv, attn_w_q, attn_sinks, attn_w_o, mlp_w_up, mlp_w_down, final_norm):
    b, t, d = x_prompt.shape
    sb, st, _ = x_sample.shape
    c = D_RNN
    n_tail = CONV_W - 1

    raw = {
        "norm_mix": norm_mix, "kv_norm": kv_norm, "final_norm": final_norm,
        "w_kv": w_kv, "attn_w_q": attn_w_q, "attn_w_o": attn_w_o, "mlp_w_down": mlp_w_down,
    }
    w_in = rec_w_in[0].astype(BF16)
    w = {}
    conv_w = rec_conv_w[0]
    conv_b = rec_conv_b[0].reshape(1, c)
    lam = rec_lambda[0].reshape(1, c)
    w_gates = jnp.concatenate([rec_gate_a_w[0], rec_gate_x_w[0]], axis=-1).astype(BF16)
    b_gates = jnp.concatenate([rec_gate_a_b[0], rec_gate_x_b[0]], axis=-1).reshape(N_LRU_BLOCKS, 1, 2 * LRU_BLOCK)
    sinks = attn_sinks[0].astype(F32)

    def attn_prompt(h, w_q, w_kv_b):
        o, kv = _attn_front(h.reshape(b, t, d), w_q, w_kv_b, sinks)
        return o.reshape(b * t, Q_DIM), kv.reshape(b * t, 2 * KV_DIM)

    z_p, h_p, tail_p, w_out, w_up = _rec_front(
        x_prompt, norm_mix[0], w_in, conv_w, conv_b, w_gates, b_gates, lam,
        side_casts=[(_flat(rec_w_out), None), (_flat(mlp_w_up), norm_mlp.reshape(-1))])
    w["rec_w_out"] = w_out.reshape(rec_w_out.shape)
    w["mlp_w_up"] = w_up.reshape(mlp_w_up.shape)
    y_p, kv_p = _trunk(z_p.reshape(b * t, c), x_prompt.reshape(b * t, d), attn_prompt, raw, w)
    y_prompt = y_p.reshape(b, t, d)
    conv_p = jnp.swapaxes(tail_p.reshape(n_tail, b, c), 0, 1).reshape(b, 1, n_tail, c)
    keep = min(WINDOW, t)
    kv_tail = kv_p.reshape(b, t, 2 * KV_DIM)[:, t - keep:]
    new_k_prompt = kv_tail[..., :KV_DIM].reshape(b, keep, N_KV_HEADS, HEAD_DIM)
    new_v_prompt = kv_tail[..., KV_DIM:].reshape(b, keep, N_KV_HEADS, HEAD_DIM)

    conv_state_tb = jnp.swapaxes(state_conv[:, 0], 0, 1).reshape(n_tail * sb, c)
    h0 = state_h[:, 0]
    ck = cache_k.reshape(sb, WINDOW, KV_DIM)
    cv = cache_v.reshape(sb, WINDOW, KV_DIM)

    def attn_sample(h, w_q, w_kv_b):
        q, kv = _qkv_proj(h, w_q, w_kv_b)
        o = _attn_sample(q.reshape(st, sb, Q_DIM), kv.reshape(st, sb, 2 * KV_DIM), ck, cv, sinks)
        return o.reshape(st * sb, Q_DIM), kv

    x_s = jnp.swapaxes(x_sample, 0, 1).reshape(st * sb, d)
    gx_s = _mm("rec_in", _rmsnorm(x_s, norm_mix[0], BF16), w_in[None], out_dtype=F32)
    z_s, h_s = _lru(gx_s, conv_state_tb, h0, conv_w, conv_b, w_gates, b_gates, lam, n_batch=sb, n_steps=st)
    y_s, kv_s = _trunk(z_s, x_s, attn_sample, raw, w)
    y_sample = jnp.swapaxes(y_s.reshape(st, sb, d), 0, 1)
    conv_s = jnp.swapaxes(gx_s.reshape(st, sb, 2 * c)[st - n_tail:, :, c:], 0, 1).reshape(sb, 1, n_tail, c)
    kv_s = jnp.swapaxes(kv_s.reshape(st, sb, 2 * KV_DIM), 0, 1)
    new_k_sample = kv_s[..., :KV_DIM].reshape(sb, st, N_KV_HEADS, HEAD_DIM)
    new_v_sample = kv_s[..., KV_DIM:].reshape(sb, st, N_KV_HEADS, HEAD_DIM)

    return (y_prompt, y_sample, conv_p, h_p.reshape(b, 1, c), new_k_prompt, new_v_prompt,
            conv_s, h_s.reshape(sb, 1, c), new_k_sample, new_v_sample)
```

```python
import functools
import math

import jax
import jax.numpy as jnp
from jax import lax
from jax.experimental import pallas as pl
from jax.experimental.pallas import tpu as pltpu

F32 = jnp.float32
BF16 = jnp.bfloat16

D_MODEL = 2048
D_RNN = 2048
N_LRU_BLOCKS = 16
LRU_BLOCK = D_RNN // N_LRU_BLOCKS
CONV_W = 4
LRU_C = 8.0
N_HEADS = 32
N_KV_HEADS = 4
HEAD_DIM = 64
GROUP = N_HEADS // N_KV_HEADS
Q_DIM = N_HEADS * HEAD_DIM
KV_DIM = N_KV_HEADS * HEAD_DIM
WINDOW = 128
EPS = 1e-6
LOG2E = math.log2(math.e)

SUBLANES = 8
LANES = 128
VMEM_LIMIT_BYTES = 52 * 1024 * 1024


def _params(semantics):
    return pltpu.CompilerParams(dimension_semantics=semantics,
                                vmem_limit_bytes=VMEM_LIMIT_BYTES)


def _rmsnorm_body(x_ref, g_ref, o_ref):
    x = x_ref[...]
    y = x * lax.rsqrt(jnp.mean(x * x, axis=-1, keepdims=True) + EPS)
    o_ref[...] = (y * g_ref[...]).astype(o_ref.dtype)


def _rmsnorm(x, gain, out_dtype, tm=1024):
    m, d = x.shape
    tm = min(tm, m)
    row_spec = pl.BlockSpec((tm, d), lambda i: (i, 0))
    return pl.pallas_call(
        _rmsnorm_body,
        grid=(m // tm,),
        in_specs=[row_spec, pl.BlockSpec((1, d), lambda i: (0, 0))],
        out_specs=row_spec,
        out_shape=jax.ShapeDtypeStruct((m, d), out_dtype),
        compiler_params=_params(("arbitrary",)),
        name="rmsnorm",
    )(x, gain.reshape(1, d).astype(F32))


PERM_STEPS = 32


def _perm_matrix(n_batch, to_time_major):
    n = n_batch * PERM_STEPS
    r = lax.broadcasted_iota(jnp.int32, (n, n), 0)
    c = lax.broadcasted_iota(jnp.int32, (n, n), 1)
    if to_time_major:
        src = (r % n_batch) * PERM_STEPS + r // n_batch
    else:
        src = (r % PERM_STEPS) * n_batch + r // PERM_STEPS
    return jnp.where(c == src, 1.0, 0.0).astype(BF16)


def _matmul_body(*refs, nk, act, has_scale, has_resid, has_copy, has_ssq, side_gains, norm_dim):
    refs = list(refs)
    x_ref, w_ref = refs[:2]
    pos = 2
    s_ref = r_ref = b_ref = q_ref = None
    if has_scale:
        s_ref = refs[pos]
        pos += 1
    if has_resid:
        r_ref = refs[pos]
        pos += 1
    side_in = []
    for has_gain in side_gains:
        side_in.append((refs[pos], refs[pos + 1] if has_gain else None))
        pos += 2 if has_gain else 1
    o_ref = refs[pos]
    pos += 1
    if has_copy:
        b_ref = refs[pos]
        pos += 1
    if has_ssq:
        q_ref = refs[pos]
        pos += 1
    side_out = refs[pos:pos + len(side_gains)]
    pos += len(side_gains)
    acc_ref = refs[pos] if nk > 1 else None

    def partial_product():
        for (src_ref, g_ref), dst_ref in zip(side_in, side_out):
            chunk = src_ref[...]
            if g_ref is not None:
                chunk = chunk * g_ref[...]
            dst_ref[...] = chunk.astype(dst_ref.dtype)
        return jnp.dot(x_ref[...], w_ref[...], preferred_element_type=F32)

    def finish(acc):
        if s_ref is not None:
            acc = acc * lax.rsqrt(jnp.sum(s_ref[...], axis=-1, keepdims=True) * (1.0 / norm_dim) + EPS)
        if act == "relu2":
            r = jnp.maximum(acc, 0.0)
            acc = r * r
        if r_ref is not None:
            acc = r_ref[...] + acc
        o_ref[...] = acc.astype(o_ref.dtype)
        if b_ref is not None:
            b_ref[...] = acc.astype(b_ref.dtype)
        if q_ref is not None:
            sq = acc * acc
            tot = sq[:, 0:LANES]
            for c in range(1, sq.shape[1] // LANES):
                tot = tot + sq[:, c * LANES:(c + 1) * LANES]
            q_ref[...] = tot

    if nk == 1:
        finish(partial_product())
    else:
        k = pl.program_id(2)

        @pl.when(k == 0)
        def _():
            acc_ref[...] = partial_product()

        @pl.when(jnp.logical_and(k > 0, k < nk - 1))
        def _():
            acc_ref[...] += partial_product()

        @pl.when(k == nk - 1)
        def _():
            finish(acc_ref[...] + partial_product())


def _matmul(x, w, *, out_dtype, layer=0, act=None, ssq_in=None, resid=None, bf16_copy=False, ssq_out=False,
            side_casts=(), cols_outer=False, bm=1024, bn=1024, bk=None, name="matmul"):
    m = x.shape[0]
    _, kdim, n = w.shape
    bm, bn = min(bm, m), min(bn, n)
    bk = kdim if bk is None else min(bk, kdim)
    nk, nb = kdim // bk, n // bn
    mb = m // bm
    n_steps = mb * nb * nk

    def spec(shape, index_map):
        if cols_outer:
            return pl.BlockSpec(shape, lambda j, i, k: index_map(i, j, k))
        return pl.BlockSpec(shape, index_map)

    in_specs = [spec((bm, bk), lambda i, j, k: (i, k)),
                spec((None, bk, bn), lambda i, j, k: (layer, k, j))]
    args = [x, w]
    if ssq_in is not None:
        in_specs.append(spec((bm, ssq_in.shape[1]), lambda i, j, k: (i, 0)))
        args.append(ssq_in)
    tile_spec = spec((bm, bn), lambda i, j, k: (i, j))
    if resid is not None:
        in_specs.append(tile_spec)
        args.append(resid)
    step = lambda i, j, k: (i * nb + j) * nk + k
    for src, gain in side_casts:
        rows, cols = src.shape
        chunk = rows // n_steps
        assert chunk * n_steps == rows and chunk % (2 * SUBLANES) == 0, (name, rows, n_steps)
        in_specs.append(spec((chunk, cols), lambda i, j, k: (step(i, j, k), 0)))
        args.append(src)
        if gain is not None:
            in_specs.append(spec((chunk, 1), lambda i, j, k: (step(i, j, k), 0)))
            args.append(gain.reshape(rows, 1).astype(F32))

    out_specs = [tile_spec]
    out_shape = [jax.ShapeDtypeStruct((m, n), out_dtype)]
    if bf16_copy:
        out_specs.append(tile_spec)
        out_shape.append(jax.ShapeDtypeStruct((m, n), BF16))
    if ssq_out:
        out_specs.append(spec((bm, LANES), lambda i, j, k: (i, j)))
        out_shape.append(jax.ShapeDtypeStruct((m, nb * LANES), F32))
    for src, _ in side_casts:
        rows, cols = src.shape
        out_specs.append(spec((rows // n_steps, cols), lambda i, j, k: (step(i, j, k), 0)))
        out_shape.append(jax.ShapeDtypeStruct((rows, cols), BF16))

    outs = pl.pallas_call(
        functools.partial(_matmul_body, nk=nk, act=act, has_scale=ssq_in is not None,
                          has_resid=resid is not None, has_copy=bf16_copy, has_ssq=ssq_out,
                          side_gains=tuple(g is not None for _, g in side_casts), norm_dim=kdim),
        grid=(nb, mb, nk) if cols_outer else (mb, nb, nk),
        in_specs=in_specs,
        out_specs=out_specs,
        out_shape=out_shape,
        scratch_shapes=[pltpu.VMEM((bm, bn), F32)] if nk > 1 else [],
        compiler_params=_params(("arbitrary", "arbitrary", "arbitrary")),
        name=name,
    )(*args)
    return outs[0] if len(outs) == 1 else outs


def _qkv_body(x_ref, wq_ref, wkv_ref, q_ref, kv_ref):
    xf = x_ref[...]
    scale = lax.rsqrt(jnp.mean(xf * xf, axis=-1, keepdims=True) + EPS)
    x = xf.astype(BF16)
    q = jnp.dot(x, wq_ref[...], preferred_element_type=F32)
    q_ref[...] = (q * (scale * (HEAD_DIM ** -0.5 * LOG2E))).astype(q_ref.dtype)
    kv_ref[...] = jnp.dot(x, wkv_ref[...], preferred_element_type=F32) * scale


def _qkv_proj(x, w_q, w_kv, bm=1024):
    m, kdim = x.shape
    bm = min(bm, m)
    return pl.pallas_call(
        _qkv_body,
        grid=(m // bm,),
        in_specs=[pl.BlockSpec((bm, kdim), lambda i: (i, 0)),
                  pl.BlockSpec((kdim, Q_DIM), lambda i: (0, 0), pipeline_mode=pl.Buffered(1)),
                  pl.BlockSpec((kdim, 2 * KV_DIM), lambda i: (0, 0), pipeline_mode=pl.Buffered(1))],
        out_specs=[pl.BlockSpec((bm, Q_DIM), lambda i: (i, 0)),
                   pl.BlockSpec((bm, 2 * KV_DIM), lambda i: (i, 0))],
        out_shape=[jax.ShapeDtypeStruct((m, Q_DIM), BF16), jax.ShapeDtypeStruct((m, 2 * KV_DIM), F32)],
        compiler_params=_params(("arbitrary",)),
        name="qkv_proj",
    )(x, w_q, w_kv)


def _gelu_tanh(x):
    return x * (0.5 + 0.5 * jnp.tanh(0.7978845608028654 * (x + 0.044715 * (x * x * x))))


def _softplus(z):
    return jnp.maximum(z, 0.0) + jnp.log1p(jnp.exp(-jnp.abs(z)))


def _lru_decay(lam):
    return (-0.5 * LRU_C * LOG2E) * _softplus(-lam)


def _lru_columns(gate, x, tail, h, cw, cb, wg, bg, d, *, n_steps, n_batch):
    n_rows = n_steps * n_batch
    x_ext = jnp.concatenate([tail, x], axis=0)
    xc = cb + cw[0:1] * x_ext[0:n_rows]
    for j in range(1, CONV_W):
        xc = xc + cw[j:j + 1] * x_ext[j * n_batch:j * n_batch + n_rows]
    half_pre = 0.5 * (jnp.dot(xc.astype(BF16), wg, preferred_element_type=F32) + bg)
    tr = jnp.tanh(half_pre[:, :LRU_BLOCK])
    ti = jnp.tanh(half_pre[:, LRU_BLOCK:])
    a = jnp.exp2(d * tr + d)
    gain2 = 1.0 - a * a
    u = (gain2 * lax.rsqrt(jnp.maximum(gain2, 1e-30))) * ((0.5 + 0.5 * ti) * xc)
    hs = []
    for s in range(n_steps):
        rows = slice(s * n_batch, (s + 1) * n_batch)
        h = a[rows] * h + u[rows]
        hs.append(h)
    return _gelu_tanh(gate) * jnp.concatenate(hs, axis=0), x_ext[n_rows:], h


def _lru_body(gate_ref, xr_ref, cs_ref, h0_ref, cw_ref, cb_ref, wg_ref, bg_ref, lam_ref,
              z_ref, ht_ref, h_sc, tail_sc, *, n_steps, n_batch, tc_cols):
    t = pl.program_id(1)

    @pl.when(t == 0)
    def _():
        h_sc[...] = h0_ref[...]
        tail_sc[...] = cs_ref[...]

    decay = _lru_decay(lam_ref[...])
    for n in range(tc_cols // LRU_BLOCK):
        cols = slice(n * LRU_BLOCK, (n + 1) * LRU_BLOCK)
        z, tail_sc[:, cols], h_sc[:, cols] = _lru_columns(
            gate_ref[:, cols], xr_ref[:, cols], tail_sc[:, cols], h_sc[:, cols], cw_ref[:, cols], cb_ref[:, cols],
            wg_ref[n], bg_ref[n], decay[:, cols], n_steps=n_steps, n_batch=n_batch)
        z_ref[:, cols] = z.astype(z_ref.dtype)

    @pl.when(t == pl.num_programs(1) - 1)
    def _():
        ht_ref[...] = h_sc[...]


def _lru(gx, conv_state, h0, conv_w, conv_b, w_gates, b_gates, lam, *, n_batch, n_steps, tc_cols=512):
    m = gx.shape[0]
    c = D_RNN
    n_rows = n_steps * n_batch
    n_tail = (CONV_W - 1) * n_batch
    ncb = c // tc_cols
    gpb = tc_cols // LRU_BLOCK
    z_spec = pl.BlockSpec((n_rows, tc_cols), lambda ci, ti: (ti, ci))
    z_shape = jax.ShapeDtypeStruct((m, c), BF16)
    z, ht = pl.pallas_call(
        functools.partial(_lru_body, n_steps=n_steps, n_batch=n_batch, tc_cols=tc_cols),
        grid=(ncb, m // n_rows),
        in_specs=[
            pl.BlockSpec((n_rows, tc_cols), lambda ci, ti: (ti, ci)),
            pl.BlockSpec((n_rows, tc_cols), lambda ci, ti: (ti, ncb + ci)),
            pl.BlockSpec((n_tail, tc_cols), lambda ci, ti: (0, ci)),
            pl.BlockSpec((n_batch, tc_cols), lambda ci, ti: (0, ci)),
            pl.BlockSpec((CONV_W, tc_cols), lambda ci, ti: (0, ci)),
            pl.BlockSpec((1, tc_cols), lambda ci, ti: (0, ci)),
            pl.BlockSpec((gpb, LRU_BLOCK, 2 * LRU_BLOCK), lambda ci, ti: (ci, 0, 0)),
            pl.BlockSpec((gpb, 1, 2 * LRU_BLOCK), lambda ci, ti: (ci, 0, 0)),
            pl.BlockSpec((1, tc_cols), lambda ci, ti: (0, ci)),
        ],
        out_specs=[z_spec, pl.BlockSpec((n_batch, tc_cols), lambda ci, ti: (0, ci))],
        out_shape=[z_shape, jax.ShapeDtypeStruct((n_batch, c), F32)],
        scratch_shapes=[pltpu.VMEM((n_batch, tc_cols), F32), pltpu.VMEM((n_tail, tc_cols), F32)],
        compiler_params=_params(("arbitrary", "arbitrary")),
        name="lru",
    )(gx, gx, conv_state, h0, conv_w, conv_b, w_gates, b_gates, lam)
    return z, ht


FRONT_STEPS = 2 * PERM_STEPS


def _rec_front_body(*refs, n_batch, side_gains):
    x_ref, norm_gain_ref, win_ref, cw_ref, cb_ref, wg_ref, bg_ref, lam_ref = refs[:8]
    pos = 8
    side_in = []
    for has_gain in side_gains:
        side_in.append((refs[pos], refs[pos + 1] if has_gain else None))
        pos += 2 if has_gain else 1
    z_ref, ht_ref, tail_ref = refs[pos:pos + 3]
    pos += 3
    side_out = refs[pos:pos + len(side_gains)]
    h_sc, tail_sc = refs[pos + len(side_gains):]
    s = pl.program_id(0)
    c = D_RNN

    @pl.when(s == 0)
    def _():
        h_sc[...] = jnp.zeros_like(h_sc)
        tail_sc[...] = jnp.zeros_like(tail_sc)

    for (src_ref, g_ref), dst_ref in zip(side_in, side_out):
        chunk = src_ref[...]
        if g_ref is not None:
            chunk = chunk * g_ref[...]
        dst_ref[...] = chunk.astype(dst_ref.dtype)

    p_rows = n_batch * PERM_STEPS
    n_tiles = FRONT_STEPS // PERM_STEPS
    to_tb = _perm_matrix(n_batch, True)
    xn = []
    for tile in range(n_tiles):
        x = x_ref[:, tile * PERM_STEPS:(tile + 1) * PERM_STEPS, :].reshape(p_rows, x_ref.shape[-1])
        y = x * lax.rsqrt(jnp.mean(x * x, axis=-1, keepdims=True) + EPS)
        xn.append(jnp.dot(to_tb, (y * norm_gain_ref[...]).astype(BF16), preferred_element_type=F32).astype(BF16))
    gx = jnp.dot(jnp.concatenate(xn, axis=0), win_ref[...], preferred_element_type=F32)

    to_bt = _perm_matrix(n_batch, False)
    decay = _lru_decay(lam_ref[...])
    for n in range(N_LRU_BLOCKS):
        cols = slice(n * LRU_BLOCK, (n + 1) * LRU_BLOCK)
        z, tail_sc[:, cols], h_sc[:, cols] = _lru_columns(
            gx[:, n * LRU_BLOCK:(n + 1) * LRU_BLOCK], gx[:, c + n * LRU_BLOCK:c + (n + 1) * LRU_BLOCK],
            tail_sc[:, cols], h_sc[:, cols], cw_ref[:, cols], cb_ref[:, cols], wg_ref[n], bg_ref[n], decay[:, cols],
            n_steps=FRONT_STEPS, n_batch=n_batch)
        z = z.astype(BF16)
        for tile in range(n_tiles):
            zp = jnp.dot(to_bt, z[tile * p_rows:(tile + 1) * p_rows], preferred_element_type=F32)
            z_ref[:, tile * PERM_STEPS:(tile + 1) * PERM_STEPS, cols] = (
                zp.astype(z_ref.dtype).reshape(n_batch, PERM_STEPS, LRU_BLOCK))

    @pl.when(s == pl.num_programs(0) - 1)
    def _():
        ht_ref[...] = h_sc[...]
        tail_ref[...] = tail_sc[...]


def _rec_front(x3, gain, w_in, conv_w, conv_b, w_gates, b_gates, lam, *, side_casts=()):
    n_batch, t, d = x3.shape
    c = D_RNN
    n_blocks = t // FRONT_STEPS
    n_tail = (CONV_W - 1) * n_batch
    const = lambda *shape: pl.BlockSpec(shape, lambda s: (0,) * len(shape))

    in_specs = [pl.BlockSpec((n_batch, FRONT_STEPS, d), lambda s: (0, s, 0)), const(1, d),
                pl.BlockSpec((d, 2 * c), lambda s: (0, 0), pipeline_mode=pl.Buffered(1)),
                const(CONV_W, c), const(1, c), const(N_LRU_BLOCKS, LRU_BLOCK, 2 * LRU_BLOCK),
                const(N_LRU_BLOCKS, 1, 2 * LRU_BLOCK), const(1, c)]
    args = [x3, gain.reshape(1, d).astype(F32), w_in, conv_w, conv_b, w_gates, b_gates, lam]
    out_specs = [pl.BlockSpec((n_batch, FRONT_STEPS, c), lambda s: (0, s, 0)), const(n_batch, c), const(n_tail, c)]
    out_shape = [jax.ShapeDtypeStruct((n_batch, t, c), BF16),
                 jax.ShapeDtypeStruct((n_batch, c), F32), jax.ShapeDtypeStruct((n_tail, c), F32)]
    for src, gain in side_casts:
        rows, cols = src.shape
        chunk = rows // n_blocks
        assert chunk * n_blocks == rows and chunk % (2 * SUBLANES) == 0, (rows, n_blocks)
        in_specs.append(pl.BlockSpec((chunk, cols), lambda s: (s, 0)))
        args.append(src)
        if gain is not None:
            in_specs.append(pl.BlockSpec((chunk, 1), lambda s: (s, 0)))
            args.append(gain.reshape(rows, 1).astype(F32))
        out_specs.append(pl.BlockSpec((chunk, cols), lambda s: (s, 0)))
        out_shape.append(jax.ShapeDtypeStruct((rows, cols), BF16))

    return pl.pallas_call(
        functools.partial(_rec_front_body, n_batch=n_batch, side_gains=tuple(g is not None for _, g in side_casts)),
        grid=(n_blocks,),
        in_specs=in_specs,
        out_specs=out_specs,
        out_shape=out_shape,
        scratch_shapes=[pltpu.VMEM((n_batch, c), F32), pltpu.VMEM((n_tail, c), F32)],
        compiler_params=_params(("arbitrary",)),
        name="rec_front",
    )(*args)


ATTN_BLOCKS = 4


def _attn_prompt_body(sink_ref, q_ref, kvc_ref, kvp_ref, o_ref):
    i = pl.program_id(1)
    for blk in range(ATTN_BLOCKS):
        rows = slice(blk * WINDOW, (blk + 1) * WINDOW)
        prev = kvp_ref[...] if blk == 0 else kvc_ref[(blk - 1) * WINDOW:blk * WINDOW]
        kv = jnp.concatenate([prev, kvc_ref[rows]], axis=0).astype(BF16)
        _attend_block(sink_ref, q_ref, o_ref, rows, kv, jnp.where(i == 0, -jnp.inf, 0.0) if blk == 0 else None)


def _attend_block(sink_ref, q_ref, o_ref, rows, kv, prev_bias):
    n_keys = 2 * WINDOW
    low_k = lax.broadcasted_iota(jnp.int32, (n_keys, LANES), 1) < HEAD_DIM
    low_q = lax.broadcasted_iota(jnp.int32, (WINDOW, LANES), 1) < HEAD_DIM
    zero = jnp.zeros((n_keys, LANES), BF16)
    one = jnp.ones((n_keys, LANES), BF16)
    from_prev = (lax.broadcasted_iota(jnp.int32, (WINDOW, WINDOW), 1)
                 > lax.broadcasted_iota(jnp.int32, (WINDOW, WINDOW), 0))
    dims = (((1,), (1,)), ((), ()))

    n_pairs = GROUP // 2

    def score_head(h):
        c0 = (h // 2) * LANES
        k_t = kv[:, c0:c0 + LANES]
        v_t = kv[:, KV_DIM + c0:KV_DIM + c0 + LANES]
        k_sw = pltpu.roll(k_t, HEAD_DIM, 1)
        v_sw = pltpu.roll(v_t, HEAD_DIM, 1)
        k_lo, k_hi = (k_t, k_sw) if h % 2 == 0 else (k_sw, k_t)
        v_lo, v_hi = (v_t, v_sw) if h % 2 == 0 else (v_sw, v_t)
        k_a = jnp.where(low_k, k_lo, zero)
        k_b = jnp.where(low_k, zero, k_hi)
        v_a = jnp.where(low_k, v_lo, one)
        v_b = jnp.where(low_k, one, v_hi)
        qc0 = h * GROUP * HEAD_DIM
        q_h = jnp.concatenate([q_ref[rows, qc0 + p * LANES:qc0 + (p + 1) * LANES] for p in range(n_pairs)], axis=0)
        scores = [lax.dot_general(q_h, k_x, dims, preferred_element_type=F32) for k_x in (k_a, k_b)]
        return scores, (v_a, v_b)

    for h in range(N_KV_HEADS):
        scores, values = score_head(h)
        qc0 = h * GROUP * HEAD_DIM
        acc, esink = [], []
        for side, v_x in enumerate(values):
            s = scores[side]
            probs, es = [], []
            for p in range(n_pairs):
                s_p = s[p * WINDOW:(p + 1) * WINDOW]
                s_prev = s_p[:, :WINDOW] if prev_bias is None else s_p[:, :WINDOW] + prev_bias
                s_v = jnp.where(from_prev, s_prev, s_p[:, WINDOW:])
                sink = sink_ref[h * GROUP + 2 * p + side] * LOG2E
                m = jnp.maximum(jnp.max(s_v, axis=-1, keepdims=True), sink)
                e = jnp.exp2(s_v - m)
                probs.append(jnp.concatenate([jnp.where(from_prev, e, 0.0), jnp.where(from_prev, 0.0, e)],
                                             axis=1).astype(BF16))
                es.append(jnp.exp2(sink - m))
            acc.append(jnp.dot(jnp.concatenate(probs, axis=0), v_x, preferred_element_type=F32))
            esink.append(es)
        for p in range(n_pairs):
            part = slice(p * WINDOW, (p + 1) * WINDOW)
            acc_a, acc_b = acc[0][part], acc[1][part]
            num = jnp.where(low_q, acc_a, acc_b)
            den = pltpu.roll(jnp.where(low_q, acc_b, acc_a), HEAD_DIM, 1) + jnp.where(low_q, esink[0][p], esink[1][p])
            o_ref[rows, qc0 + p * LANES:qc0 + (p + 1) * LANES] = (num / den).astype(o_ref.dtype)


def _attn_prompt(q3, kv3, sinks):
    b, t, _ = q3.shape
    step_rows = ATTN_BLOCKS * WINDOW
    nb = t // step_rows
    return pl.pallas_call(
        _attn_prompt_body,
        grid=(b, nb),
        in_specs=[
            pl.BlockSpec(memory_space=pltpu.SMEM),
            pl.BlockSpec((None, step_rows, Q_DIM), lambda bi, i: (bi, i, 0)),
            pl.BlockSpec((None, step_rows, 2 * KV_DIM), lambda bi, i: (bi, i, 0)),
            pl.BlockSpec((None, WINDOW, 2 * KV_DIM), lambda bi, i: (bi, jnp.maximum(ATTN_BLOCKS * i - 1, 0), 0)),
        ],
        out_specs=pl.BlockSpec((None, step_rows, Q_DIM), lambda bi, i: (bi, i, 0)),
        out_shape=jax.ShapeDtypeStruct((b, t, Q_DIM), BF16),
        compiler_params=_params(("arbitrary", "arbitrary")),
        name="attn_prompt",
    )(sinks, q3, kv3, kv3)


SAMPLE_BATCH_TILE = 8


def _attn_sample_body(sink_ref, q_ref, kvn_ref, ck_ref, cv_ref, o_ref, *, n_steps):
    bb = SAMPLE_BATCH_TILE
    n_new = n_steps * bb
    n_cache = bb * WINDOW
    n_keys = n_cache + n_new
    n_keys_pad = -(-n_keys // LANES) * LANES
    q = q_ref[...].reshape(n_new, Q_DIM)
    kvn = kvn_ref[...].reshape(n_new, 2 * KV_DIM)
    pad = jnp.zeros((n_keys_pad - n_keys, KV_DIM), F32)
    k_all = jnp.concatenate([ck_ref[...].reshape(n_cache, KV_DIM), kvn[:, :KV_DIM], pad], axis=0).astype(BF16)
    v_all = jnp.concatenate([cv_ref[...].reshape(n_cache, KV_DIM), kvn[:, KV_DIM:], pad], axis=0).astype(BF16)
    n_rows = GROUP * n_new
    r = lax.broadcasted_iota(jnp.int32, (n_rows, n_keys_pad), 0)
    c = lax.broadcasted_iota(jnp.int32, (n_rows, n_keys_pad), 1)
    b_r = r % bb
    t_r = (r // bb) % n_steps
    is_cache = c < n_cache
    c2 = jnp.maximum(c - n_cache, 0)
    b_c = jnp.where(is_cache, c // WINDOW, c2 % bb)
    k_step = jnp.where(is_cache, c % WINDOW - WINDOW, c2 // bb)
    dq = t_r - k_step
    allowed = jnp.logical_and(jnp.logical_and(b_c == b_r, c < n_keys),
                              jnp.logical_and(dq >= 0, dq < WINDOW))
    grp = lax.broadcasted_iota(jnp.int32, (n_rows, 1), 0) // n_new
    dims = (((1,), (1,)), ((), ()))
    for h in range(N_KV_HEADS):
        k_h = k_all[:, h * HEAD_DIM:(h + 1) * HEAD_DIM]
        v_h = v_all[:, h * HEAD_DIM:(h + 1) * HEAD_DIM]
        q_h = jnp.concatenate(
            [q[:, (h * GROUP + g) * HEAD_DIM:(h * GROUP + g + 1) * HEAD_DIM] for g in range(GROUP)], axis=0)
        sink = jnp.zeros((n_rows, 1), F32)
        for g in range(GROUP):
            sink = jnp.where(grp == g, sink_ref[h * GROUP + g] * LOG2E, sink)
        s = lax.dot_general(q_h, k_h, dims, preferred_element_type=F32)
        s = jnp.where(allowed, s, -jnp.inf)
        m = jnp.maximum(jnp.max(s, axis=-1, keepdims=True), sink)
        p = jnp.exp2(s - m)
        denom = jnp.sum(p, axis=-1, keepdims=True) + jnp.exp2(sink - m)
        o_h = jnp.dot(p.astype(BF16), v_h, preferred_element_type=F32) / denom
        for g in range(GROUP):
            c0 = (h * GROUP + g) * HEAD_DIM
            o_ref[:, :, c0:c0 + HEAD_DIM] = (
                o_h[g * n_new:(g + 1) * n_new].reshape(n_steps, bb, HEAD_DIM).astype(o_ref.dtype))


def _attn_sample(q3, kvn3, cache_k, cache_v, sinks):
    n_steps, b, _ = q3.shape
    bb = SAMPLE_BATCH_TILE
    return pl.pallas_call(
        functools.partial(_attn_sample_body, n_steps=n_steps),
        grid=(b // bb,),
        in_specs=[
            pl.BlockSpec(memory_space=pltpu.SMEM),
            pl.BlockSpec((n_steps, bb, Q_DIM), lambda i: (0, i, 0)),
            pl.BlockSpec((n_steps, bb, 2 * KV_DIM), lambda i: (0, i, 0)),
            pl.BlockSpec((bb, WINDOW, KV_DIM), lambda i: (i, 0, 0)),
            pl.BlockSpec((bb, WINDOW, KV_DIM), lambda i: (i, 0, 0)),
        ],
        out_specs=pl.BlockSpec((n_steps, bb, Q_DIM), lambda i: (0, i, 0)),
        out_shape=jax.ShapeDtypeStruct((n_steps, b, Q_DIM), BF16),
        compiler_params=_params(("arbitrary",)),
        name="attn_sample",
    )(sinks, q3, kvn3, cache_k, cache_v)


TILES = {
    "rec_in": dict(bm=512, bn=2048),
    "rec_out": dict(bm=512, bn=2048),
    "attn_out": dict(bm=512, bn=2048),
    "mlp_up": dict(bm=1024, bn=2048),
    "mlp_down": dict(bm=1024, bn=1024, bk=2048),
}


def _mm(name, x, w, **kw):
    return _matmul(x, w, name=name, **TILES[name], **kw)


def _flat(w):
    return w.reshape(-1, w.shape[-1])


def _trunk(z, x, attn_fn, raw, w):
    h, hb, ssq = _mm("rec_out", z, w["rec_w_out"], out_dtype=F32, resid=x, bf16_copy=True, ssq_out=True)
    if "mlp_w_down" not in w:
        hid, w_down, w_kv, w_q, w_o = _mm("mlp_up", hb, w["mlp_w_up"], layer=0, out_dtype=BF16, act="relu2",
                                          ssq_in=ssq, side_casts=[
            (_flat(raw["mlp_w_down"]), None), (raw["w_kv"], raw["kv_norm"]),
            (_flat(raw["attn_w_q"]), raw["norm_mix"][1]), (_flat(raw["attn_w_o"]), None)])
        w["mlp_w_down"] = w_down.reshape(raw["mlp_w_down"].shape)
        w["w_kv"] = w_kv
        w["attn_w_q"] = w_q
        w["attn_w_o"] = w_o.reshape(raw["attn_w_o"].shape)
    else:
        hid = _mm("mlp_up", hb, w["mlp_w_up"], layer=0, out_dtype=BF16, act="relu2", ssq_in=ssq)
    h = _mm("mlp_down", hid, w["mlp_w_down"], layer=0, out_dtype=F32, resid=h)
    q, kv = _qkv_proj(h, w["attn_w_q"], w["w_kv"])
    o = attn_fn(q, kv)
    h, hb, ssq = _mm("attn_out", o, w["attn_w_o"], out_dtype=F32, resid=h, bf16_copy=True, ssq_out=True)
    hid = _mm("mlp_up", hb, w["mlp_w_up"], layer=1, out_dtype=BF16, act="relu2", ssq_in=ssq)
    h = _mm("mlp_down", hid, w["mlp_w_down"], layer=1, out_dtype=F32, resid=h)
    return _rmsnorm(h, raw["final_norm"], F32), kv


def kernel(x_prompt, x_sample, state_conv, state_h, cache_k, cache_v, norm_mix, norm_mlp, rec_w_in,
           rec_conv_w, rec_conv_b, rec_gate_a_w, rec_gate_a_b, rec_gate_x_w, rec_gate_x_b, rec_lambda,
           rec_w_out, kv_norm, w_kv, attn_w_q, attn_sinks, attn_w_o, mlp_w_up, mlp_w_down, final_norm):
    b, t, d = x_prompt.shape
    sb, st, _ = x_sample.shape
    c = D_RNN
    n_tail = CONV_W - 1

    raw = {
        "norm_mix": norm_mix, "kv_norm": kv_norm, "final_norm": final_norm,
        "w_kv": w_kv, "attn_w_q": attn_w_q, "attn_w_o": attn_w_o, "mlp_w_down": mlp_w_down,
    }
    w_in = rec_w_in[0].astype(BF16)
    w = {}
    conv_w = rec_conv_w[0]
    conv_b = rec_conv_b[0].reshape(1, c)
    lam = rec_lambda[0].reshape(1, c)
    w_gates = jnp.concatenate([rec_gate_a_w[0], rec_gate_x_w[0]], axis=-1).astype(BF16)
    b_gates = jnp.concatenate([rec_gate_a_b[0], rec_gate_x_b[0]], axis=-1).reshape(N_LRU_BLOCKS, 1, 2 * LRU_BLOCK)
    sinks = attn_sinks[0].astype(F32)

    def attn_prompt(q, kv):
        o = _attn_prompt(q.reshape(b, t, Q_DIM), kv.reshape(b, t, 2 * KV_DIM), sinks)
        return o.reshape(b * t, Q_DIM)

    z_p, h_p, tail_p, w_out, w_up = _rec_front(
        x_prompt, norm_mix[0], w_in, conv_w, conv_b, w_gates, b_gates, lam,
        side_casts=[(_flat(rec_w_out), None), (_flat(mlp_w_up), norm_mlp.reshape(-1))])
    w["rec_w_out"] = w_out.reshape(rec_w_out.shape)
    w["mlp_w_up"] = w_up.reshape(mlp_w_up.shape)
    y_p, kv_p = _trunk(z_p.reshape(b * t, c), x_prompt.reshape(b * t, d), attn_prompt, raw, w)
    y_prompt = y_p.reshape(b, t, d)
    conv_p = jnp.swapaxes(tail_p.reshape(n_tail, b, c), 0, 1).reshape(b, 1, n_tail, c)
    keep = min(WINDOW, t)
    kv_tail = kv_p.reshape(b, t, 2 * KV_DIM)[:, t - keep:]
    new_k_prompt = kv_tail[..., :KV_DIM].reshape(b, keep, N_KV_HEADS, HEAD_DIM)
    new_v_prompt = kv_tail[..., KV_DIM:].reshape(b, keep, N_KV_HEADS, HEAD_DIM)

    conv_state_tb = jnp.swapaxes(state_conv[:, 0], 0, 1).reshape(n_tail * sb, c)
    h0 = state_h[:, 0]
    ck = cache_k.reshape(sb, WINDOW, KV_DIM)
    cv = cache_v.reshape(sb, WINDOW, KV_DIM)

    def attn_sample(q, kv):
        o = _attn_sample(q.reshape(st, sb, Q_DIM), kv.reshape(st, sb, 2 * KV_DIM), ck, cv, sinks)
        return o.reshape(st * sb, Q_DIM)

    x_s = jnp.swapaxes(x_sample, 0, 1).reshape(st * sb, d)
    gx_s = _mm("rec_in", _rmsnorm(x_s, norm_mix[0], BF16), w_in[None], out_dtype=F32)
    z_s, h_s = _lru(gx_s, conv_state_tb, h0, conv_w, conv_b, w_gates, b_gates, lam, n_batch=sb, n_steps=st)
    y_s, kv_s = _trunk(z_s, x_s, attn_sample, raw, w)
    y_sample = jnp.swapaxes(y_s.reshape(st, sb, d), 0, 1)
    conv_s = jnp.swapaxes(gx_s.reshape(st, sb, 2 * c)[st - n_tail:, :, c:], 0, 1).reshape(sb, 1, n_tail, c)
    kv_s = jnp.swapaxes(kv_s.reshape(st, sb, 2 * KV_DIM), 0, 1)
    new_k_sample = kv_s[..., :KV_DIM].reshape(sb, st, N_KV_HEADS, HEAD_DIM)
    new_v_sample = kv_s[..., KV_DIM:].reshape(sb, st, N_KV_HEADS, HEAD_DIM)

    return (y_prompt, y_sample, conv_p, h_p.reshape(b, 1, c), new_k_prompt, new_v_prompt,
            conv_s, h_s.reshape(sb, 1, c), new_k_sample, new_v_sample)
```

```python
import functools
import math

import jax
import jax.numpy as jnp
from jax import lax
from jax.experimental import pallas as pl
from jax.experimental.pallas import tpu as pltpu

F32 = jnp.float32
BF16 = jnp.bfloat16

D_MODEL = 2048
D_RNN = 2048
N_LRU_BLOCKS = 16
LRU_BLOCK = D_RNN // N_LRU_BLOCKS
CONV_W = 4
LRU_C = 8.0
N_HEADS = 32
N_KV_HEADS = 4
HEAD_DIM = 64
GROUP = N_HEADS // N_KV_HEADS
Q_DIM = N_HEADS * HEAD_DIM
KV_DIM = N_KV_HEADS * HEAD_DIM
WINDOW = 128
EPS = 1e-6
LOG2E = math.log2(math.e)

SUBLANES = 8
LANES = 128
VMEM_LIMIT_BYTES = 60 * 1024 * 1024


def _params(semantics):
    return pltpu.CompilerParams(dimension_semantics=semantics,
                                vmem_limit_bytes=VMEM_LIMIT_BYTES)


def _rmsnorm_body(x_ref, g_ref, o_ref):
    x = x_ref[...]
    y = x * lax.rsqrt(jnp.mean(x * x, axis=-1, keepdims=True) + EPS)
    o_ref[...] = (y * g_ref[...]).astype(o_ref.dtype)


def _rmsnorm(x, gain, out_dtype, tm=1024):
    m, d = x.shape
    tm = min(tm, m)
    row_spec = pl.BlockSpec((tm, d), lambda i: (i, 0))
    return pl.pallas_call(
        _rmsnorm_body,
        grid=(m // tm,),
        in_specs=[row_spec, pl.BlockSpec((1, d), lambda i: (0, 0))],
        out_specs=row_spec,
        out_shape=jax.ShapeDtypeStruct((m, d), out_dtype),
        compiler_params=_params(("arbitrary",)),
        name="rmsnorm",
    )(x, gain.reshape(1, d).astype(F32))


PERM_STEPS = 32


def _perm_matrix(n_batch, to_time_major):
    n = n_batch * PERM_STEPS
    r = lax.broadcasted_iota(jnp.int32, (n, n), 0)
    c = lax.broadcasted_iota(jnp.int32, (n, n), 1)
    if to_time_major:
        src = (r % n_batch) * PERM_STEPS + r // n_batch
    else:
        src = (r % PERM_STEPS) * n_batch + r // PERM_STEPS
    return jnp.where(c == src, 1.0, 0.0).astype(BF16)


def _matmul_body(*refs, nk, act, has_scale, has_resid, has_copy, has_ssq, side_gains, norm_dim):
    refs = list(refs)
    x_ref, w_ref = refs[:2]
    pos = 2
    s_ref = r_ref = b_ref = q_ref = None
    if has_scale:
        s_ref = refs[pos]
        pos += 1
    if has_resid:
        r_ref = refs[pos]
        pos += 1
    side_in = []
    for has_gain in side_gains:
        side_in.append((refs[pos], refs[pos + 1] if has_gain else None))
        pos += 2 if has_gain else 1
    o_ref = refs[pos]
    pos += 1
    if has_copy:
        b_ref = refs[pos]
        pos += 1
    if has_ssq:
        q_ref = refs[pos]
        pos += 1
    side_out = refs[pos:pos + len(side_gains)]
    pos += len(side_gains)
    acc_ref = refs[pos] if nk > 1 else None

    def partial_product():
        for (src_ref, g_ref), dst_ref in zip(side_in, side_out):
            chunk = src_ref[...]
            if g_ref is not None:
                chunk = chunk * g_ref[...]
            dst_ref[...] = chunk.astype(dst_ref.dtype)
        return jnp.dot(x_ref[...], w_ref[...], preferred_element_type=F32)

    def finish(acc):
        if s_ref is not None:
            acc = acc * lax.rsqrt(jnp.sum(s_ref[...], axis=-1, keepdims=True) * (1.0 / norm_dim) + EPS)
        if act == "relu2":
            r = jnp.maximum(acc, 0.0)
            acc = r * r
        if r_ref is not None:
            acc = r_ref[...] + acc
        o_ref[...] = acc.astype(o_ref.dtype)
        if b_ref is not None:
            b_ref[...] = acc.astype(b_ref.dtype)
        if q_ref is not None:
            sq = acc * acc
            tot = sq[:, 0:LANES]
            for c in range(1, sq.shape[1] // LANES):
                tot = tot + sq[:, c * LANES:(c + 1) * LANES]
            q_ref[...] = tot

    if nk == 1:
        finish(partial_product())
    else:
        k = pl.program_id(2)

        @pl.when(k == 0)
        def _():
            acc_ref[...] = partial_product()

        @pl.when(jnp.logical_and(k > 0, k < nk - 1))
        def _():
            acc_ref[...] += partial_product()

        @pl.when(k == nk - 1)
        def _():
            finish(acc_ref[...] + partial_product())


def _matmul(x, w, *, out_dtype, layer=0, act=None, ssq_in=None, resid=None, bf16_copy=False, ssq_out=False,
            side_casts=(), cols_outer=False, bm=1024, bn=1024, bk=None, name="matmul"):
    m = x.shape[0]
    _, kdim, n = w.shape
    bm, bn = min(bm, m), min(bn, n)
    bk = kdim if bk is None else min(bk, kdim)
    nk, nb = kdim // bk, n // bn
    mb = m // bm
    n_steps = mb * nb * nk

    def spec(shape, index_map, **kw):
        if cols_outer:
            return pl.BlockSpec(shape, lambda j, i, k: index_map(i, j, k), **kw)
        return pl.BlockSpec(shape, index_map, **kw)

    in_specs = [spec((bm, bk), lambda i, j, k: (i, k)),
                spec((None, bk, bn), lambda i, j, k: (layer, k, j))]
    args = [x, w]
    if ssq_in is not None:
        in_specs.append(spec((bm, ssq_in.shape[1]), lambda i, j, k: (i, 0)))
        args.append(ssq_in)
    tile_spec = spec((bm, bn), lambda i, j, k: (i, j), **(dict(pipeline_mode=pl.Buffered(1)) if nk > 1 else {}))
    if resid is not None:
        in_specs.append(tile_spec)
        args.append(resid)
    step = lambda i, j, k: (i * nb + j) * nk + k
    for src, gain in side_casts:
        rows, cols = src.shape
        chunk = rows // n_steps
        assert chunk * n_steps == rows and chunk % (2 * SUBLANES) == 0, (name, rows, n_steps)
        in_specs.append(spec((chunk, cols), lambda i, j, k: (step(i, j, k), 0)))
        args.append(src)
        if gain is not None:
            in_specs.append(spec((chunk, 1), lambda i, j, k: (step(i, j, k), 0)))
            args.append(gain.reshape(rows, 1).astype(F32))

    out_specs = [tile_spec]
    out_shape = [jax.ShapeDtypeStruct((m, n), out_dtype)]
    if bf16_copy:
        out_specs.append(tile_spec)
        out_shape.append(jax.ShapeDtypeStruct((m, n), BF16))
    if ssq_out:
        out_specs.append(spec((bm, LANES), lambda i, j, k: (i, j)))
        out_shape.append(jax.ShapeDtypeStruct((m, nb * LANES), F32))
    for src, _ in side_casts:
        rows, cols = src.shape
        out_specs.append(spec((rows // n_steps, cols), lambda i, j, k: (step(i, j, k), 0)))
        out_shape.append(jax.ShapeDtypeStruct((rows, cols), BF16))

    outs = pl.pallas_call(
        functools.partial(_matmul_body, nk=nk, act=act, has_scale=ssq_in is not None,
                          has_resid=resid is not None, has_copy=bf16_copy, has_ssq=ssq_out,
                          side_gains=tuple(g is not None for _, g in side_casts), norm_dim=kdim),
        grid=(nb, mb, nk) if cols_outer else (mb, nb, nk),
        in_specs=in_specs,
        out_specs=out_specs,
        out_shape=out_shape,
        scratch_shapes=[pltpu.VMEM((bm, bn), F32)] if nk > 1 else [],
        compiler_params=_params(("arbitrary", "arbitrary", "arbitrary")),
        name=name,
    )(*args)
    return outs[0] if len(outs) == 1 else outs


def _qkv_body(x_ref, wq_ref, wkv_ref, q_ref, kv_ref):
    xf = x_ref[...]
    scale = lax.rsqrt(jnp.mean(xf * xf, axis=-1, keepdims=True) + EPS)
    x = xf.astype(BF16)
    q = jnp.dot(x, wq_ref[...], preferred_element_type=F32)
    q_ref[...] = (q * (scale * (HEAD_DIM ** -0.5 * LOG2E))).astype(q_ref.dtype)
    kv_ref[...] = jnp.dot(x, wkv_ref[...], preferred_element_type=F32) * scale


def _qkv_proj(x, w_q, w_kv, bm=1024):
    m, kdim = x.shape
    bm = min(bm, m)
    return pl.pallas_call(
        _qkv_body,
        grid=(m // bm,),
        in_specs=[pl.BlockSpec((bm, kdim), lambda i: (i, 0)),
                  pl.BlockSpec((kdim, Q_DIM), lambda i: (0, 0), pipeline_mode=pl.Buffered(1)),
                  pl.BlockSpec((kdim, 2 * KV_DIM), lambda i: (0, 0), pipeline_mode=pl.Buffered(1))],
        out_specs=[pl.BlockSpec((bm, Q_DIM), lambda i: (i, 0)),
                   pl.BlockSpec((bm, 2 * KV_DIM), lambda i: (i, 0))],
        out_shape=[jax.ShapeDtypeStruct((m, Q_DIM), BF16), jax.ShapeDtypeStruct((m, 2 * KV_DIM), F32)],
        compiler_params=_params(("arbitrary",)),
        name="qkv_proj",
    )(x, w_q, w_kv)


def _gelu_tanh(x):
    return x * (0.5 + 0.5 * jnp.tanh(0.7978845608028654 * (x + 0.044715 * (x * x * x))))


def _softplus(z):
    return jnp.maximum(z, 0.0) + jnp.log1p(jnp.exp(-jnp.abs(z)))


def _lru_decay(lam):
    return (-0.5 * LRU_C * LOG2E) * _softplus(-lam)


def _lru_columns(gate, x, tail, h, cw, cb, wg, bg, d, *, n_steps, n_batch):
    n_rows = n_steps * n_batch
    x_ext = jnp.concatenate([tail, x], axis=0)
    xc = cb + cw[0:1] * x_ext[0:n_rows]
    for j in range(1, CONV_W):
        xc = xc + cw[j:j + 1] * x_ext[j * n_batch:j * n_batch + n_rows]
    half_pre = 0.5 * (jnp.dot(xc.astype(BF16), wg, preferred_element_type=F32) + bg)
    tr = jnp.tanh(half_pre[:, :LRU_BLOCK])
    ti = jnp.tanh(half_pre[:, LRU_BLOCK:])
    a = jnp.exp2(d * tr + d)
    gain2 = 1.0 - a * a
    u = (gain2 * lax.rsqrt(jnp.maximum(gain2, 1e-30))) * ((0.5 + 0.5 * ti) * xc)
    hs = []
    for s in range(n_steps):
        rows = slice(s * n_batch, (s + 1) * n_batch)
        h = a[rows] * h + u[rows]
        hs.append(h)
    return _gelu_tanh(gate) * jnp.concatenate(hs, axis=0), x_ext[n_rows:], h


def _lru_body(gate_ref, xr_ref, cs_ref, h0_ref, cw_ref, cb_ref, wg_ref, bg_ref, lam_ref,
              z_ref, ht_ref, h_sc, tail_sc, *, n_steps, n_batch, tc_cols):
    t = pl.program_id(1)

    @pl.when(t == 0)
    def _():
        h_sc[...] = h0_ref[...]
        tail_sc[...] = cs_ref[...]

    decay = _lru_decay(lam_ref[...])
    for n in range(tc_cols // LRU_BLOCK):
        cols = slice(n * LRU_BLOCK, (n + 1) * LRU_BLOCK)
        z, tail_sc[:, cols], h_sc[:, cols] = _lru_columns(
            gate_ref[:, cols], xr_ref[:, cols], tail_sc[:, cols], h_sc[:, cols], cw_ref[:, cols], cb_ref[:, cols],
            wg_ref[n], bg_ref[n], decay[:, cols], n_steps=n_steps, n_batch=n_batch)
        z_ref[:, cols] = z.astype(z_ref.dtype)

    @pl.when(t == pl.num_programs(1) - 1)
    def _():
        ht_ref[...] = h_sc[...]


def _lru(gx, conv_state, h0, conv_w, conv_b, w_gates, b_gates, lam, *, n_batch, n_steps, tc_cols=512):
    m = gx.shape[0]
    c = D_RNN
    n_rows = n_steps * n_batch
    n_tail = (CONV_W - 1) * n_batch
    ncb = c // tc_cols
    gpb = tc_cols // LRU_BLOCK
    z_spec = pl.BlockSpec((n_rows, tc_cols), lambda ci, ti: (ti, ci))
    z_shape = jax.ShapeDtypeStruct((m, c), BF16)
    z, ht = pl.pallas_call(
        functools.partial(_lru_body, n_steps=n_steps, n_batch=n_batch, tc_cols=tc_cols),
        grid=(ncb, m // n_rows),
        in_specs=[
            pl.BlockSpec((n_rows, tc_cols), lambda ci, ti: (ti, ci)),
            pl.BlockSpec((n_rows, tc_cols), lambda ci, ti: (ti, ncb + ci)),
            pl.BlockSpec((n_tail, tc_cols), lambda ci, ti: (0, ci)),
            pl.BlockSpec((n_batch, tc_cols), lambda ci, ti: (0, ci)),
            pl.BlockSpec((CONV_W, tc_cols), lambda ci, ti: (0, ci)),
            pl.BlockSpec((1, tc_cols), lambda ci, ti: (0, ci)),
            pl.BlockSpec((gpb, LRU_BLOCK, 2 * LRU_BLOCK), lambda ci, ti: (ci, 0, 0)),
            pl.BlockSpec((gpb, 1, 2 * LRU_BLOCK), lambda ci, ti: (ci, 0, 0)),
            pl.BlockSpec((1, tc_cols), lambda ci, ti: (0, ci)),
        ],
        out_specs=[z_spec, pl.BlockSpec((n_batch, tc_cols), lambda ci, ti: (0, ci))],
        out_shape=[z_shape, jax.ShapeDtypeStruct((n_batch, c), F32)],
        scratch_shapes=[pltpu.VMEM((n_batch, tc_cols), F32), pltpu.VMEM((n_tail, tc_cols), F32)],
        compiler_params=_params(("arbitrary", "arbitrary")),
        name="lru",
    )(gx, gx, conv_state, h0, conv_w, conv_b, w_gates, b_gates, lam)
    return z, ht


FRONT_STEPS = 2 * PERM_STEPS


def _rec_front_body(*refs, n_batch, side_gains):
    x_ref, norm_gain_ref, win_ref, cw_ref, cb_ref, wg_ref, bg_ref, lam_ref = refs[:8]
    pos = 8
    side_in = []
    for has_gain in side_gains:
        side_in.append((refs[pos], refs[pos + 1] if has_gain else None))
        pos += 2 if has_gain else 1
    z_ref, ht_ref, tail_ref = refs[pos:pos + 3]
    pos += 3
    side_out = refs[pos:pos + len(side_gains)]
    h_sc, tail_sc = refs[pos + len(side_gains):]
    s = pl.program_id(0)
    c = D_RNN

    @pl.when(s == 0)
    def _():
        h_sc[...] = jnp.zeros_like(h_sc)
        tail_sc[...] = jnp.zeros_like(tail_sc)

    for (src_ref, g_ref), dst_ref in zip(side_in, side_out):
        chunk = src_ref[...]
        if g_ref is not None:
            chunk = chunk * g_ref[...]
        dst_ref[...] = chunk.astype(dst_ref.dtype)

    p_rows = n_batch * PERM_STEPS
    n_tiles = FRONT_STEPS // PERM_STEPS
    to_tb = _perm_matrix(n_batch, True)
    xn = []
    for tile in range(n_tiles):
        x = x_ref[:, tile * PERM_STEPS:(tile + 1) * PERM_STEPS, :].reshape(p_rows, x_ref.shape[-1])
        y = x * lax.rsqrt(jnp.mean(x * x, axis=-1, keepdims=True) + EPS)
        xn.append(jnp.dot(to_tb, (y * norm_gain_ref[...]).astype(BF16), preferred_element_type=F32).astype(BF16))
    gx = jnp.dot(jnp.concatenate(xn, axis=0), win_ref[...], preferred_element_type=F32)

    to_bt = _perm_matrix(n_batch, False)
    decay = _lru_decay(lam_ref[...])
    for n in range(N_LRU_BLOCKS):
        cols = slice(n * LRU_BLOCK, (n + 1) * LRU_BLOCK)
        z, tail_sc[:, cols], h_sc[:, cols] = _lru_columns(
            gx[:, n * LRU_BLOCK:(n + 1) * LRU_BLOCK], gx[:, c + n * LRU_BLOCK:c + (n + 1) * LRU_BLOCK],
            tail_sc[:, cols], h_sc[:, cols], cw_ref[:, cols], cb_ref[:, cols], wg_ref[n], bg_ref[n], decay[:, cols],
            n_steps=FRONT_STEPS, n_batch=n_batch)
        z = z.astype(BF16)
        for tile in range(n_tiles):
            zp = jnp.dot(to_bt, z[tile * p_rows:(tile + 1) * p_rows], preferred_element_type=F32)
            z_ref[:, tile * PERM_STEPS:(tile + 1) * PERM_STEPS, cols] = (
                zp.astype(z_ref.dtype).reshape(n_batch, PERM_STEPS, LRU_BLOCK))

    @pl.when(s == pl.num_programs(0) - 1)
    def _():
        ht_ref[...] = h_sc[...]
        tail_ref[...] = tail_sc[...]


def _rec_front(x3, gain, w_in, conv_w, conv_b, w_gates, b_gates, lam, *, side_casts=()):
    n_batch, t, d = x3.shape
    c = D_RNN
    n_blocks = t // FRONT_STEPS
    n_tail = (CONV_W - 1) * n_batch
    const = lambda *shape: pl.BlockSpec(shape, lambda s: (0,) * len(shape))

    in_specs = [pl.BlockSpec((n_batch, FRONT_STEPS, d), lambda s: (0, s, 0)), const(1, d),
                pl.BlockSpec((d, 2 * c), lambda s: (0, 0), pipeline_mode=pl.Buffered(1)),
                const(CONV_W, c), const(1, c), const(N_LRU_BLOCKS, LRU_BLOCK, 2 * LRU_BLOCK),
                const(N_LRU_BLOCKS, 1, 2 * LRU_BLOCK), const(1, c)]
    args = [x3, gain.reshape(1, d).astype(F32), w_in, conv_w, conv_b, w_gates, b_gates, lam]
    out_specs = [pl.BlockSpec((n_batch, FRONT_STEPS, c), lambda s: (0, s, 0)), const(n_batch, c), const(n_tail, c)]
    out_shape = [jax.ShapeDtypeStruct((n_batch, t, c), BF16),
                 jax.ShapeDtypeStruct((n_batch, c), F32), jax.ShapeDtypeStruct((n_tail, c), F32)]
    for src, gain in side_casts:
        rows, cols = src.shape
        chunk = rows // n_blocks
        assert chunk * n_blocks == rows and chunk % (2 * SUBLANES) == 0, (rows, n_blocks)
        in_specs.append(pl.BlockSpec((chunk, cols), lambda s: (s, 0)))
        args.append(src)
        if gain is not None:
            in_specs.append(pl.BlockSpec((chunk, 1), lambda s: (s, 0)))
            args.append(gain.reshape(rows, 1).astype(F32))
        out_specs.append(pl.BlockSpec((chunk, cols), lambda s: (s, 0)))
        out_shape.append(jax.ShapeDtypeStruct((rows, cols), BF16))

    return pl.pallas_call(
        functools.partial(_rec_front_body, n_batch=n_batch, side_gains=tuple(g is not None for _, g in side_casts)),
        grid=(n_blocks,),
        in_specs=in_specs,
        out_specs=out_specs,
        out_shape=out_shape,
        scratch_shapes=[pltpu.VMEM((n_batch, c), F32), pltpu.VMEM((n_tail, c), F32)],
        compiler_params=_params(("arbitrary",)),
        name="rec_front",
    )(*args)


ATTN_BLOCKS = 4


def _attn_prompt_body(sink_ref, q_ref, kvc_ref, kvp_ref, o_ref):
    i = pl.program_id(1)
    for blk in range(ATTN_BLOCKS):
        rows = slice(blk * WINDOW, (blk + 1) * WINDOW)
        prev = kvp_ref[...] if blk == 0 else kvc_ref[(blk - 1) * WINDOW:blk * WINDOW]
        kv = jnp.concatenate([prev, kvc_ref[rows]], axis=0).astype(BF16)
        _attend_block(sink_ref, q_ref, o_ref, rows, kv, jnp.where(i == 0, -jnp.inf, 0.0) if blk == 0 else None)


def _attend_block(sink_ref, q_ref, o_ref, rows, kv, prev_bias):
    n_keys = 2 * WINDOW
    low_k = lax.broadcasted_iota(jnp.int32, (n_keys, LANES), 1) < HEAD_DIM
    low_q = lax.broadcasted_iota(jnp.int32, (WINDOW, LANES), 1) < HEAD_DIM
    zero = jnp.zeros((n_keys, LANES), BF16)
    one = jnp.ones((n_keys, LANES), BF16)
    from_prev = (lax.broadcasted_iota(jnp.int32, (WINDOW, WINDOW), 1)
                 > lax.broadcasted_iota(jnp.int32, (WINDOW, WINDOW), 0))
    dims = (((1,), (1,)), ((), ()))

    n_pairs = GROUP // 2

    def score_head(h):
        c0 = (h // 2) * LANES
        k_t = kv[:, c0:c0 + LANES]
        v_t = kv[:, KV_DIM + c0:KV_DIM + c0 + LANES]
        k_sw = pltpu.roll(k_t, HEAD_DIM, 1)
        v_sw = pltpu.roll(v_t, HEAD_DIM, 1)
        k_lo, k_hi = (k_t, k_sw) if h % 2 == 0 else (k_sw, k_t)
        v_lo, v_hi = (v_t, v_sw) if h % 2 == 0 else (v_sw, v_t)
        k_a = jnp.where(low_k, k_lo, zero)
        k_b = jnp.where(low_k, zero, k_hi)
        v_a = jnp.where(low_k, v_lo, one)
        v_b = jnp.where(low_k, one, v_hi)
        qc0 = h * GROUP * HEAD_DIM
        q_h = jnp.concatenate([q_ref[rows, qc0 + p * LANES:qc0 + (p + 1) * LANES] for p in range(n_pairs)], axis=0)
        scores = [lax.dot_general(q_h, k_x, dims, preferred_element_type=F32) for k_x in (k_a, k_b)]
        return scores, (v_a, v_b)

    for h in range(N_KV_HEADS):
        scores, values = score_head(h)
        qc0 = h * GROUP * HEAD_DIM
        acc, esink = [], []
        for side, v_x in enumerate(values):
            s = scores[side]
            probs, es = [], []
            for p in range(n_pairs):
                s_p = s[p * WINDOW:(p + 1) * WINDOW]
                s_prev = s_p[:, :WINDOW] if prev_bias is None else s_p[:, :WINDOW] + prev_bias
                s_v = jnp.where(from_prev, s_prev, s_p[:, WINDOW:])
                sink = sink_ref[h * GROUP + 2 * p + side] * LOG2E
                m = jnp.maximum(jnp.max(s_v, axis=-1, keepdims=True), sink)
                e = jnp.exp2(s_v - m)
                probs.append(jnp.concatenate([jnp.where(from_prev, e, 0.0), jnp.where(from_prev, 0.0, e)],
                                             axis=1).astype(BF16))
                es.append(jnp.exp2(sink - m))
            acc.append(jnp.dot(jnp.concatenate(probs, axis=0), v_x, preferred_element_type=F32))
            esink.append(es)
        for p in range(n_pairs):
            part = slice(p * WINDOW, (p + 1) * WINDOW)
            acc_a, acc_b = acc[0][part], acc[1][part]
            num = jnp.where(low_q, acc_a, acc_b)
            den = pltpu.roll(jnp.where(low_q, acc_b, acc_a), HEAD_DIM, 1) + jnp.where(low_q, esink[0][p], esink[1][p])
            o_ref[rows, qc0 + p * LANES:qc0 + (p + 1) * LANES] = (num / den).astype(o_ref.dtype)


def _attn_prompt(q3, kv3, sinks):
    b, t, _ = q3.shape
    step_rows = ATTN_BLOCKS * WINDOW
    nb = t // step_rows
    return pl.pallas_call(
        _attn_prompt_body,
        grid=(b, nb),
        in_specs=[
            pl.BlockSpec(memory_space=pltpu.SMEM),
            pl.BlockSpec((None, step_rows, Q_DIM), lambda bi, i: (bi, i, 0)),
            pl.BlockSpec((None, step_rows, 2 * KV_DIM), lambda bi, i: (bi, i, 0)),
            pl.BlockSpec((None, WINDOW, 2 * KV_DIM), lambda bi, i: (bi, jnp.maximum(ATTN_BLOCKS * i - 1, 0), 0)),
        ],
        out_specs=pl.BlockSpec((None, step_rows, Q_DIM), lambda bi, i: (bi, i, 0)),
        out_shape=jax.ShapeDtypeStruct((b, t, Q_DIM), BF16),
        compiler_params=_params(("arbitrary", "arbitrary")),
        name="attn_prompt",
    )(sinks, q3, kv3, kv3)


SAMPLE_BATCH_TILE = 8


def _attn_sample_body(sink_ref, q_ref, kvn_ref, ck_ref, cv_ref, o_ref, *, n_steps):
    bb = SAMPLE_BATCH_TILE
    n_new = n_steps * bb
    n_cache = bb * WINDOW
    n_keys = n_cache + n_new
    n_keys_pad = -(-n_keys // LANES) * LANES
    q = q_ref[...].reshape(n_new, Q_DIM)
    kvn = kvn_ref[...].reshape(n_new, 2 * KV_DIM)
    pad = jnp.zeros((n_keys_pad - n_keys, KV_DIM), F32)
    k_all = jnp.concatenate([ck_ref[...].reshape(n_cache, KV_DIM), kvn[:, :KV_DIM], pad], axis=0).astype(BF16)
    v_all = jnp.concatenate([cv_ref[...].reshape(n_cache, KV_DIM), kvn[:, KV_DIM:], pad], axis=0).astype(BF16)
    n_rows = GROUP * n_new
    r = lax.broadcasted_iota(jnp.int32, (n_rows, n_keys_pad), 0)
    c = lax.broadcasted_iota(jnp.int32, (n_rows, n_keys_pad), 1)
    b_r = r % bb
    t_r = (r // bb) % n_steps
    is_cache = c < n_cache
    c2 = jnp.maximum(c - n_cache, 0)
    b_c = jnp.where(is_cache, c // WINDOW, c2 % bb)
    k_step = jnp.where(is_cache, c % WINDOW - WINDOW, c2 // bb)
    dq = t_r - k_step
    allowed = jnp.logical_and(jnp.logical_and(b_c == b_r, c < n_keys),
                              jnp.logical_and(dq >= 0, dq < WINDOW))
    grp = lax.broadcasted_iota(jnp.int32, (n_rows, 1), 0) // n_new
    dims = (((1,), (1,)), ((), ()))
    for h in range(N_KV_HEADS):
        k_h = k_all[:, h * HEAD_DIM:(h + 1) * HEAD_DIM]
        v_h = v_all[:, h * HEAD_DIM:(h + 1) * HEAD_DIM]
        q_h = jnp.concatenate(
            [q[:, (h * GROUP + g) * HEAD_DIM:(h * GROUP + g + 1) * HEAD_DIM] for g in range(GROUP)], axis=0)
        sink = jnp.zeros((n_rows, 1), F32)
        for g in range(GROUP):
            sink = jnp.where(grp == g, sink_ref[h * GROUP + g] * LOG2E, sink)
        s = lax.dot_general(q_h, k_h, dims, preferred_element_type=F32)
        s = jnp.where(allowed, s, -jnp.inf)
        m = jnp.maximum(jnp.max(s, axis=-1, keepdims=True), sink)
        p = jnp.exp2(s - m)
        denom = jnp.sum(p, axis=-1, keepdims=True) + jnp.exp2(sink - m)
        o_h = jnp.dot(p.astype(BF16), v_h, preferred_element_type=F32) / denom
        for g in range(GROUP):
            c0 = (h * GROUP + g) * HEAD_DIM
            o_ref[:, :, c0:c0 + HEAD_DIM] = (
                o_h[g * n_new:(g + 1) * n_new].reshape(n_steps, bb, HEAD_DIM).astype(o_ref.dtype))


def _attn_sample(q3, kvn3, cache_k, cache_v, sinks):
    n_steps, b, _ = q3.shape
    bb = SAMPLE_BATCH_TILE
    return pl.pallas_call(
        functools.partial(_attn_sample_body, n_steps=n_steps),
        grid=(b // bb,),
        in_specs=[
            pl.BlockSpec(memory_space=pltpu.SMEM),
            pl.BlockSpec((n_steps, bb, Q_DIM), lambda i: (0, i, 0)),
            pl.BlockSpec((n_steps, bb, 2 * KV_DIM), lambda i: (0, i, 0)),
            pl.BlockSpec((bb, WINDOW, KV_DIM), lambda i: (i, 0, 0)),
            pl.BlockSpec((bb, WINDOW, KV_DIM), lambda i: (i, 0, 0)),
        ],
        out_specs=pl.BlockSpec((n_steps, bb, Q_DIM), lambda i: (0, i, 0)),
        out_shape=jax.ShapeDtypeStruct((n_steps, b, Q_DIM), BF16),
        compiler_params=_params(("arbitrary",)),
        name="attn_sample",
    )(sinks, q3, kvn3, cache_k, cache_v)


TILES = {
    "rec_in": dict(bm=512, bn=2048),
    "rec_out": dict(bm=512, bn=2048),
    "attn_out": dict(bm=512, bn=2048),
    "mlp_up": dict(bm=1024, bn=2048),
    "mlp_down": dict(bm=1024, bn=1024, bk=4096),
}


def _mm(name, x, w, **kw):
    return _matmul(x, w, name=name, **TILES[name], **kw)


def _flat(w):
    return w.reshape(-1, w.shape[-1])


def _trunk(z, x, attn_fn, raw, w):
    h, hb, ssq = _mm("rec_out", z, w["rec_w_out"], out_dtype=F32, resid=x, bf16_copy=True, ssq_out=True)
    if "mlp_w_down" not in w:
        hid, w_down, w_kv, w_q, w_o = _mm("mlp_up", hb, w["mlp_w_up"], layer=0, out_dtype=BF16, act="relu2",
                                          ssq_in=ssq, side_casts=[
            (_flat(raw["mlp_w_down"]), None), (raw["w_kv"], raw["kv_norm"]),
            (_flat(raw["attn_w_q"]), raw["norm_mix"][1]), (_flat(raw["attn_w_o"]), None)])
        w["mlp_w_down"] = w_down.reshape(raw["mlp_w_down"].shape)
        w["w_kv"] = w_kv
        w["attn_w_q"] = w_q
        w["attn_w_o"] = w_o.reshape(raw["attn_w_o"].shape)
    else:
        hid = _mm("mlp_up", hb, w["mlp_w_up"], layer=0, out_dtype=BF16, act="relu2", ssq_in=ssq)
    h = _mm("mlp_down", hid, w["mlp_w_down"], layer=0, out_dtype=F32, resid=h)
    q, kv = _qkv_proj(h, w["attn_w_q"], w["w_kv"])
    o = attn_fn(q, kv)
    h, hb, ssq = _mm("attn_out", o, w["attn_w_o"], out_dtype=F32, resid=h, bf16_copy=True, ssq_out=True)
    hid = _mm("mlp_up", hb, w["mlp_w_up"], layer=1, out_dtype=BF16, act="relu2", ssq_in=ssq)
    h = _mm("mlp_down", hid, w["mlp_w_down"], layer=1, out_dtype=F32, resid=h)
    return _rmsnorm(h, raw["final_norm"], F32), kv


def kernel(x_prompt, x_sample, state_conv, state_h, cache_k, cache_v, norm_mix, norm_mlp, rec_w_in,
           rec_conv_w, rec_conv_b, rec_gate_a_w, rec_gate_a_b, rec_gate_x_w, rec_gate_x_b, rec_lambda,
           rec_w_out, kv_norm, w_kv, attn_w_q, attn_sinks, attn_w_o, mlp_w_up, mlp_w_down, final_norm):
    b, t, d = x_prompt.shape
    sb, st, _ = x_sample.shape
    c = D_RNN
    n_tail = CONV_W - 1

    raw = {
        "norm_mix": norm_mix, "kv_norm": kv_norm, "final_norm": final_norm,
        "w_kv": w_kv, "attn_w_q": attn_w_q, "attn_w_o": attn_w_o, "mlp_w_down": mlp_w_down,
    }
    w_in = rec_w_in[0].astype(BF16)
    w = {}
    conv_w = rec_conv_w[0]
    conv_b = rec_conv_b[0].reshape(1, c)
    lam = rec_lambda[0].reshape(1, c)
    w_gates = jnp.concatenate([rec_gate_a_w[0], rec_gate_x_w[0]], axis=-1).astype(BF16)
    b_gates = jnp.concatenate([rec_gate_a_b[0], rec_gate_x_b[0]], axis=-1).reshape(N_LRU_BLOCKS, 1, 2 * LRU_BLOCK)
    sinks = attn_sinks[0].astype(F32)

    def attn_prompt(q, kv):
        o = _attn_prompt(q.reshape(b, t, Q_DIM), kv.reshape(b, t, 2 * KV_DIM), sinks)
        return o.reshape(b * t, Q_DIM)

    z_p, h_p, tail_p, w_out, w_up = _rec_front(
        x_prompt, norm_mix[0], w_in, conv_w, conv_b, w_gates, b_gates, lam,
        side_casts=[(_flat(rec_w_out), None), (_flat(mlp_w_up), norm_mlp.reshape(-1))])
    w["rec_w_out"] = w_out.reshape(rec_w_out.shape)
    w["mlp_w_up"] = w_up.reshape(mlp_w_up.shape)
    y_p, kv_p = _trunk(z_p.reshape(b * t, c), x_prompt.reshape(b * t, d), attn_prompt, raw, w)
    y_prompt = y_p.reshape(b, t, d)
    conv_p = jnp.swapaxes(tail_p.reshape(n_tail, b, c), 0, 1).reshape(b, 1, n_tail, c)
    keep = min(WINDOW, t)
    kv_tail = kv_p.reshape(b, t, 2 * KV_DIM)[:, t - keep:]
    new_k_prompt = kv_tail[..., :KV_DIM].reshape(b, keep, N_KV_HEADS, HEAD_DIM)
    new_v_prompt = kv_tail[..., KV_DIM:].reshape(b, keep, N_KV_HEADS, HEAD_DIM)

    conv_state_tb = jnp.swapaxes(state_conv[:, 0], 0, 1).reshape(n_tail * sb, c)
    h0 = state_h[:, 0]
    ck = cache_k.reshape(sb, WINDOW, KV_DIM)
    cv = cache_v.reshape(sb, WINDOW, KV_DIM)

    def attn_sample(q, kv):
        o = _attn_sample(q.reshape(st, sb, Q_DIM), kv.reshape(st, sb, 2 * KV_DIM), ck, cv, sinks)
        return o.reshape(st * sb, Q_DIM)

    x_s = jnp.swapaxes(x_sample, 0, 1).reshape(st * sb, d)
    gx_s = _mm("rec_in", _rmsnorm(x_s, norm_mix[0], BF16), w_in[None], out_dtype=F32)
    z_s, h_s = _lru(gx_s, conv_state_tb, h0, conv_w, conv_b, w_gates, b_gates, lam, n_batch=sb, n_steps=st)
    y_s, kv_s = _trunk(z_s, x_s, attn_sample, raw, w)
    y_sample = jnp.swapaxes(y_s.reshape(st, sb, d), 0, 1)
    conv_s = jnp.swapaxes(gx_s.reshape(st, sb, 2 * c)[st - n_tail:, :, c:], 0, 1).reshape(sb, 1, n_tail, c)
    kv_s = jnp.swapaxes(kv_s.reshape(st, sb, 2 * KV_DIM), 0, 1)
    new_k_sample = kv_s[..., :KV_DIM].reshape(sb, st, N_KV_HEADS, HEAD_DIM)
    new_v_sample = kv_s[..., KV_DIM:].reshape(sb, st, N_KV_HEADS, HEAD_DIM)

    return (y_prompt, y_sample, conv_p, h_p.reshape(b, 1, c), new_k_prompt, new_v_prompt,
            conv_s, h_s.reshape(sb, 1, c), new_k_sample, new_v_sample)
```
